```python
import math
import jax, jax.numpy as jnp
from jax import lax
import numpy as np


D_MODEL = 1024
BATCH = 8
SEQ = 4096
DEPTH = 2
DEC_BATCH = 8
DEC_SEQ = 64
PAST_LEN = 1024

CHUNK = 64
Q_BLOCK = 2 * CHUNK
N_MIXERS = 2
N_ATT_LAYERS = (DEPTH + 1) // 2
N_RWKV_LAYERS = DEPTH // 2
N_HEADS = 16
HEAD_DIM = D_MODEL // N_HEADS
D_FF = 2816
N_EXPERTS = 8
TOP_K = 2
D_FF_EXPERT = 3584
DECAY_LORA = 64
AAA_LORA = 64
GATE_LORA = 128
N_SHIFT_MIX = 6
LN_EPS = 1e-5
GN_EPS = 64e-5
DEEPNORM_ALPHA = (2.0 * DEPTH) ** 0.25
DEEPNORM_BETA = (8.0 * DEPTH) ** -0.25

kernel_name = 'sb_rwkv7_deepnorm_stream_step'


def layer_norm(x, g, b):
    xf = x.astype(jnp.float32)
    mu = xf.mean(-1, keepdims=True)
    var = jnp.square(xf - mu).mean(-1, keepdims=True)
    return ((xf - mu) * lax.rsqrt(var + LN_EPS) * g + b).astype(x.dtype)


def split_heads(z):
    b, t, _ = z.shape
    return z.reshape(b, t, N_HEADS, HEAD_DIM).transpose(0, 2, 1, 3)


def merge_heads(z):
    b, h, t, d = z.shape
    return z.transpose(0, 2, 1, 3).reshape(b, t, h * d)


def sb_project(x, w_qkv):
    q, k, v = jnp.split(x @ w_qkv, 3, axis=-1)
    return split_heads(q), split_heads(k), split_heads(v)


def sb_block(q_blk, k, v, q_start):
    z = jnp.einsum('bhqd,bhkd->bhqk', q_blk, k,
                   preferred_element_type=jnp.float32) * (HEAD_DIM ** -0.5)
    qpos = q_start + jnp.arange(q_blk.shape[2])
    kpos = jnp.arange(k.shape[2])
    mask = kpos[None, :] < qpos[:, None]
    log_stay = jnp.where(mask, jax.nn.log_sigmoid(-z), 0.0)
    between = lax.cumsum(log_stay, axis=3, reverse=True) - log_stay
    a = jnp.where(mask, jnp.exp(jax.nn.log_sigmoid(z) + between), 0.0)
    return jnp.einsum('bhqk,bhkd->bhqd', a.astype(v.dtype), v)


def sb_prompt(q, k, v):
    b, h, t, d = q.shape
    nblk = t // Q_BLOCK
    qb = q.reshape(b, h, nblk, Q_BLOCK, d).transpose(2, 0, 1, 3, 4)
    starts = jnp.arange(nblk) * Q_BLOCK
    out = lax.map(lambda a: sb_block(a[0], k, v, a[1]), (qb, starts))
    return out.transpose(1, 2, 0, 3, 4).reshape(b, h, t, d)


def to_heads(z):
    return z.reshape(z.shape[0], z.shape[1], N_HEADS, HEAD_DIM).astype(jnp.float32)


def rwkv7_time_mix(x, x_last, s0, mu, w_rkv, w0, w1, w2, a0, a1, a2, g1, g2,
                   k_k, k_a, r_k, gn_g, gn_b, w_o):
    b, t, d = x.shape
    x_prev = jnp.concatenate([x_last[:, None, :].astype(x.dtype), x[:, :-1]], axis=1)
    xx = x_prev - x
    xr, xw, xk, xv, xa, xg = [x + xx * mu[i] for i in range(N_SHIFT_MIX)]
    r, k, v = jnp.einsum('nbtd,nde->nbte', jnp.stack([xr, xk, xv]), w_rkv)
    wl = (w0 + jnp.tanh(xw @ w1) @ w2).astype(jnp.float32)
    decay = jnp.exp(-jnp.exp(-jax.nn.softplus(-wl) - 0.5))
    a = jax.nn.sigmoid(a0 + (xa @ a1) @ a2)
    g = jax.nn.sigmoid(xg @ g1) @ g2
    kk = to_heads(k * k_k)
    kk = kk / jnp.maximum(jnp.sqrt(jnp.sum(kk * kk, -1, keepdims=True)), 1e-12)
    k = k * (1 + (a - 1) * k_a)
    rh, wh, kh, vh, ah = to_heads(r), to_heads(decay), to_heads(k), to_heads(v), to_heads(a)

    def step(s, inp):
        r_t, w_t, k_t, v_t, kk_t, a_t = inp
        s = (s * w_t[:, :, None, :]
             - jnp.einsum('bhvk,bhk->bhv', s, kk_t)[..., None] * (kk_t * a_t)[:, :, None, :]
             + v_t[..., None] * k_t[:, :, None, :])
        return s, jnp.einsum('bhvk,bhk->bhv', s, r_t)

    seq_inputs = tuple(jnp.swapaxes(z, 0, 1) for z in (rh, wh, kh, vh, kk, ah))
    s_final, o = lax.scan(step, s0.astype(jnp.float32), seq_inputs)
    o = jnp.swapaxes(o, 0, 1)
    mu_o = o.mean(-1, keepdims=True)
    var_o = jnp.square(o - mu_o).mean(-1, keepdims=True)
    on = ((o - mu_o) * lax.rsqrt(var_o + GN_EPS)).reshape(b, t, d) * gn_g + gn_b
    bonus = (jnp.sum(rh * kh * r_k, -1, keepdims=True) * vh).reshape(b, t, d)
    out = ((on + bonus).astype(x.dtype) * g) @ w_o
    return out, x[:, -1], s_final


def swiglu(x, w_gate_up, w_down):
    gate, up = jnp.split(x @ w_gate_up, 2, axis=-1)
    return (jax.nn.silu(gate) * up) @ w_down


def moe_swiglu(x, w_router, w_gate_up, w_down):
    logits = (x @ w_router).astype(jnp.float32)
    top_val, top_idx = lax.top_k(logits, TOP_K)
    gates = jax.nn.softmax(top_val, axis=-1)
    combine = jnp.sum(jax.nn.one_hot(top_idx, N_EXPERTS, dtype=jnp.float32) * gates[..., None], axis=-2)
    y = jnp.zeros_like(x)
    for e in range(N_EXPERTS):
        y = y + combine[..., e:e + 1].astype(x.dtype) * swiglu(x, w_gate_up[e], w_down[e])
    return y


def setup_inputs(seed: int = 0) -> dict:
    key = jax.random.key(seed)
    ks = jax.random.split(key, 40)
    nrm = lambda i, shape, s: jax.random.normal(ks[i], shape, jnp.float32) * s
    D, H, N, E = D_MODEL, N_HEADS, HEAD_DIM, N_EXPERTS
    NA, NR = N_ATT_LAYERS, N_RWKV_LAYERS
    return {
        'x_prompt': nrm(0, (BATCH, SEQ, D), 1.0),
        'x_sample': nrm(1, (DEC_BATCH, DEC_SEQ, D), 1.0),
        'cache_k': nrm(2, (NA, DEC_BATCH, H, PAST_LEN, N), 1.0),
        'cache_v': nrm(3, (NA, DEC_BATCH, H, PAST_LEN, N), 1.0),
        'state_wkv': nrm(4, (NR, DEC_BATCH, H, N, N), 0.5),
        'state_shift': nrm(5, (NR, DEC_BATCH, D), 1.0),
        'att_w_qkv': nrm(6, (NA, D, 3 * D), D ** -0.5),
        'att_w_o': nrm(7, (NA, D, D), D ** -0.5 * DEEPNORM_BETA),
        'ffn_w_gate_up': nrm(8, (NA, D, 2 * D_FF), D ** -0.5),
        'ffn_w_down': nrm(9, (NA, D_FF, D), D_FF ** -0.5 * DEEPNORM_BETA),
        'rwkv_mu': jax.random.uniform(ks[10], (NR, N_SHIFT_MIX, D), jnp.float32),
        'rwkv_w_rkv': nrm(11, (NR, 3, D, D), D ** -0.5),
        'rwkv_w0': jax.random.uniform(ks[12], (NR, D), jnp.float32, -6.0, -1.0),
        'rwkv_w1': nrm(13, (NR, D, DECAY_LORA), D ** -0.5),
        'rwkv_w2': nrm(14, (NR, DECAY_LORA, D), 0.1 * DECAY_LORA ** -0.5),
        'rwkv_a0': nrm(15, (NR, D), 0.1),
        'rwkv_a1': nrm(16, (NR, D, AAA_LORA), D ** -0.5),
        'rwkv_a2': nrm(17, (NR, AAA_LORA, D), 0.1 * AAA_LORA ** -0.5),
        'rwkv_g1': nrm(18, (NR, D, GATE_LORA), D ** -0.5),
        'rwkv_g2': nrm(19, (NR, GATE_LORA, D), GATE_LORA ** -0.5),
        'rwkv_k_k': 0.85 + nrm(20, (NR, D), 0.05),
        'rwkv_k_a': 1.0 + nrm(21, (NR, D), 0.05),
        'rwkv_r_k': nrm(22, (NR, H, N), 0.1),
        'rwkv_gn_g': 1.0 + nrm(23, (NR, D), 0.05),
        'rwkv_gn_b': nrm(24, (NR, D), 0.02),
        'rwkv_w_o': nrm(25, (NR, D, D), D ** -0.5 * DEEPNORM_BETA),
        'moe_w_router': nrm(26, (NR, D, E), D ** -0.5),
        'moe_w_gate_up': nrm(27, (NR, E, D, 2 * D_FF_EXPERT), D ** -0.5),
        'moe_w_down': nrm(28, (NR, E, D_FF_EXPERT, D), D_FF_EXPERT ** -0.5 * DEEPNORM_BETA),
        'ln_mix_g': 1.0 + nrm(29, (DEPTH, D), 0.05),
        'ln_mix_b': nrm(30, (DEPTH, D), 0.02),
        'ln_ffn_g': 1.0 + nrm(31, (DEPTH, D), 0.05),
        'ln_ffn_b': nrm(32, (DEPTH, D), 0.02),
    }


def reference(x_prompt, x_sample, cache_k, cache_v, state_wkv, state_shift,
              att_w_qkv, att_w_o, ffn_w_gate_up, ffn_w_down,
              rwkv_mu, rwkv_w_rkv, rwkv_w0, rwkv_w1, rwkv_w2, rwkv_a0, rwkv_a1, rwkv_a2,
              rwkv_g1, rwkv_g2, rwkv_k_k, rwkv_k_a, rwkv_r_k, rwkv_gn_g, rwkv_gn_b, rwkv_w_o,
              moe_w_router, moe_w_gate_up, moe_w_down,
              ln_mix_g, ln_mix_b, ln_ffn_g, ln_ffn_b):
    xp, xs = x_prompt, x_sample
    past_len = cache_k.shape[3]
    k_p, v_p, k_s, v_s = [], [], [], []
    wkv_p, sh_p, wkv_s, sh_s = [], [], [], []
    for i in range(DEPTH):
        j = i // N_MIXERS
        if i % N_MIXERS == 0:
            q, k, v = sb_project(xp, att_w_qkv[j])
            mp = merge_heads(sb_prompt(q, k, v)) @ att_w_o[j]
            k_p.append(k)
            v_p.append(v)
            q, k, v = sb_project(xs, att_w_qkv[j])
            keys = jnp.concatenate([cache_k[j].astype(k.dtype), k], axis=2)
            vals = jnp.concatenate([cache_v[j].astype(v.dtype), v], axis=2)
            ms = merge_heads(sb_block(q, keys, vals, past_len)) @ att_w_o[j]
            k_s.append(k)
            v_s.append(v)
        else:
            rp = (rwkv_mu[j], rwkv_w_rkv[j], rwkv_w0[j], rwkv_w1[j], rwkv_w2[j],
                  rwkv_a0[j], rwkv_a1[j], rwkv_a2[j], rwkv_g1[j], rwkv_g2[j],
                  rwkv_k_k[j], rwkv_k_a[j], rwkv_r_k[j], rwkv_gn_g[j], rwkv_gn_b[j], rwkv_w_o[j])
            zero_shift = jnp.zeros((xp.shape[0], D_MODEL), xp.dtype)
            zero_wkv = jnp.zeros((xp.shape[0], N_HEADS, HEAD_DIM, HEAD_DIM), jnp.float32)
            mp, shp, sp = rwkv7_time_mix(xp, zero_shift, zero_wkv, *rp)
            ms, shs, ss = rwkv7_time_mix(xs, state_shift[j], state_wkv[j], *rp)
            wkv_p.append(sp)
            sh_p.append(shp)
            wkv_s.append(ss)
            sh_s.append(shs)
        xp = layer_norm(DEEPNORM_ALPHA * xp + mp, ln_mix_g[i], ln_mix_b[i])
        xs = layer_norm(DEEPNORM_ALPHA * xs + ms, ln_mix_g[i], ln_mix_b[i])
        if i % 2 == 0:
            fp = swiglu(xp, ffn_w_gate_up[j], ffn_w_down[j])
            fs = swiglu(xs, ffn_w_gate_up[j], ffn_w_down[j])
        else:
            fp = moe_swiglu(xp, moe_w_router[j], moe_w_gate_up[j], moe_w_down[j])
            fs = moe_swiglu(xs, moe_w_router[j], moe_w_gate_up[j], moe_w_down[j])
        xp = layer_norm(DEEPNORM_ALPHA * xp + fp, ln_ffn_g[i], ln_ffn_b[i])
        xs = layer_norm(DEEPNORM_ALPHA * xs + fs, ln_ffn_g[i], ln_ffn_b[i])
    return (xp, xs,
            jnp.stack(k_p), jnp.stack(v_p), jnp.stack(wkv_p), jnp.stack(sh_p),
            jnp.stack(k_s), jnp.stack(v_s), jnp.stack(wkv_s), jnp.stack(sh_s))
```

```python
import functools

import jax
import jax.numpy as jnp
from jax import lax
from jax.experimental import pallas as pl
from jax.experimental.pallas import tpu as pltpu

F32 = jnp.float32
BF16 = jnp.bfloat16

HEAD_DIM = 64
LANES = 128
PAIR = 2 * HEAD_DIM
LN_EPS = 1e-5
GN_EPS = 64e-5
DEEPNORM_ALPHA = 4.0 ** 0.25
SCAN_CHUNK = 64
V7X_VMEM_LIMIT = 56 * 1024 * 1024


def _params(*sem, vmem=V7X_VMEM_LIMIT):
    return pltpu.CompilerParams(dimension_semantics=sem, vmem_limit_bytes=vmem)


def _layer_norm(y, g, b):
    mu = jnp.mean(y, axis=-1, keepdims=True)
    c = y - mu
    var = jnp.mean(c * c, axis=-1, keepdims=True)
    return c * lax.rsqrt(var + LN_EPS) * g + b


def _split2(x):
    hi = x.astype(BF16)
    lo = (x - hi.astype(F32)).astype(BF16)
    return hi, lo


def _dot(a, b):
    return jnp.dot(a, b, preferred_element_type=F32)


def _dot_nt(a, b):
    return lax.dot_general(a, b, (((1,), (1,)), ((), ())), preferred_element_type=F32)


def _dot_tn(a, b):
    return lax.dot_general(a, b, (((0,), (0,)), ((), ())), preferred_element_type=F32)


def _dot3(a, b):
    ah, al = _split2(a)
    bh, bl = _split2(b)
    return _dot(jnp.concatenate([ah, ah, al], axis=1), jnp.concatenate([bh, bl, bh], axis=0))


def _dot3_nt(a, b):
    ah, al = _split2(a)
    bh, bl = _split2(b)
    return _dot_nt(jnp.concatenate([ah, ah, al], axis=1), jnp.concatenate([bh, bl, bh], axis=1))


def _dot3_tn(a, b):
    ah, al = _split2(a)
    bh, bl = _split2(b)
    return _dot_tn(jnp.concatenate([ah, ah, al], axis=0), jnp.concatenate([bh, bl, bh], axis=0))


def _dot2_exact_rhs(a, b_bf16):
    ah, al = _split2(a)
    return _dot(jnp.concatenate([ah, al], axis=1), jnp.concatenate([b_bf16, b_bf16], axis=0))


def _head_ones(n):
    r = lax.broadcasted_iota(jnp.int32, (n, n), 0) >> 6
    c = lax.broadcasted_iota(jnp.int32, (n, n), 1) >> 6
    return (r == c).astype(BF16)


def _head_sum(x, ones_pair):
    d = x.shape[1]
    cols = [_dot2_exact_rhs(x[:, c:c + PAIR], ones_pair) for c in range(0, d, PAIR)]
    return jnp.concatenate(cols, axis=1)


def _qkv_kernel(x_ref, wq_ref, wk_ref, wv_ref, qb_ref, kb_ref, vb_ref, kf_ref, vf_ref, xb_scr, *, hpb):
    @pl.when(pl.program_id(2) == 0)
    def _():
        xb_scr[...] = x_ref[0].astype(BF16)

    xb = xb_scr[...]
    q = _dot(xb, wq_ref[...]) * (HEAD_DIM ** -0.5)
    k = _dot(xb, wk_ref[...])
    v = _dot(xb, wv_ref[...])
    for h in range(hpb):
        sl = slice(h * HEAD_DIM, (h + 1) * HEAD_DIM)
        qb_ref[0, h] = q[:, sl].astype(BF16)
        kb_ref[0, h] = k[:, sl].astype(BF16)
        vb_ref[0, h] = v[:, sl].astype(BF16)
        kf_ref[0, h] = k[:, sl]
        vf_ref[0, h] = v[:, sl]


def _qkv_proj(x, w_bf16, tm):
    b, t, d = x.shape
    h = d // HEAD_DIM
    hpb = 4
    tn = hpb * HEAD_DIM
    nb = d // tn
    grid = (b, t // tm, nb)
    hspec = pl.BlockSpec((1, hpb, tm, HEAD_DIM), lambda bi, ti, ni: (bi, ni, ti, 0))
    shp = (b, h, t, HEAD_DIM)
    return pl.pallas_call(
        functools.partial(_qkv_kernel, hpb=hpb),
        grid=grid,
        in_specs=[
            pl.BlockSpec((1, tm, d), lambda bi, ti, ni: (bi, ti, 0)),
            pl.BlockSpec((d, tn), lambda bi, ti, ni: (0, ni)),
            pl.BlockSpec((d, tn), lambda bi, ti, ni: (0, nb + ni)),
            pl.BlockSpec((d, tn), lambda bi, ti, ni: (0, 2 * nb + ni)),
        ],
        out_specs=[hspec] * 5,
        out_shape=[jax.ShapeDtypeStruct(shp, BF16)] * 3 + [jax.ShapeDtypeStruct(shp, F32)] * 2,
        scratch_shapes=[pltpu.VMEM((tm, d), BF16)],
        compiler_params=_params("parallel", "parallel", "arbitrary"),
        name="qkv_proj",
    )(x, w_bf16, w_bf16, w_bf16)


def _sb_attn_kernel(q_ref, k_ref, v_ref, o_ref, *, blk, nq, diag0):
    r = lax.broadcasted_iota(jnp.int32, (2 * blk, 2 * blk), 0)
    c = lax.broadcasted_iota(jnp.int32, (2 * blk, 2 * blk), 1)
    r = jnp.where(r >= blk, r - blk, r)
    suffix = ((c >= blk) | (r > c)).astype(BF16)
    qi = lax.broadcasted_iota(jnp.int32, (blk, blk), 0)
    kj = lax.broadcasted_iota(jnp.int32, (blk, blk), 1)
    causal = kj < qi

    def q_body(qb, carry_unused):
        q0 = pl.multiple_of(qb * blk, blk)
        qs = [q_ref[0, h, pl.ds(q0, blk), :] for h in range(2)]

        def block(kb, st, masked):
            k0 = pl.multiple_of(kb * blk, blk)
            new = []
            for h in range(2):
                later, acc = st[h]
                z = _dot_nt(qs[h], k_ref[0, h, pl.ds(k0, blk), :])
                soft = jnp.log(1.0 + jnp.exp(-jnp.abs(z)))
                log_beta = jnp.minimum(z, 0.0) - soft
                log_stay = log_beta - z
                if masked:
                    log_stay = jnp.where(causal, log_stay, 0.0)
                hi, lo = _split2(log_stay)
                sums = _dot(jnp.concatenate([hi, lo], axis=1), suffix)
                between = sums[:, :blk] + later
                later = later + sums[:, blk:]
                a = jnp.exp(log_beta + between)
                if masked:
                    a = jnp.where(causal, a, 0.0)
                acc = acc + _dot(a.astype(BF16), v_ref[0, h, pl.ds(k0, blk), :])
                new.append((later, acc))
            return tuple(new)

        zero = (jnp.zeros((blk, blk), F32), jnp.zeros((blk, HEAD_DIM), F32))
        st = block(diag0 + qb, (zero, zero), True)
        st = lax.fori_loop(0, diag0 + qb, lambda i, s: block(diag0 + qb - 1 - i, s, False), st)
        o_ref[0, pl.ds(q0, blk), :] = jnp.concatenate([st[0][1], st[1][1]], axis=1).astype(BF16)
        return carry_unused

    lax.fori_loop(0, nq, q_body, 0)


def _sb_attention(q, k, v, blk, q_start):
    b, h, tq, _ = q.shape
    tk = k.shape[2]
    assert tq % blk == 0 and q_start % blk == 0 and tk == q_start + tq
    qspec = pl.BlockSpec((1, 2, tq, HEAD_DIM), lambda bi, pi: (bi, pi, 0, 0))
    kspec = pl.BlockSpec((1, 2, tk, HEAD_DIM), lambda bi, pi: (bi, pi, 0, 0))
    return pl.pallas_call(
        functools.partial(_sb_attn_kernel, blk=blk, nq=tq // blk, diag0=q_start // blk),
        grid=(b, h // 2),
        in_specs=[qspec, kspec, kspec],
        out_specs=pl.BlockSpec((1, tq, PAIR), lambda bi, pi: (bi, 0, pi)),
        out_shape=jax.ShapeDtypeStruct((b, tq, h * HEAD_DIM), BF16),
        compiler_params=_params("parallel", "parallel"),
        name="sb_attention",
    )(q, k, v)


def _mm_res_ln_kernel(a_ref, w_ref, x_ref, g_ref, b_ref, o_ref):
    y = _dot(a_ref[...], w_ref[...])
    o_ref[...] = _layer_norm(DEEPNORM_ALPHA * x_ref[...] + y, g_ref[...], b_ref[...])


def _mm_res_ln(a, w_bf16, x, g, b, tm):
    t, d = x.shape
    kdim = a.shape[1]
    vec = pl.BlockSpec((1, d), lambda i: (0, 0))
    return pl.pallas_call(
        _mm_res_ln_kernel,
        grid=(t // tm,),
        in_specs=[pl.BlockSpec((tm, kdim), lambda i: (i, 0)),
                  pl.BlockSpec((kdim, d), lambda i: (0, 0)),
                  pl.BlockSpec((tm, d), lambda i: (i, 0)), vec, vec],
        out_specs=pl.BlockSpec((tm, d), lambda i: (i, 0)),
        out_shape=jax.ShapeDtypeStruct((t, d), F32),
        compiler_params=_params("parallel"),
        name="attn_out_ln",
    )(a, w_bf16, x, g.reshape(1, d), b.reshape(1, d))


def _swiglu_step(xb, wg_ref, wu_ref, wd_ref):
    gate = _dot(xb, wg_ref[...])
    up = _dot(xb, wu_ref[...])
    hidden = gate * jax.nn.sigmoid(gate) * up
    return _dot(hidden.astype(BF16), wd_ref[...])


def _ffn_ln_kernel(x_ref, wg_ref, wu_ref, wd_ref, g_ref, b_ref, o_ref, xb_scr, acc_scr):
    j = pl.program_id(1)

    @pl.when(j == 0)
    def _():
        xb_scr[...] = x_ref[...].astype(BF16)
        acc_scr[...] = jnp.zeros_like(acc_scr)

    acc_scr[...] += _swiglu_step(xb_scr[...], wg_ref, wu_ref, wd_ref)

    @pl.when(j == pl.num_programs(1) - 1)
    def _():
        o_ref[...] = _layer_norm(DEEPNORM_ALPHA * x_ref[...] + acc_scr[...], g_ref[...], b_ref[...])


def _ffn_ln(x, w_gate_up_bf16, w_down_bf16, g, b, tm, tn):
    t, d = x.shape
    f = w_down_bf16.shape[0]
    nj = f // tn
    vec = pl.BlockSpec((1, d), lambda i, j: (0, 0))
    return pl.pallas_call(
        _ffn_ln_kernel,
        grid=(t // tm, nj),
        in_specs=[pl.BlockSpec((tm, d), lambda i, j: (i, 0)),
                  pl.BlockSpec((d, tn), lambda i, j: (0, j)),
                  pl.BlockSpec((d, tn), lambda i, j: (0, nj + j)),
                  pl.BlockSpec((tn, d), lambda i, j: (j, 0)), vec, vec],
        out_specs=pl.BlockSpec((tm, d), lambda i, j: (i, 0)),
        out_shape=jax.ShapeDtypeStruct((t, d), F32),
        scratch_shapes=[pltpu.VMEM((tm, d), BF16), pltpu.VMEM((tm, d), F32)],
        compiler_params=_params("parallel", "arbitrary"),
        name="dense_swiglu_ln",
    )(x, w_gate_up_bf16, w_gate_up_bf16, w_down_bf16, g.reshape(1, d), b.reshape(1, d))


def _moe_ffn_kernel(te_ref, nt_ref, x_ref, gate_ref, wg_ref, wu_ref, wd_ref, o_ref, xb_scr, acc_scr):
    i = pl.program_id(0)
    j = pl.program_id(1)
    used = i < nt_ref[0]

    @pl.when(used & (j == 0))
    def _():
        xb_scr[...] = x_ref[...].astype(BF16)
        acc_scr[...] = jnp.zeros_like(acc_scr)

    @pl.when(used)
    def _():
        acc_scr[...] += _swiglu_step(xb_scr[...], wg_ref.at[0], wu_ref.at[0], wd_ref.at[0])

    @pl.when(j == pl.num_programs(1) - 1)
    def _():
        o_ref[...] = jnp.where(used, gate_ref[...] * acc_scr[...], 0.0)


def _moe_ffn(tile_expert, n_tiles_used, xg, gates, w_gate_up_bf16, w_down_bf16, tm, tn):
    n, d = xg.shape
    f = w_down_bf16.shape[1]
    nj = f // tn
    grid_spec = pltpu.PrefetchScalarGridSpec(
        num_scalar_prefetch=2,
        grid=(n // tm, nj),
        in_specs=[pl.BlockSpec((tm, d), lambda i, j, te, nt: (i, 0)),
                  pl.BlockSpec((tm, 1), lambda i, j, te, nt: (i, 0)),
                  pl.BlockSpec((1, d, tn), lambda i, j, te, nt: (te[i], 0, j)),
                  pl.BlockSpec((1, d, tn), lambda i, j, te, nt: (te[i], 0, nj + j)),
                  pl.BlockSpec((1, tn, d), lambda i, j, te, nt: (te[i], j, 0))],
        out_specs=pl.BlockSpec((tm, d), lambda i, j, te, nt: (i, 0)),
        scratch_shapes=[pltpu.VMEM((tm, d), BF16), pltpu.VMEM((tm, d), F32)],
    )
    return pl.pallas_call(
        _moe_ffn_kernel,
        grid_spec=grid_spec,
        out_shape=jax.ShapeDtypeStruct((n, d), F32),
        compiler_params=_params("parallel", "arbitrary"),
        name="expert_swiglu",
    )(tile_expert, n_tiles_used, xg, gates, w_gate_up_bf16, w_gate_up_bf16, w_down_bf16)


def _row_copy(src_hbm, dst_ref, sem, src_row, dst_row):
    return pltpu.make_async_copy(src_hbm.at[pl.ds(src_row, 1)], dst_ref.at[pl.ds(dst_row, 1)], sem)


def _gather_rows_into(idx_ref, src_hbm, dst_ref, sem, n):
    def start(r, c):
        _row_copy(src_hbm, dst_ref, sem, idx_ref[0, 0, r], r).start()
        return c

    def wait(r, c):
        _row_copy(src_hbm, dst_ref, sem, 0, r).wait()
        return c

    lax.fori_loop(0, n, start, 0)
    lax.fori_loop(0, n, wait, 0)


def _gather_kernel(idx_ref, src_hbm, o_ref, sem, *, tg):
    _gather_rows_into(idx_ref, src_hbm, o_ref, sem, tg)


def _gather_rows(src, idx, tg):
    n = idx.shape[0]
    d = src.shape[1]
    return pl.pallas_call(
        functools.partial(_gather_kernel, tg=tg),
        grid=(n // tg,),
        in_specs=[pl.BlockSpec((1, 1, tg), lambda i: (i, 0, 0), memory_space=pltpu.SMEM),
                  pl.BlockSpec(memory_space=pl.ANY)],
        out_specs=pl.BlockSpec((tg, d), lambda i: (i, 0)),
        out_shape=jax.ShapeDtypeStruct((n, d), src.dtype),
        scratch_shapes=[pltpu.SemaphoreType.DMA(())],
        compiler_params=_params("arbitrary"),
        name="gather_rows",
    )(idx.reshape(n // tg, 1, tg), src)


def _combine_ln_kernel(p0_ref, p1_ref, x_ref, y_hbm, g_ref, b_ref, op_ref, os_ref, a_scr, b_scr, sem0, sem1,
                       *, tm, n_prompt_tiles):
    _gather_rows_into(p0_ref, y_hbm, a_scr, sem0, tm)
    _gather_rows_into(p1_ref, y_hbm, b_scr, sem1, tm)
    out = _layer_norm(DEEPNORM_ALPHA * x_ref[...] + (a_scr[...] + b_scr[...]), g_ref[...], b_ref[...])
    i = pl.program_id(0)

    @pl.when(i < n_prompt_tiles)
    def _():
        op_ref[...] = out

    @pl.when(i >= n_prompt_tiles)
    def _():
        os_ref[...] = out


def _combine_ln(x, y_sorted, pos0, pos1, g, b, tm, n_prompt):
    t, d = x.shape
    npt = n_prompt // tm
    nst = (t - n_prompt) // tm
    idx = pl.BlockSpec((1, 1, tm), lambda i: (i, 0, 0), memory_space=pltpu.SMEM)
    vec = pl.BlockSpec((1, d), lambda i: (0, 0))
    return pl.pallas_call(
        functools.partial(_combine_ln_kernel, tm=tm, n_prompt_tiles=npt),
        grid=(npt + nst,),
        in_specs=[idx, idx, pl.BlockSpec((tm, d), lambda i: (i, 0)),
                  pl.BlockSpec(memory_space=pl.ANY), vec, vec],
        out_specs=[pl.BlockSpec((tm, d), lambda i: (jnp.minimum(i, npt - 1), 0)),
                   pl.BlockSpec((tm, d), lambda i: (jnp.maximum(i - npt, 0), 0))],
        out_shape=[jax.ShapeDtypeStruct((n_prompt, d), F32), jax.ShapeDtypeStruct((t - n_prompt, d), F32)],
        scratch_shapes=[pltpu.VMEM((tm, d), F32), pltpu.VMEM((tm, d), F32),
                        pltpu.SemaphoreType.DMA(()), pltpu.SemaphoreType.DMA(())],
        compiler_params=_params("arbitrary"),
        name="moe_combine_ln",
    )(pos0.reshape(-1, 1, tm), pos1.reshape(-1, 1, tm), x, y_sorted, g.reshape(1, d), b.reshape(1, d))


def _rwkv_proj_kernel(x_ref, prev_ref, mu_ref, wr_ref, wk_ref, wv_ref, w1_ref, w2_ref, a1_ref, a2_ref,
                      g1_ref, g2_ref, vec_ref,
                      r_ref, lw_ref, k_ref, v_ref, kk_ref, b_ref, g_ref, bonus_ref, *, seg):
    x = x_ref[...]
    tm, d = x.shape
    rolled = pltpu.roll(x, 1, 0).reshape(tm // seg, seg, d)
    first = lax.broadcasted_iota(jnp.int32, (tm // seg, seg, d), 1) == 0
    x_prev = jnp.where(first, prev_ref[...], rolled).reshape(tm, d)
    xx = x_prev - x

    def mixed(i):
        return (x + xx * mu_ref[i:i + 1, :]).astype(BF16)

    r = _dot(mixed(0), wr_ref[...])
    k = _dot(mixed(2), wk_ref[...])
    v = _dot(mixed(3), wv_ref[...])
    w0, a0, k_k, k_a, r_k = (vec_ref[i:i + 1, :] for i in range(5))
    wl = w0 + _dot(jnp.tanh(_dot(mixed(1), w1_ref[...])).astype(BF16), w2_ref[...])
    softplus_neg = jnp.maximum(-wl, 0.0) + jnp.log(1.0 + jnp.exp(-jnp.abs(wl)))
    lw_ref[...] = -jnp.exp(-softplus_neg - 0.5)
    a = jax.nn.sigmoid(a0 + _dot(_dot(mixed(4), a1_ref[...]).astype(BF16), a2_ref[...]))
    g_ref[...] = _dot(jax.nn.sigmoid(_dot(mixed(5), g1_ref[...])).astype(BF16), g2_ref[...])
    ones_pair = _head_ones(PAIR)
    kk = k * k_k
    kk = kk / jnp.maximum(jnp.sqrt(_head_sum(kk * kk, ones_pair)), 1e-12)
    k = k * (1.0 + (a - 1.0) * k_a)
    r_ref[...] = r
    k_ref[...] = k
    v_ref[...] = v
    kk_ref[...] = kk
    b_ref[...] = kk * a
    bonus_ref[...] = _head_sum(r * k * r_k, ones_pair) * v


def _rwkv_proj(x, prev_rows, mu, w_r, w_k, w_v, w1, w2, a1, a2, g1, g2, vecs, tm, seg):
    t, d = x.shape
    row = pl.BlockSpec((tm, d), lambda i: (i, 0))

    def full(a):
        return pl.BlockSpec(a.shape, lambda i: (0,) * a.ndim)

    consts = (mu, w_r, w_k, w_v, w1, w2, a1, a2, g1, g2, vecs)
    return pl.pallas_call(
        functools.partial(_rwkv_proj_kernel, seg=seg),
        grid=(t // tm,),
        in_specs=[row, pl.BlockSpec((tm // seg, 1, d), lambda i: (i, 0, 0))] + [full(c) for c in consts],
        out_specs=[row] * 8,
        out_shape=[jax.ShapeDtypeStruct((t, d), F32)] * 8,
        compiler_params=_params("parallel"),
        name="rwkv_proj",
    )(x, prev_rows, *consts)


def _unit_lower_inverse(low, limit):
    n = low.shape[0]
    r = lax.broadcasted_iota(jnp.int32, (n, n), 0)
    c = lax.broadcasted_iota(jnp.int32, (n, n), 1)
    eye = (r == c).astype(F32)
    neg = jnp.where((r >> 3) == (c >> 3), -low, 0.0)
    inv = eye + neg
    p = neg
    for _ in range(2):
        p = _dot3(p, p)
        inv = inv + _dot3(inv, p)
    shift = 3
    while (1 << shift) < limit:
        off = jnp.where(((r >> (shift + 1)) == (c >> (shift + 1))) & ((r >> shift) != (c >> shift)), low, 0.0)
        inv = inv - _dot3(_dot3(inv, off), inv)
        shift += 1
    return inv


def _rwkv_scan_kernel(r_ref, lw_ref, k_ref, v_ref, kk_ref, b_ref, h0_ref, o_ref, hT_ref, h_scr, *, n_chunks):
    cs = SCAN_CHUNK
    tb = pl.program_id(2)

    @pl.when(tb == 0)
    def _():
        h_scr[...] = h0_ref[0, 0]

    lane = lax.broadcasted_iota(jnp.int32, (cs, PAIR), 1)
    row = lax.broadcasted_iota(jnp.int32, (cs, PAIR), 0)
    head_a = lane < HEAD_DIM
    strict = (lane & (HEAD_DIM - 1)) < row
    incl = (lane & (HEAD_DIM - 1)) <= row
    tri = (lax.broadcasted_iota(jnp.int32, (cs, cs), 1) <= lax.broadcasted_iota(jnp.int32, (cs, cs), 0)).astype(BF16)
    pr = lax.broadcasted_iota(jnp.int32, (PAIR, PAIR), 0) >> 6
    pc = lax.broadcasted_iota(jnp.int32, (PAIR, PAIR), 1) >> 6
    same_head = pr == pc

    def stack(x):
        return jnp.concatenate([jnp.where(head_a, x, 0.0), jnp.where(head_a, 0.0, x)], axis=0)

    def chunk(ci, carry):
        t0 = pl.multiple_of(ci * cs, cs)
        sl = pl.ds(t0, cs)
        r, lw, k, v, kk, b = (ref[0, sl, :] for ref in (r_ref, lw_ref, k_ref, v_ref, kk_ref, b_ref))
        h = h_scr[...]
        l1 = lw.astype(BF16)
        rem = lw - l1.astype(F32)
        l2 = rem.astype(BF16)
        l3 = (rem - l2.astype(F32)).astype(BF16)
        parts = _dot(tri, jnp.concatenate([l1, l2, l3], axis=1))
        cum = parts[:, :PAIR] + parts[:, PAIR:2 * PAIR] + parts[:, 2 * PAIR:]
        total = cum[cs - 1:cs, :]
        grow = jnp.exp(-cum)
        r_hat = r * jnp.exp(cum)
        kk_hat = kk * jnp.exp(cum - lw)
        k_hat = k * grow
        b_hat = b * grow
        to_end = jnp.exp(total - cum)
        lhs = jnp.concatenate([kk_hat, r_hat], axis=0)
        rhs = jnp.concatenate([stack(b_hat), stack(k_hat)], axis=0)
        gram = _dot3_nt(lhs, rhs)
        l_b = jnp.where(strict, gram[:cs, :PAIR], 0.0)
        l_k = jnp.where(strict, gram[:cs, PAIR:], 0.0)
        a_rb = jnp.where(incl, gram[cs:, :PAIR], 0.0)
        a_rk = jnp.where(incl, gram[cs:, PAIR:], 0.0)
        inv = _unit_lower_inverse(stack(l_b), cs)
        v_st = stack(v)
        w = _dot3(jnp.concatenate([kk_hat, l_k], axis=1), jnp.concatenate([h, v_st], axis=0))
        u_st = _dot3(inv, stack(w))
        o = _dot3(jnp.concatenate([r_hat, a_rk, -a_rb], axis=1), jnp.concatenate([h, v_st, u_st], axis=0))
        o_ref[0, sl, :] = o
        u = u_st[:cs] + u_st[cs:]
        upd = _dot3_tn(jnp.concatenate([k * to_end, -(b * to_end)], axis=0), jnp.concatenate([v, u], axis=0))
        decay_col = jnp.transpose(jnp.broadcast_to(jnp.exp(total), (PAIR, PAIR)))
        h_scr[...] = decay_col * h + jnp.where(same_head, upd, 0.0)
        return carry

    lax.fori_loop(0, n_chunks, chunk, 0)

    @pl.when(tb == pl.num_programs(2) - 1)
    def _():
        hT_ref[0, 0] = h_scr[...]


def _rwkv_scan(r, lw, k, v, kk, b, h0, tb):
    bsz, t, d = r.shape
    npair = d // PAIR
    seq = pl.BlockSpec((1, tb, PAIR), lambda bi, pi, ti: (bi, ti, pi))
    st = pl.BlockSpec((1, 1, PAIR, PAIR), lambda bi, pi, ti: (bi, pi, 0, 0))
    return pl.pallas_call(
        functools.partial(_rwkv_scan_kernel, n_chunks=tb // SCAN_CHUNK),
        grid=(bsz, npair, t // tb),
        in_specs=[seq] * 6 + [st],
        out_specs=[seq, st],
        out_shape=[jax.ShapeDtypeStruct((bsz, t, d), F32), jax.ShapeDtypeStruct((bsz, npair, PAIR, PAIR), F32)],
        scratch_shapes=[pltpu.VMEM((PAIR, PAIR), F32)],
        compiler_params=_params("parallel", "parallel", "arbitrary"),
        name="rwkv_scan",
    )(r, lw, k, v, kk, b, h0)


def _rwkv_out_kernel(o_ref, g_ref, bonus_ref, x_ref, wo_ref, vec_ref, dst_hbm_unused, out_ref):
    o = o_ref[...]
    ones_pair = _head_ones(PAIR)
    mean = _head_sum(o, ones_pair) * (1.0 / HEAD_DIM)
    c = o - mean
    var = _head_sum(c * c, ones_pair) * (1.0 / HEAD_DIM)
    on = c * lax.rsqrt(var + GN_EPS) * vec_ref[0:1, :] + vec_ref[1:2, :]
    y = _dot(((on + bonus_ref[...]) * g_ref[...]).astype(BF16), wo_ref[...])
    out_ref[...] = _layer_norm(DEEPNORM_ALPHA * x_ref[...] + y, vec_ref[2:3, :], vec_ref[3:4, :])


def _rwkv_out(o, g, bonus, x, w_o_bf16, vecs, dst, row_offset, tm):
    t, d = x.shape
    row = pl.BlockSpec((tm, d), lambda i: (i, 0))
    off = row_offset // tm
    return pl.pallas_call(
        _rwkv_out_kernel,
        grid=(t // tm,),
        in_specs=[row, row, row, row, pl.BlockSpec((d, d), lambda i: (0, 0)),
                  pl.BlockSpec(vecs.shape, lambda i: (0, 0)), pl.BlockSpec(memory_space=pl.ANY)],
        out_specs=pl.BlockSpec((tm, d), lambda i: (off + i, 0)),
        out_shape=jax.ShapeDtypeStruct(dst.shape, F32),
        input_output_aliases={6: 0},
        compiler_params=_params("parallel"),
        name="rwkv_out_ln",
    )(o, g, bonus, x, w_o_bf16, vecs, dst)


def _router_kernel(x_ref, w_ref, o_ref, *, n_experts):
    logits = _dot3(x_ref[...], w_ref[...])
    lane = lax.broadcasted_iota(jnp.int32, logits.shape, 1)
    lane_f = lane.astype(F32)
    lowest = jnp.float32(-3.0e38)
    logits = jnp.where(lane < n_experts, logits, lowest)
    m1 = jnp.max(logits, axis=1, keepdims=True)
    i1 = jnp.min(jnp.where(logits == m1, lane_f, float(LANES)), axis=1, keepdims=True)
    rest = jnp.where(lane_f == i1, lowest, logits)
    m2 = jnp.max(rest, axis=1, keepdims=True)
    i2 = jnp.min(jnp.where(rest == m2, lane_f, float(LANES)), axis=1, keepdims=True)
    e = jnp.exp(m2 - m1)
    g1 = 1.0 / (1.0 + e)
    g2 = e * g1
    out = jnp.where(lane == 0, i1, 0.0)
    out = jnp.where(lane == 1, i2, out)
    out = jnp.where(lane == 2, g1, out)
    o_ref[...] = jnp.where(lane == 3, g2, out)


def _router(x, w_router, tm):
    t, d = x.shape
    e = w_router.shape[1]
    w_pad = jnp.zeros((d, LANES), F32).at[:, :e].set(w_router)
    return pl.pallas_call(
        functools.partial(_router_kernel, n_experts=e),
        grid=(t // tm,),
        in_specs=[pl.BlockSpec((tm, d), lambda i: (i, 0)), pl.BlockSpec((d, LANES), lambda i: (0, 0))],
        out_specs=pl.BlockSpec((tm, LANES), lambda i: (i, 0)),
        out_shape=jax.ShapeDtypeStruct((t, LANES), F32),
        compiler_params=_params("parallel"),
        name="router_top2",
    )(x, w_pad)


def _row_tile(t, cap):
    tm = cap
    while t % tm:
        tm //= 2
    return tm


def _pair_states(s):
    b, h, n, _ = s.shape
    st = jnp.swapaxes(s, 2, 3).reshape(b, h // 2, 2, n, n)
    z = jnp.zeros_like(st[:, :, 0])
    top = jnp.concatenate([st[:, :, 0], z], axis=-1)
    bot = jnp.concatenate([z, st[:, :, 1]], axis=-1)
    return jnp.concatenate([top, bot], axis=-2)


def _unpair_states(hp):
    b, p, _, _ = hp.shape
    n = HEAD_DIM
    st = jnp.stack([hp[:, :, :n, :n], hp[:, :, n:, n:]], axis=2).reshape(b, 2 * p, n, n)
    return jnp.swapaxes(st, 2, 3)


def _attention_layer(x, w_qkv, w_o, ln_g, ln_b, cache_k=None, cache_v=None):
    b, t, d = x.shape
    q, kb, vb, kf, vf = _qkv_proj(x, w_qkv, _row_tile(t, 512))
    if cache_k is None:
        ao = _sb_attention(q, kb, vb, LANES, 0)
    else:
        keys = jnp.concatenate([cache_k.astype(BF16), kb], axis=2)
        vals = jnp.concatenate([cache_v.astype(BF16), vb], axis=2)
        ao = _sb_attention(q, keys, vals, t, cache_k.shape[2])
    x1 = _mm_res_ln(ao.reshape(b * t, d), w_o, x.reshape(b * t, d), ln_g, ln_b, _row_tile(b * t, 512))
    return x1, kf, vf


def _rwkv_layer(x, x_last, s0, p, dst, row_offset):
    b, t, d = x.shape
    seg = SCAN_CHUNK
    xs = x.reshape(b, t // seg, seg, d)
    prev = jnp.concatenate([x_last[:, None, :], xs[:, :-1, seg - 1, :]], axis=1).reshape(b * t // seg, 1, d)
    xf = x.reshape(b * t, d)
    tm = _row_tile(b * t, 256)
    r, lw, k, v, kk, bb, g, bonus = _rwkv_proj(xf, prev, p["mu"], p["w_r"], p["w_k"], p["w_v"], p["w1"], p["w2"],
                                               p["a1"], p["a2"], p["g1"], p["g2"], p["proj_vecs"], tm, seg)
    shp = (b, t, d)
    o, h_t = _rwkv_scan(*(z.reshape(shp) for z in (r, lw, k, v, kk, bb)), _pair_states(s0), _row_tile(t, 512))
    dst = _rwkv_out(o.reshape(b * t, d), g, bonus, xf, p["w_o"], p["out_vecs"], dst, row_offset, tm)
    return dst, _unpair_states(h_t)


def _moe_layer(x, n_prompt, w_router, w_gate_up, w_down, ln_g, ln_b, tm):
    t, d = x.shape
    e = w_router.shape[1]
    route = _router(x, w_router, tm)
    experts = jnp.concatenate([route[:, 0], route[:, 1]]).astype(jnp.int32)
    gates = jnp.concatenate([route[:, 2], route[:, 3]])
    onehot = (experts[:, None] == jnp.arange(e, dtype=jnp.int32)[None, :]).astype(jnp.int32)
    counts = jnp.sum(onehot, axis=0)
    padded = ((counts + tm - 1) // tm) * tm
    ends = jnp.cumsum(padded)
    starts = ends - padded
    rank = jnp.take_along_axis(jnp.cumsum(onehot, axis=0), experts[:, None], axis=1)[:, 0] - 1
    pos = starts[experts] + rank
    n_rows = ((2 * t + e * (tm - 1)) // tm) * tm
    order = jnp.argsort(experts, stable=True).astype(jnp.int32)
    row_id = jnp.arange(n_rows, dtype=jnp.int32)
    row_expert = jnp.minimum(jnp.searchsorted(ends, row_id, side="right"), e - 1).astype(jnp.int32)
    row_rank = row_id - starts[row_expert]
    real = row_rank < counts[row_expert]
    first = jnp.cumsum(counts) - counts
    slot = order[jnp.clip(first[row_expert] + row_rank, 0, 2 * t - 1)]
    src_token = jnp.where(real, slot % t, 0).astype(jnp.int32)
    row_gate = jnp.where(real, gates[slot], 0.0).reshape(n_rows, 1)
    tile_expert = row_expert[::tm]
    n_tiles_used = (ends[-1] // tm).astype(jnp.int32).reshape(1)
    xg = _gather_rows(x, src_token, tm)
    y = _moe_ffn(tile_expert, n_tiles_used, xg, row_gate, w_gate_up, w_down, tm, 512)
    return _combine_ln(x, y, pos[:t].astype(jnp.int32), pos[t:].astype(jnp.int32), ln_g, ln_b, tm, n_prompt)


def kernel(x_prompt, x_sample, cache_k, cache_v, state_wkv, state_shift, att_w_qkv, att_w_o, ffn_w_gate_up, ffn_w_down, rwkv_mu, rwkv_w_rkv, rwkv_w0, rwkv_w1, rwkv_w2, rwkv_a0, rwkv_a1, rwkv_a2, rwkv_g1, rwkv_g2, rwkv_k_k, rwkv_k_a, rwkv_r_k, rwkv_gn_g, rwkv_gn_b, rwkv_w_o, moe_w_router, moe_w_gate_up, moe_w_down, ln_mix_g, ln_mix_b, ln_ffn_g, ln_ffn_b):
    bp, tp, d = x_prompt.shape
    bs, ts, _ = x_sample.shape
    n_prompt, n_sample = bp * tp, bs * ts
    bf = lambda a: a.astype(BF16)

    w_qkv, w_o = bf(att_w_qkv[0]), bf(att_w_o[0])
    xp1, k_p, v_p = _attention_layer(x_prompt, w_qkv, w_o, ln_mix_g[0], ln_mix_b[0])
    xs1, k_s, v_s = _attention_layer(x_sample, w_qkv, w_o, ln_mix_g[0], ln_mix_b[0], cache_k[0], cache_v[0])
    w_gu, w_dn = bf(ffn_w_gate_up[0]), bf(ffn_w_down[0])
    xp2 = _ffn_ln(xp1, w_gu, w_dn, ln_ffn_g[0], ln_ffn_b[0], _row_tile(n_prompt, 512), 256).reshape(bp, tp, d)
    xs2 = _ffn_ln(xs1, w_gu, w_dn, ln_ffn_g[0], ln_ffn_b[0], _row_tile(n_sample, 512), 256).reshape(bs, ts, d)

    p = {
        "mu": rwkv_mu[0], "w_r": bf(rwkv_w_rkv[0, 0]), "w_k": bf(rwkv_w_rkv[0, 1]), "w_v": bf(rwkv_w_rkv[0, 2]),
        "w1": bf(rwkv_w1[0]), "w2": bf(rwkv_w2[0]), "a1": bf(rwkv_a1[0]), "a2": bf(rwkv_a2[0]),
        "g1": bf(rwkv_g1[0]), "g2": bf(rwkv_g2[0]), "w_o": bf(rwkv_w_o[0]),
        "proj_vecs": jnp.stack([rwkv_w0[0], rwkv_a0[0], rwkv_k_k[0], rwkv_k_a[0], rwkv_r_k[0].reshape(d)]),
        "out_vecs": jnp.stack([rwkv_gn_g[0], rwkv_gn_b[0], ln_mix_g[1], ln_mix_b[1]]),
    }
    x3 = jnp.zeros((n_prompt + n_sample, d), F32)
    x3, wkv_p = _rwkv_layer(xp2, jnp.zeros((bp, d), F32), jnp.zeros((bp, d // HEAD_DIM, HEAD_DIM, HEAD_DIM), F32),
                            p, x3, 0)
    x3, wkv_s = _rwkv_layer(xs2, state_shift[0], state_wkv[0], p, x3, n_prompt)
    y_p, y_s = _moe_layer(x3, n_prompt, moe_w_router[0], bf(moe_w_gate_up[0]), bf(moe_w_down[0]),
                          ln_ffn_g[1], ln_ffn_b[1], _row_tile(n_sample, 512))
    return (y_p.reshape(bp, tp, d), y_s.reshape(bs, ts, d),
            k_p[None], v_p[None], wkv_p[None], xp2[:, -1][None],
            k_s[None], v_s[None], wkv_s[None], xs2[:, -1][None])
```

```python
import functools

import jax
import jax.numpy as jnp
import numpy as np
from jax import lax
from jax.experimental import pallas as pl
from jax.experimental.pallas import tpu as pltpu

F32 = jnp.float32
BF16 = jnp.bfloat16

HEAD_DIM = 64
LANES = 128
PAIR = 2 * HEAD_DIM
LN_EPS = 1e-5
GN_EPS = 64e-5
DEEPNORM_ALPHA = 4.0 ** 0.25
Q_SCALE = HEAD_DIM ** -0.5 * 1.4426950408889634
SCAN_CHUNK = 64
V7X_VMEM_LIMIT = 56 * 1024 * 1024


def _params(*sem, vmem=V7X_VMEM_LIMIT):
    return pltpu.CompilerParams(dimension_semantics=sem, vmem_limit_bytes=vmem)


def _layer_norm(y, g, b):
    mu = jnp.mean(y, axis=-1, keepdims=True)
    c = y - mu
    var = jnp.mean(c * c, axis=-1, keepdims=True)
    return c * lax.rsqrt(var + LN_EPS) * g + b


def _split2(x):
    hi = x.astype(BF16)
    lo = (x - hi.astype(F32)).astype(BF16)
    return hi, lo


def _dot(a, b):
    return jnp.dot(a, b, preferred_element_type=F32)


def _dot_nt(a, b):
    return lax.dot_general(a, b, (((1,), (1,)), ((), ())), preferred_element_type=F32)


def _dot_tn(a, b):
    return lax.dot_general(a, b, (((0,), (0,)), ((), ())), preferred_element_type=F32)


def _dot3(a, b):
    ah, al = _split2(a)
    bh, bl = _split2(b)
    return _dot(jnp.concatenate([ah, ah, al], axis=1), jnp.concatenate([bh, bl, bh], axis=0))


def _dot3_nt(a, b):
    ah, al = _split2(a)
    bh, bl = _split2(b)
    return _dot_nt(jnp.concatenate([ah, ah, al], axis=1), jnp.concatenate([bh, bl, bh], axis=1))


def _dot3_tn(a, b):
    ah, al = _split2(a)
    bh, bl = _split2(b)
    return _dot_tn(jnp.concatenate([ah, ah, al], axis=0), jnp.concatenate([bh, bl, bh], axis=0))


def _dot2_exact_rhs(a, b_bf16):
    ah, al = _split2(a)
    return _dot(jnp.concatenate([ah, al], axis=1), jnp.concatenate([b_bf16, b_bf16], axis=0))


def _head_ones(n):
    r = lax.broadcasted_iota(jnp.int32, (n, n), 0) >> 6
    c = lax.broadcasted_iota(jnp.int32, (n, n), 1) >> 6
    return (r == c).astype(BF16)


def _head_sum(x, ones_pair):
    d = x.shape[1]
    cols = [_dot2_exact_rhs(x[:, c:c + PAIR], ones_pair) for c in range(0, d, PAIR)]
    return jnp.concatenate(cols, axis=1)


def _qkv_kernel(x_ref, wq_ref, wk_ref, wv_ref, qb_ref, kb_ref, vb_ref, kf_ref, vf_ref, xb_scr, *, hpb):
    @pl.when(pl.program_id(2) == 0)
    def _():
        xb_scr[...] = x_ref[0].astype(BF16)

    xb = xb_scr[...]
    q = _dot(xb, wq_ref[...]) * Q_SCALE
    k = _dot(xb, wk_ref[...])
    v = _dot(xb, wv_ref[...])
    qb_ref[0] = q.astype(BF16)
    kb_ref[0] = k.astype(BF16)
    vb_ref[0] = v.astype(BF16)
    for h in range(hpb):
        sl = slice(h * HEAD_DIM, (h + 1) * HEAD_DIM)
        kf_ref[0, h] = k[:, sl]
        vf_ref[0, h] = v[:, sl]


def _qkv_proj(x, w_bf16, tm):
    b, t, d = x.shape
    h = d // HEAD_DIM
    hpb = 4
    tn = hpb * HEAD_DIM
    nb = d // tn
    grid = (b, t // tm, nb)
    hspec = pl.BlockSpec((1, hpb, tm, HEAD_DIM), lambda bi, ti, ni: (bi, ni, ti, 0))
    shp = (b, h, t, HEAD_DIM)
    return pl.pallas_call(
        functools.partial(_qkv_kernel, hpb=hpb),
        grid=grid,
        in_specs=[
            pl.BlockSpec((1, tm, d), lambda bi, ti, ni: (bi, ti, 0)),
            pl.BlockSpec((d, tn), lambda bi, ti, ni: (0, ni)),
            pl.BlockSpec((d, tn), lambda bi, ti, ni: (0, nb + ni)),
            pl.BlockSpec((d, tn), lambda bi, ti, ni: (0, 2 * nb + ni)),
        ],
        out_specs=[pl.BlockSpec((1, tm, tn), lambda bi, ti, ni: (bi, ti, ni))] * 3 + [hspec] * 2,
        out_shape=[jax.ShapeDtypeStruct((b, t, d), BF16)] * 3 + [jax.ShapeDtypeStruct(shp, F32)] * 2,
        scratch_shapes=[pltpu.VMEM((tm, d), BF16)],
        compiler_params=_params("parallel", "parallel", "arbitrary"),
        name="qkv_proj",
    )(x, w_bf16, w_bf16, w_bf16)


def _aligned(x, m):
    return x if isinstance(x, int) else pl.multiple_of(x, m)


ATTN_STAGES = 5
MASKED_SCORE = -1.0e30


def _attn_schedule(nq, per_q, past_blocks, bq, bk):
    items = []
    for qb in range(nq):
        n_full = past_blocks + qb * per_q
        blocks = [(n_full + j, j + 1) for j in reversed(range(per_q))] + [(kb, 0) for kb in reversed(range(n_full))]
        for i, (kb, bias_id) in enumerate(blocks):
            items.append((qb * bq, kb * bk, bias_id, int(i == 0), int(i == len(blocks) - 1)))
    pad = [(0, 0, 0, 1, 0)] * (ATTN_STAGES - 1)
    return pad + items + pad, len(items)


def _sb_attn_kernel(tab_ref, q_ref, k_ref, v_ref, o_ref, z_scr, lb_scr, x2_scr, sums_scr, a_scr, later_scr,
                    acc_scr, bias_scr, suffix_scr, *, bq, bk, n_items, tab_len):
    per_q = bq // bk
    r = lax.broadcasted_iota(jnp.int32, (2 * bk, 2 * bk), 0)
    c = lax.broadcasted_iota(jnp.int32, (2 * bk, 2 * bk), 1)
    r = jnp.where(r >= bk, r - bk, r)
    suffix_scr[...] = ((c >= bk) | (r > c)).astype(BF16)
    qi = lax.broadcasted_iota(jnp.int32, (bq, bk), 0)
    kj = lax.broadcasted_iota(jnp.int32, (bq, bk), 1)
    bias_scr[0] = jnp.zeros((bq, bk), F32)
    for j in range(per_q):
        bias_scr[j + 1] = jnp.where((kj + j * bk) < qi, 0.0, MASKED_SCORE)
    for scr in (z_scr, lb_scr, x2_scr, sums_scr, a_scr, later_scr, acc_scr):
        scr[...] = jnp.zeros_like(scr)
    lane_a = lax.broadcasted_iota(jnp.int32, (bk, PAIR), 1) < HEAD_DIM

    def field(f, i):
        return tab_ref[f * tab_len + i]

    def stack_heads(blk):
        zero = jnp.zeros_like(blk)
        return jnp.concatenate([jnp.where(lane_a, blk, zero), jnp.where(lane_a, zero, blk)], axis=0)

    def body(t, carry):
        k5 = pl.multiple_of(field(1, t), bk)
        contrib = _dot(a_scr[...], stack_heads(v_ref[0, pl.ds(k5, bk), :]))
        acc_scr[...] = jnp.where(field(3, t) == 1, contrib, acc_scr[...] + contrib)
        slot = t & 1
        restart = field(3, t + 1) == 1
        for h in range(2):
            cols = slice(h * bk, (h + 1) * bk)
            later = jnp.where(restart, 0.0, later_scr[:, cols])
            between = sums_scr[h, :, :bk] + later
            later_scr[:, cols] = later + sums_scr[h, :, bk:]
            a_scr[:, cols] = jnp.exp2(lb_scr[slot, :, cols] + between).astype(BF16)
        for h in range(2):
            sums_scr[h] = _dot(x2_scr[h], suffix_scr[...])
        bias = bias_scr[field(2, t + 3)]
        for h in range(2):
            cols = slice(h * bk, (h + 1) * bk)
            z = z_scr[:, cols] + bias
            soft = jnp.log2(1.0 + jnp.exp2(-jnp.abs(z)))
            log_beta = jnp.minimum(z, 0.0) - soft
            hi, lo = _split2(log_beta - z)
            lb_scr[slot, :, cols] = log_beta
            x2_scr[h] = jnp.concatenate([hi, lo], axis=1)
        q1 = pl.multiple_of(field(0, t + 4), bq)
        k1 = pl.multiple_of(field(1, t + 4), bk)
        z_scr[...] = _dot_nt(q_ref[0, pl.ds(q1, bq), :], stack_heads(k_ref[0, pl.ds(k1, bk), :]))

        @pl.when(field(4, t) == 1)
        def _():
            o_ref[0, pl.ds(pl.multiple_of(field(0, t), bq), bq), :] = acc_scr[...].astype(BF16)

        return carry

    lax.fori_loop(0, n_items + ATTN_STAGES - 1, body, 0)


def _sb_attention(q, k, v, bq, bk, q_start):
    b, tq, d = q.shape
    tk = k.shape[1]
    assert tq % bq == 0 and bq % bk == 0 and q_start % bk == 0 and tk == q_start + tq
    per_q = bq // bk
    items, n_items = _attn_schedule(tq // bq, per_q, q_start // bk, bq, bk)
    table = jnp.asarray(np.asarray(items, np.int32).T.reshape(-1))
    qspec = pl.BlockSpec((1, tq, PAIR), lambda bi, pi, tab: (bi, 0, pi))
    kspec = pl.BlockSpec((1, tk, PAIR), lambda bi, pi, tab: (bi, 0, pi))
    grid_spec = pltpu.PrefetchScalarGridSpec(
        num_scalar_prefetch=1,
        grid=(b, d // PAIR),
        in_specs=[qspec, kspec, kspec],
        out_specs=qspec,
        scratch_shapes=[pltpu.VMEM((bq, 2 * bk), F32), pltpu.VMEM((2, bq, 2 * bk), F32),
                        pltpu.VMEM((2, bq, 2 * bk), BF16), pltpu.VMEM((2, bq, 2 * bk), F32),
                        pltpu.VMEM((bq, 2 * bk), BF16), pltpu.VMEM((bq, 2 * bk), F32),
                        pltpu.VMEM((bq, PAIR), F32), pltpu.VMEM((per_q + 1, bq, bk), F32),
                        pltpu.VMEM((2 * bk, 2 * bk), BF16)],
    )
    return pl.pallas_call(
        functools.partial(_sb_attn_kernel, bq=bq, bk=bk, n_items=n_items, tab_len=len(items)),
        grid_spec=grid_spec,
        out_shape=jax.ShapeDtypeStruct((b, tq, d), BF16),
        compiler_params=_params("parallel", "parallel"),
        name="sb_attention",
    )(table, q, k, v)


def _mm_res_ln_kernel(a_ref, w_ref, x_ref, g_ref, b_ref, o_ref):
    y = _dot(a_ref[...], w_ref[...])
    o_ref[...] = _layer_norm(DEEPNORM_ALPHA * x_ref[...] + y, g_ref[...], b_ref[...])


def _mm_res_ln(a, w_bf16, x, g, b, tm):
    t, d = x.shape
    kdim = a.shape[1]
    vec = pl.BlockSpec((1, d), lambda i: (0, 0))
    return pl.pallas_call(
        _mm_res_ln_kernel,
        grid=(t // tm,),
        in_specs=[pl.BlockSpec((tm, kdim), lambda i: (i, 0)),
                  pl.BlockSpec((kdim, d), lambda i: (0, 0)),
                  pl.BlockSpec((tm, d), lambda i: (i, 0)), vec, vec],
        out_specs=pl.BlockSpec((tm, d), lambda i: (i, 0)),
        out_shape=jax.ShapeDtypeStruct((t, d), F32),
        compiler_params=_params("parallel"),
        name="attn_out_ln",
    )(a, w_bf16, x, g.reshape(1, d), b.reshape(1, d))


def _swiglu_step(xb, wg_ref, wu_ref, wd_ref):
    gate = _dot(xb, wg_ref[...])
    up = _dot(xb, wu_ref[...])
    hidden = gate * jax.nn.sigmoid(gate) * up
    return _dot(hidden.astype(BF16), wd_ref[...])


def _ffn_ln_kernel(x_ref, wg_ref, wu_ref, wd_ref, g_ref, b_ref, o_ref, xb_scr, acc_scr):
    j = pl.program_id(1)

    @pl.when(j == 0)
    def _():
        xb_scr[...] = x_ref[...].astype(BF16)
        acc_scr[...] = jnp.zeros_like(acc_scr)

    acc_scr[...] += _swiglu_step(xb_scr[...], wg_ref, wu_ref, wd_ref)

    @pl.when(j == pl.num_programs(1) - 1)
    def _():
        o_ref[...] = _layer_norm(DEEPNORM_ALPHA * x_ref[...] + acc_scr[...], g_ref[...], b_ref[...])


def _ffn_ln(x, w_gate_up_bf16, w_down_bf16, g, b, tm, tn):
    t, d = x.shape
    f = w_down_bf16.shape[0]
    nj = f // tn
    vec = pl.BlockSpec((1, d), lambda i, j: (0, 0))
    return pl.pallas_call(
        _ffn_ln_kernel,
        grid=(t // tm, nj),
        in_specs=[pl.BlockSpec((tm, d), lambda i, j: (i, 0)),
                  pl.BlockSpec((d, tn), lambda i, j: (0, j)),
                  pl.BlockSpec((d, tn), lambda i, j: (0, nj + j)),
                  pl.BlockSpec((tn, d), lambda i, j: (j, 0)), vec, vec],
        out_specs=pl.BlockSpec((tm, d), lambda i, j: (i, 0)),
        out_shape=jax.ShapeDtypeStruct((t, d), F32),
        scratch_shapes=[pltpu.VMEM((tm, d), BF16), pltpu.VMEM((tm, d), F32)],
        compiler_params=_params("parallel", "arbitrary"),
        name="dense_swiglu_ln",
    )(x, w_gate_up_bf16, w_gate_up_bf16, w_down_bf16, g.reshape(1, d), b.reshape(1, d))


def _moe_ffn_kernel(te_ref, nt_ref, x_ref, wg_ref, wu_ref, wd_ref, o_ref, xb_scr, acc_scr):
    i = pl.program_id(0)
    j = pl.program_id(1)
    used = i < nt_ref[0]

    @pl.when(used & (j == 0))
    def _():
        xb_scr[...] = x_ref[...].astype(BF16)
        acc_scr[...] = jnp.zeros_like(acc_scr)

    @pl.when(used)
    def _():
        acc_scr[...] += _swiglu_step(xb_scr[...], wg_ref.at[0], wu_ref.at[0], wd_ref.at[0])

    @pl.when(j == pl.num_programs(1) - 1)
    def _():
        o_ref[...] = jnp.where(used, acc_scr[...], 0.0)


def _moe_ffn(tile_expert, n_tiles_used, xg, w_gate_up_bf16, w_down_bf16, tm, tn):
    n, d = xg.shape
    f = w_down_bf16.shape[1]
    nj = f // tn
    grid_spec = pltpu.PrefetchScalarGridSpec(
        num_scalar_prefetch=2,
        grid=(n // tm, nj),
        in_specs=[pl.BlockSpec((tm, d), lambda i, j, te, nt: (i, 0)),
                  pl.BlockSpec((1, d, tn), lambda i, j, te, nt: (te[i], 0, j)),
                  pl.BlockSpec((1, d, tn), lambda i, j, te, nt: (te[i], 0, nj + j)),
                  pl.BlockSpec((1, tn, d), lambda i, j, te, nt: (te[i], j, 0))],
        out_specs=pl.BlockSpec((tm, d), lambda i, j, te, nt: (i, 0)),
        scratch_shapes=[pltpu.VMEM((tm, d), BF16), pltpu.VMEM((tm, d), F32)],
    )
    return pl.pallas_call(
        _moe_ffn_kernel,
        grid_spec=grid_spec,
        out_shape=jax.ShapeDtypeStruct((n, d), F32),
        compiler_params=_params("parallel", "arbitrary"),
        name="expert_swiglu",
    )(tile_expert, n_tiles_used, xg, w_gate_up_bf16, w_gate_up_bf16, w_down_bf16)


def _row_copy(src_hbm, dst_ref, sem, src_row, dst_row):
    return pltpu.make_async_copy(src_hbm.at[pl.ds(src_row, 1)], dst_ref.at[pl.ds(dst_row, 1)], sem)


def _gather_rows_into(idx_ref, src_hbm, dst_ref, sem, n):
    def start(r, c):
        _row_copy(src_hbm, dst_ref, sem, idx_ref[0, 0, r], r).start()
        return c

    def wait(r, c):
        _row_copy(src_hbm, dst_ref, sem, 0, r).wait()
        return c

    lax.fori_loop(0, n, start, 0)
    lax.fori_loop(0, n, wait, 0)


def _scatter_copy(x_ref, dst_hbm, sem, src_row, dst_row):
    return pltpu.make_async_copy(x_ref.at[pl.ds(src_row, 1)], dst_hbm.at[pl.ds(dst_row, 1)], sem)


def _scatter_kernel(p0_ref, p1_ref, x_ref, dst_in_hbm_unused, dst_hbm, sem, *, tm):
    def start(r, c):
        _scatter_copy(x_ref, dst_hbm, sem, r, p0_ref[0, 0, r]).start()
        _scatter_copy(x_ref, dst_hbm, sem, r, p1_ref[0, 0, r]).start()
        return c

    def wait(r, c):
        _scatter_copy(x_ref, dst_hbm, sem, r, 0).wait()
        _scatter_copy(x_ref, dst_hbm, sem, r, 0).wait()
        return c

    lax.fori_loop(0, tm, start, 0)
    lax.fori_loop(0, tm, wait, 0)


def _scatter_rows(x, pos0, pos1, dst, tm):
    t, d = x.shape
    idx = pl.BlockSpec((1, 1, tm), lambda i: (i, 0, 0), memory_space=pltpu.SMEM)
    return pl.pallas_call(
        functools.partial(_scatter_kernel, tm=tm),
        grid=(t // tm,),
        in_specs=[idx, idx, pl.BlockSpec((tm, d), lambda i: (i, 0)), pl.BlockSpec(memory_space=pl.ANY)],
        out_specs=pl.BlockSpec(memory_space=pl.ANY),
        out_shape=jax.ShapeDtypeStruct(dst.shape, dst.dtype),
        input_output_aliases={3: 0},
        scratch_shapes=[pltpu.SemaphoreType.DMA(())],
        compiler_params=_params("arbitrary"),
        name="scatter_rows",
    )(pos0.reshape(-1, 1, tm), pos1.reshape(-1, 1, tm), x, dst)


def _combine_ln_kernel(p0_ref, p1_ref, x_ref, route_ref, y_hbm, g_ref, b_ref, op_ref, os_ref, a_scr, b_scr,
                       sem0, sem1, *, tm, n_prompt_tiles):
    _gather_rows_into(p0_ref, y_hbm, a_scr, sem0, tm)
    _gather_rows_into(p1_ref, y_hbm, b_scr, sem1, tm)
    moe = route_ref[:, 2:3] * a_scr[...] + route_ref[:, 3:4] * b_scr[...]
    out = _layer_norm(DEEPNORM_ALPHA * x_ref[...] + moe, g_ref[...], b_ref[...])
    i = pl.program_id(0)

    @pl.when(i < n_prompt_tiles)
    def _():
        op_ref[...] = out

    @pl.when(i >= n_prompt_tiles)
    def _():
        os_ref[...] = out


def _combine_ln(x, route, y_sorted, pos0, pos1, g, b, tm, n_prompt):
    t, d = x.shape
    npt = n_prompt // tm
    nst = (t - n_prompt) // tm
    idx = pl.BlockSpec((1, 1, tm), lambda i: (i, 0, 0), memory_space=pltpu.SMEM)
    vec = pl.BlockSpec((1, d), lambda i: (0, 0))
    return pl.pallas_call(
        functools.partial(_combine_ln_kernel, tm=tm, n_prompt_tiles=npt),
        grid=(npt + nst,),
        in_specs=[idx, idx, pl.BlockSpec((tm, d), lambda i: (i, 0)), pl.BlockSpec((tm, LANES), lambda i: (i, 0)),
                  pl.BlockSpec(memory_space=pl.ANY), vec, vec],
        out_specs=[pl.BlockSpec((tm, d), lambda i: (jnp.minimum(i, npt - 1), 0)),
                   pl.BlockSpec((tm, d), lambda i: (jnp.maximum(i - npt, 0), 0))],
        out_shape=[jax.ShapeDtypeStruct((n_prompt, d), F32), jax.ShapeDtypeStruct((t - n_prompt, d), F32)],
        scratch_shapes=[pltpu.VMEM((tm, d), F32), pltpu.VMEM((tm, d), F32),
                        pltpu.SemaphoreType.DMA(()), pltpu.SemaphoreType.DMA(())],
        compiler_params=_params("arbitrary"),
        name="moe_combine_ln",
    )(pos0.reshape(-1, 1, tm), pos1.reshape(-1, 1, tm), x, route, y_sorted, g.reshape(1, d), b.reshape(1, d))


def _rwkv_proj_kernel(x_ref, prev_ref, mu_ref, wr_ref, wk_ref, wv_ref, w1_ref, w2_ref, a1_ref, a2_ref,
                      g1_ref, g2_ref, vec_ref,
                      r_ref, lw_ref, k_ref, v_ref, kk_ref, b_ref, g_ref, bonus_ref, *, seg):
    x = x_ref[...]
    tm, d = x.shape
    rolled = pltpu.roll(x, 1, 0).reshape(tm // seg, seg, d)
    first = lax.broadcasted_iota(jnp.int32, (tm // seg, seg, d), 1) == 0
    x_prev = jnp.where(first, prev_ref[...], rolled).reshape(tm, d)
    xx = x_prev - x

    def mixed(i):
        return (x + xx * mu_ref[i:i + 1, :]).astype(BF16)

    r = _dot(mixed(0), wr_ref[...])
    k = _dot(mixed(2), wk_ref[...])
    v = _dot(mixed(3), wv_ref[...])
    w0, a0, k_k, k_a, r_k = (vec_ref[i:i + 1, :] for i in range(5))
    wl = w0 + _dot(jnp.tanh(_dot(mixed(1), w1_ref[...])).astype(BF16), w2_ref[...])
    softplus_neg = jnp.maximum(-wl, 0.0) + jnp.log(1.0 + jnp.exp(-jnp.abs(wl)))
    lw_ref[...] = -jnp.exp(-softplus_neg - 0.5)
    a = jax.nn.sigmoid(a0 + _dot(_dot(mixed(4), a1_ref[...]).astype(BF16), a2_ref[...]))
    g_ref[...] = _dot(jax.nn.sigmoid(_dot(mixed(5), g1_ref[...])).astype(BF16), g2_ref[...])
    ones_pair = _head_ones(PAIR)
    kk = k * k_k
    kk = kk / jnp.maximum(jnp.sqrt(_head_sum(kk * kk, ones_pair)), 1e-12)
    k = k * (1.0 + (a - 1.0) * k_a)
    r_ref[...] = r
    k_ref[...] = k
    v_ref[...] = v
    kk_ref[...] = kk
    b_ref[...] = kk * a
    bonus_ref[...] = _head_sum(r * k * r_k, ones_pair) * v


def _rwkv_proj(x, prev_rows, mu, w_r, w_k, w_v, w1, w2, a1, a2, g1, g2, vecs, tm, seg):
    t, d = x.shape
    row = pl.BlockSpec((tm, d), lambda i: (i, 0))

    def full(a):
        return pl.BlockSpec(a.shape, lambda i: (0,) * a.ndim)

    consts = (mu, w_r, w_k, w_v, w1, w2, a1, a2, g1, g2, vecs)
    return pl.pallas_call(
        functools.partial(_rwkv_proj_kernel, seg=seg),
        grid=(t // tm,),
        in_specs=[row, pl.BlockSpec((tm // seg, 1, d), lambda i: (i, 0, 0))] + [full(c) for c in consts],
        out_specs=[row] * 8,
        out_shape=[jax.ShapeDtypeStruct((t, d), F32)] * 8,
        compiler_params=_params("parallel"),
        name="rwkv_proj",
    )(x, prev_rows, *consts)


def _unit_lower_inverse(low, limit):
    n = low.shape[0]
    r = lax.broadcasted_iota(jnp.int32, (n, n), 0)
    c = lax.broadcasted_iota(jnp.int32, (n, n), 1)
    eye = (r == c).astype(F32)
    neg = jnp.where((r >> 3) == (c >> 3), -low, 0.0)
    inv = eye + neg
    p = neg
    for _ in range(2):
        p = _dot3(p, p)
        inv = inv + _dot3(inv, p)
    shift = 3
    while (1 << shift) < limit:
        off = jnp.where(((r >> (shift + 1)) == (c >> (shift + 1))) & ((r >> shift) != (c >> shift)), low, 0.0)
        inv = inv - _dot3(_dot3(inv, off), inv)
        shift += 1
    return inv


def _rwkv_scan_kernel(r_ref, lw_ref, k_ref, v_ref, kk_ref, b_ref, h0_ref, o_ref, hT_ref, h_scr, *, n_chunks):
    cs = SCAN_CHUNK
    tb = pl.program_id(2)

    @pl.when(tb == 0)
    def _():
        h_scr[...] = h0_ref[0, 0]

    lane = lax.broadcasted_iota(jnp.int32, (cs, PAIR), 1)
    row = lax.broadcasted_iota(jnp.int32, (cs, PAIR), 0)
    head_a = lane < HEAD_DIM
    strict = (lane & (HEAD_DIM - 1)) < row
    incl = (lane & (HEAD_DIM - 1)) <= row
    tri = (lax.broadcasted_iota(jnp.int32, (cs, cs), 1) <= lax.broadcasted_iota(jnp.int32, (cs, cs), 0)).astype(BF16)
    pr = lax.broadcasted_iota(jnp.int32, (PAIR, PAIR), 0) >> 6
    pc = lax.broadcasted_iota(jnp.int32, (PAIR, PAIR), 1) >> 6
    same_head = pr == pc

    def stack(x):
        return jnp.concatenate([jnp.where(head_a, x, 0.0), jnp.where(head_a, 0.0, x)], axis=0)

    def chunk(ci, carry):
        t0 = pl.multiple_of(ci * cs, cs)
        sl = pl.ds(t0, cs)
        r, lw, k, v, kk, b = (ref[0, sl, :] for ref in (r_ref, lw_ref, k_ref, v_ref, kk_ref, b_ref))
        h = h_scr[...]
        l1 = lw.astype(BF16)
        rem = lw - l1.astype(F32)
        l2 = rem.astype(BF16)
        l3 = (rem - l2.astype(F32)).astype(BF16)
        parts = _dot(tri, jnp.concatenate([l1, l2, l3], axis=1))
        cum = parts[:, :PAIR] + parts[:, PAIR:2 * PAIR] + parts[:, 2 * PAIR:]
        total = cum[cs - 1:cs, :]
        grow = jnp.exp(-cum)
        r_hat = r * jnp.exp(cum)
        kk_hat = kk * jnp.exp(cum - lw)
        k_hat = k * grow
        b_hat = b * grow
        to_end = jnp.exp(total - cum)
        lhs = jnp.concatenate([kk_hat, r_hat], axis=0)
        rhs = jnp.concatenate([stack(b_hat), stack(k_hat)], axis=0)
        gram = _dot3_nt(lhs, rhs)
        l_b = jnp.where(strict, gram[:cs, :PAIR], 0.0)
        l_k = jnp.where(strict, gram[:cs, PAIR:], 0.0)
        a_rb = jnp.where(incl, gram[cs:, :PAIR], 0.0)
        a_rk = jnp.where(incl, gram[cs:, PAIR:], 0.0)
        inv = _unit_lower_inverse(stack(l_b), cs)
        v_st = stack(v)
        w = _dot3(jnp.concatenate([kk_hat, l_k], axis=1), jnp.concatenate([h, v_st], axis=0))
        u_st = _dot3(inv, stack(w))
        o = _dot3(jnp.concatenate([r_hat, a_rk, -a_rb], axis=1), jnp.concatenate([h, v_st, u_st], axis=0))
        o_ref[0, sl, :] = o
        u = u_st[:cs] + u_st[cs:]
        upd = _dot3_tn(jnp.concatenate([k * to_end, -(b * to_end)], axis=0), jnp.concatenate([v, u], axis=0))
        decay_col = jnp.transpose(jnp.broadcast_to(jnp.exp(total), (PAIR, PAIR)))
        h_scr[...] = decay_col * h + jnp.where(same_head, upd, 0.0)
        return carry

    lax.fori_loop(0, n_chunks, chunk, 0)

    @pl.when(tb == pl.num_programs(2) - 1)
    def _():
        hT_ref[0, 0] = h_scr[...]


def _rwkv_scan(r, lw, k, v, kk, b, h0, tb):
    bsz, t, d = r.shape
    npair = d // PAIR
    seq = pl.BlockSpec((1, tb, PAIR), lambda bi, pi, ti: (bi, ti, pi))
    st = pl.BlockSpec((1, 1, PAIR, PAIR), lambda bi, pi, ti: (bi, pi, 0, 0))
    return pl.pallas_call(
        functools.partial(_rwkv_scan_kernel, n_chunks=tb // SCAN_CHUNK),
        grid=(bsz, npair, t // tb),
        in_specs=[seq] * 6 + [st],
        out_specs=[seq, st],
        out_shape=[jax.ShapeDtypeStruct((bsz, t, d), F32), jax.ShapeDtypeStruct((bsz, npair, PAIR, PAIR), F32)],
        scratch_shapes=[pltpu.VMEM((PAIR, PAIR), F32)],
        compiler_params=_params("parallel", "parallel", "arbitrary"),
        name="rwkv_scan",
    )(r, lw, k, v, kk, b, h0)


def _rwkv_out_kernel(o_ref, g_ref, bonus_ref, x_ref, wo_ref, vec_ref, dst_hbm_unused, out_ref):
    o = o_ref[...]
    ones_pair = _head_ones(PAIR)
    mean = _head_sum(o, ones_pair) * (1.0 / HEAD_DIM)
    c = o - mean
    var = _head_sum(c * c, ones_pair) * (1.0 / HEAD_DIM)
    on = c * lax.rsqrt(var + GN_EPS) * vec_ref[0:1, :] + vec_ref[1:2, :]
    y = _dot(((on + bonus_ref[...]) * g_ref[...]).astype(BF16), wo_ref[...])
    out_ref[...] = _layer_norm(DEEPNORM_ALPHA * x_ref[...] + y, vec_ref[2:3, :], vec_ref[3:4, :])


def _rwkv_out(o, g, bonus, x, w_o_bf16, vecs, dst, row_offset, tm):
    t, d = x.shape
    row = pl.BlockSpec((tm, d), lambda i: (i, 0))
    off = row_offset // tm
    return pl.pallas_call(
        _rwkv_out_kernel,
        grid=(t // tm,),
        in_specs=[row, row, row, row, pl.BlockSpec((d, d), lambda i: (0, 0)),
                  pl.BlockSpec(vecs.shape, lambda i: (0, 0)), pl.BlockSpec(memory_space=pl.ANY)],
        out_specs=pl.BlockSpec((tm, d), lambda i: (off + i, 0)),
        out_shape=jax.ShapeDtypeStruct(dst.shape, F32),
        input_output_aliases={6: 0},
        compiler_params=_params("parallel"),
        name="rwkv_out_ln",
    )(o, g, bonus, x, w_o_bf16, vecs, dst)


def _router_kernel(x_ref, w_ref, o_ref, *, n_experts):
    logits = _dot3(x_ref[...], w_ref[...])
    lane = lax.broadcasted_iota(jnp.int32, logits.shape, 1)
    lane_f = lane.astype(F32)
    lowest = jnp.float32(-3.0e38)
    logits = jnp.where(lane < n_experts, logits, lowest)
    m1 = jnp.max(logits, axis=1, keepdims=True)
    i1 = jnp.min(jnp.where(logits == m1, lane_f, float(LANES)), axis=1, keepdims=True)
    rest = jnp.where(lane_f == i1, lowest, logits)
    m2 = jnp.max(rest, axis=1, keepdims=True)
    i2 = jnp.min(jnp.where(rest == m2, lane_f, float(LANES)), axis=1, keepdims=True)
    e = jnp.exp(m2 - m1)
    g1 = 1.0 / (1.0 + e)
    g2 = e * g1
    out = jnp.where(lane == 0, i1, 0.0)
    out = jnp.where(lane == 1, i2, out)
    out = jnp.where(lane == 2, g1, out)
    o_ref[...] = jnp.where(lane == 3, g2, out)


def _router(x, w_router, tm):
    t, d = x.shape
    e = w_router.shape[1]
    w_pad = jnp.zeros((d, LANES), F32).at[:, :e].set(w_router)
    return pl.pallas_call(
        functools.partial(_router_kernel, n_experts=e),
        grid=(t // tm,),
        in_specs=[pl.BlockSpec((tm, d), lambda i: (i, 0)), pl.BlockSpec((d, LANES), lambda i: (0, 0))],
        out_specs=pl.BlockSpec((tm, LANES), lambda i: (i, 0)),
        out_shape=jax.ShapeDtypeStruct((t, LANES), F32),
        compiler_params=_params("parallel"),
        name="router_top2",
    )(x, w_pad)


def _row_tile(t, cap):
    tm = cap
    while t % tm:
        tm //= 2
    return tm


def _pair_states(s):
    b, h, n, _ = s.shape
    st = jnp.swapaxes(s, 2, 3).reshape(b, h // 2, 2, n, n)
    z = jnp.zeros_like(st[:, :, 0])
    top = jnp.concatenate([st[:, :, 0], z], axis=-1)
    bot = jnp.concatenate([z, st[:, :, 1]], axis=-1)
    return jnp.concatenate([top, bot], axis=-2)


def _unpair_states(hp):
    b, p, _, _ = hp.shape
    n = HEAD_DIM
    st = jnp.stack([hp[:, :, :n, :n], hp[:, :, n:, n:]], axis=2).reshape(b, 2 * p, n, n)
    return jnp.swapaxes(st, 2, 3)


def _attention_layer(x, w_qkv, w_o, ln_g, ln_b, cache_k=None, cache_v=None):
    b, t, d = x.shape
    q, kb, vb, kf, vf = _qkv_proj(x, w_qkv, _row_tile(t, 512))
    if cache_k is None:
        ao = _sb_attention(q, kb, vb, _row_tile(t, 512), LANES, 0)
    else:
        past = cache_k.shape[2]
        merge = lambda c: jnp.swapaxes(c.astype(BF16), 1, 2).reshape(b, past, d)
        keys = jnp.concatenate([merge(cache_k), kb], axis=1)
        vals = jnp.concatenate([merge(cache_v), vb], axis=1)
        ao = _sb_attention(q, keys, vals, t, t, past)
    x1 = _mm_res_ln(ao.reshape(b * t, d), w_o, x.reshape(b * t, d), ln_g, ln_b, _row_tile(b * t, 512))
    return x1, kf, vf


def _rwkv_layer(x, x_last, s0, p, dst, row_offset):
    b, t, d = x.shape
    seg = SCAN_CHUNK
    xs = x.reshape(b, t // seg, seg, d)
    prev = jnp.concatenate([x_last[:, None, :], xs[:, :-1, seg - 1, :]], axis=1).reshape(b * t // seg, 1, d)
    xf = x.reshape(b * t, d)
    tm = _row_tile(b * t, 256)
    r, lw, k, v, kk, bb, g, bonus = _rwkv_proj(xf, prev, p["mu"], p["w_r"], p["w_k"], p["w_v"], p["w1"], p["w2"],
                                               p["a1"], p["a2"], p["g1"], p["g2"], p["proj_vecs"], tm, seg)
    shp = (b, t, d)
    o, h_t = _rwkv_scan(*(z.reshape(shp) for z in (r, lw, k, v, kk, bb)), _pair_states(s0), _row_tile(t, 512))
    dst = _rwkv_out(o.reshape(b * t, d), g, bonus, xf, p["w_o"], p["out_vecs"], dst, row_offset, tm)
    return dst, _unpair_states(h_t)


def _moe_layer(x, n_prompt, w_router, w_gate_up, w_down, ln_g, ln_b, tm):
    t, d = x.shape
    e = w_router.shape[1]
    route = _router(x, w_router, tm)
    experts = jnp.concatenate([route[:, 0], route[:, 1]]).astype(jnp.int32)
    onehot = (experts[:, None] == jnp.arange(e, dtype=jnp.int32)[None, :]).astype(jnp.int32)
    counts = jnp.sum(onehot, axis=0)
    padded = ((counts + tm - 1) // tm) * tm
    ends = jnp.cumsum(padded)
    starts = ends - padded
    rank = jnp.sum(jnp.cumsum(onehot, axis=0) * onehot, axis=1) - 1
    pos = (jnp.sum(starts[None, :] * onehot, axis=1) + rank).astype(jnp.int32)
    n_rows = ((2 * t + e * (tm - 1)) // tm) * tm
    tile_start = jnp.arange(n_rows // tm, dtype=jnp.int32) * tm
    tile_expert = jnp.minimum(jnp.sum((tile_start[:, None] >= ends[None, :]).astype(jnp.int32), axis=1), e - 1)
    n_tiles_used = (ends[-1] // tm).astype(jnp.int32).reshape(1)
    xg = _scatter_rows(x, pos[:t], pos[t:], jnp.zeros((n_rows, d), F32), tm)
    y = _moe_ffn(tile_expert, n_tiles_used, xg, w_gate_up, w_down, tm, 512)
    return _combine_ln(x, route, y, pos[:t], pos[t:], ln_g, ln_b, tm, n_prompt)


def kernel(x_prompt, x_sample, cache_k, cache_v, state_wkv, state_shift, att_w_qkv, att_w_o, ffn_w_gate_up, ffn_w_down, rwkv_mu, rwkv_w_rkv, rwkv_w0, rwkv_w1, rwkv_w2, rwkv_a0, rwkv_a1, rwkv_a2, rwkv_g1, rwkv_g2, rwkv_k_k, rwkv_k_a, rwkv_r_k, rwkv_gn_g, rwkv_gn_b, rwkv_w_o, moe_w_router, moe_w_gate_up, moe_w_down, ln_mix_g, ln_mix_b, ln_ffn_g, ln_ffn_b):
    bp, tp, d = x_prompt.shape
    bs, ts, _ = x_sample.shape
    n_prompt, n_sample = bp * tp, bs * ts
    bf = lambda a: a.astype(BF16)

    w_qkv, w_o = bf(att_w_qkv[0]), bf(att_w_o[0])
    xp1, k_p, v_p = _attention_layer(x_prompt, w_qkv, w_o, ln_mix_g[0], ln_mix_b[0])
    xs1, k_s, v_s = _attention_layer(x_sample, w_qkv, w_o, ln_mix_g[0], ln_mix_b[0], cache_k[0], cache_v[0])
    w_gu, w_dn = bf(ffn_w_gate_up[0]), bf(ffn_w_down[0])
    xp2 = _ffn_ln(xp1, w_gu, w_dn, ln_ffn_g[0], ln_ffn_b[0], _row_tile(n_prompt, 512), 256).reshape(bp, tp, d)
    xs2 = _ffn_ln(xs1, w_gu, w_dn, ln_ffn_g[0], ln_ffn_b[0], _row_tile(n_sample, 512), 256).reshape(bs, ts, d)

    p = {
        "mu": rwkv_mu[0], "w_r": bf(rwkv_w_rkv[0, 0]), "w_k": bf(rwkv_w_rkv[0, 1]), "w_v": bf(rwkv_w_rkv[0, 2]),
        "w1": bf(rwkv_w1[0]), "w2": bf(rwkv_w2[0]), "a1": bf(rwkv_a1[0]), "a2": bf(rwkv_a2[0]),
        "g1": bf(rwkv_g1[0]), "g2": bf(rwkv_g2[0]), "w_o": bf(rwkv_w_o[0]),
        "proj_vecs": jnp.stack([rwkv_w0[0], rwkv_a0[0], rwkv_k_k[0], rwkv_k_a[0], rwkv_r_k[0].reshape(d)]),
        "out_vecs": jnp.stack([rwkv_gn_g[0], rwkv_gn_b[0], ln_mix_g[1], ln_mix_b[1]]),
    }
    x3 = jnp.zeros((n_prompt + n_sample, d), F32)
    x3, wkv_p = _rwkv_layer(xp2, jnp.zeros((bp, d), F32), jnp.zeros((bp, d // HEAD_DIM, HEAD_DIM, HEAD_DIM), F32),
                            p, x3, 0)
    x3, wkv_s = _rwkv_layer(xs2, state_shift[0], state_wkv[0], p, x3, n_prompt)
    y_p, y_s = _moe_layer(x3, n_prompt, moe_w_router[0], bf(moe_w_gate_up[0]), bf(moe_w_down[0]),
                          ln_ffn_g[1], ln_ffn_b[1], _row_tile(n_sample, 512))
    return (y_p.reshape(bp, tp, d), y_s.reshape(bs, ts, d),
            k_p[None], v_p[None], wkv_p[None], xp2[:, -1][None],
            k_s[None], v_s[None], wkv_s[None], xs2[:, -1][None])
```

```python
import functools
import math

import jax
import jax.numpy as jnp
import numpy as np
from jax import lax
from jax.experimental import pallas as pl
from jax.experimental.pallas import tpu as pltpu

F32 = jnp.float32
BF16 = jnp.bfloat16

HEAD_DIM = 64
LANES = 128
PAIR = 2 * HEAD_DIM
LN_EPS = 1e-5
GN_EPS = 64e-5
DEEPNORM_ALPHA = 4.0 ** 0.25
Q_SCALE = HEAD_DIM ** -0.5 * 1.4426950408889634
SCAN_PAIRS_PER_STEP = 8
DMA_ISSUE_UNROLL = 8
SCAN_CHUNK = 64
V7X_VMEM_LIMIT = 56 * 1024 * 1024


def _params(*sem, vmem=V7X_VMEM_LIMIT):
    return pltpu.CompilerParams(dimension_semantics=sem, vmem_limit_bytes=vmem)


def _layer_norm(y, g, b):
    mu = jnp.mean(y, axis=-1, keepdims=True)
    c = y - mu
    var = jnp.mean(c * c, axis=-1, keepdims=True)
    return c * lax.rsqrt(var + LN_EPS) * g + b


def _split2(x):
    hi = x.astype(BF16)
    lo = (x - hi.astype(F32)).astype(BF16)
    return hi, lo


def _dot(a, b):
    return jnp.dot(a, b, preferred_element_type=F32)


def _dot_nt(a, b):
    return lax.dot_general(a, b, (((1,), (1,)), ((), ())), preferred_element_type=F32)


def _dot_tn(a, b):
    return lax.dot_general(a, b, (((0,), (0,)), ((), ())), preferred_element_type=F32)


def _dot3(a, b):
    ah, al = _split2(a)
    bh, bl = _split2(b)
    return _dot(jnp.concatenate([ah, ah, al], axis=1), jnp.concatenate([bh, bl, bh], axis=0))


def _dot2(a, b):
    ah, al = _split2(a)
    bb = b.astype(BF16)
    return _dot(jnp.concatenate([ah, al], axis=1), jnp.concatenate([bb, bb], axis=0))


def _dot2_nt(a, b):
    ah, al = _split2(a)
    bb = b.astype(BF16)
    return _dot_nt(jnp.concatenate([ah, al], axis=1), jnp.concatenate([bb, bb], axis=1))


def _dot2_tn(a, b):
    ah, al = _split2(a)
    bb = b.astype(BF16)
    return _dot_tn(jnp.concatenate([ah, al], axis=0), jnp.concatenate([bb, bb], axis=0))


def _dot2_exact_rhs(a, b_bf16):
    ah, al = _split2(a)
    return _dot(jnp.concatenate([ah, al], axis=1), jnp.concatenate([b_bf16, b_bf16], axis=0))


def _head_ones(n):
    r = lax.broadcasted_iota(jnp.int32, (n, n), 0) >> 6
    c = lax.broadcasted_iota(jnp.int32, (n, n), 1) >> 6
    return (r == c).astype(BF16)


def _head_sum(x, ones_pair):
    d = x.shape[1]
    cols = [_dot2_exact_rhs(x[:, c:c + PAIR], ones_pair) for c in range(0, d, PAIR)]
    return jnp.concatenate(cols, axis=1)


def _qkv_kernel(x_ref, wq_ref, wk_ref, wv_ref, qb_ref, kb_ref, vb_ref, kf_ref, vf_ref, xb_scr, *, hpb):
    @pl.when(pl.program_id(2) == 0)
    def _():
        xb_scr[...] = x_ref[0].astype(BF16)

    xb = xb_scr[...]
    q = _dot(xb, wq_ref[...]) * Q_SCALE
    k = _dot(xb, wk_ref[...])
    v = _dot(xb, wv_ref[...])
    qb_ref[0] = q.astype(BF16)
    kb_ref[0] = k.astype(BF16)
    vb_ref[0] = v.astype(BF16)
    for h in range(hpb):
        sl = slice(h * HEAD_DIM, (h + 1) * HEAD_DIM)
        kf_ref[0, h] = k[:, sl]
        vf_ref[0, h] = v[:, sl]


def _qkv_proj(x, w_bf16, tm):
    b, t, d = x.shape
    h = d // HEAD_DIM
    hpb = 4
    tn = hpb * HEAD_DIM
    nb = d // tn
    grid = (b, t // tm, nb)
    hspec = pl.BlockSpec((1, hpb, tm, HEAD_DIM), lambda bi, ti, ni: (bi, ni, ti, 0))
    shp = (b, h, t, HEAD_DIM)
    return pl.pallas_call(
        functools.partial(_qkv_kernel, hpb=hpb),
        grid=grid,
        in_specs=[
            pl.BlockSpec((1, tm, d), lambda bi, ti, ni: (bi, ti, 0)),
            pl.BlockSpec((d, tn), lambda bi, ti, ni: (0, ni)),
            pl.BlockSpec((d, tn), lambda bi, ti, ni: (0, nb + ni)),
            pl.BlockSpec((d, tn), lambda bi, ti, ni: (0, 2 * nb + ni)),
        ],
        out_specs=[pl.BlockSpec((1, tm, tn), lambda bi, ti, ni: (bi, ti, ni))] * 3 + [hspec] * 2,
        out_shape=[jax.ShapeDtypeStruct((b, t, d), BF16)] * 3 + [jax.ShapeDtypeStruct(shp, F32)] * 2,
        scratch_shapes=[pltpu.VMEM((tm, d), BF16)],
        compiler_params=_params("parallel", "parallel", "arbitrary"),
        name="qkv_proj",
    )(x, w_bf16, w_bf16, w_bf16)


def _aligned(x, m):
    return x if isinstance(x, int) else pl.multiple_of(x, m)


ATTN_STAGES = 5
MASKED_SCORE = -1.0e30


def _attn_schedule(nq, per_q, past_blocks, bq, bk):
    items = []
    for qb in range(nq):
        n_full = past_blocks + qb * per_q
        blocks = [(n_full + j, j + 1) for j in reversed(range(per_q))] + [(kb, 0) for kb in reversed(range(n_full))]
        for i, (kb, bias_id) in enumerate(blocks):
            items.append((qb * bq, kb * bk, bias_id, int(i == 0), int(i == len(blocks) - 1)))
    pad = [(0, 0, 0, 1, 0)] * (ATTN_STAGES - 1)
    return pad + items + pad, len(items)


def _sb_attn_kernel(tab_ref, q_ref, k_ref, v_ref, o_ref, z_scr, lb_scr, x2_scr, sums_scr, a_scr, later_scr,
                    acc_scr, bias_scr, suffix_scr, *, bq, bk, n_items, tab_len):
    per_q = bq // bk
    r = lax.broadcasted_iota(jnp.int32, (2 * bk, 2 * bk), 0)
    c = lax.broadcasted_iota(jnp.int32, (2 * bk, 2 * bk), 1)
    r = jnp.where(r >= bk, r - bk, r)
    suffix_scr[...] = ((c >= bk) | (r > c)).astype(BF16)
    qi = lax.broadcasted_iota(jnp.int32, (bq, bk), 0)
    kj = lax.broadcasted_iota(jnp.int32, (bq, bk), 1)
    bias_scr[0] = jnp.zeros((bq, bk), F32)
    for j in range(per_q):
        bias_scr[j + 1] = jnp.where((kj + j * bk) < qi, 0.0, MASKED_SCORE)
    for scr in (z_scr, lb_scr, x2_scr, sums_scr, a_scr, later_scr, acc_scr):
        scr[...] = jnp.zeros_like(scr)
    lane_a = lax.broadcasted_iota(jnp.int32, (bk, PAIR), 1) < HEAD_DIM

    def field(f, i):
        return tab_ref[f * tab_len + i]

    def stack_heads(blk):
        zero = jnp.zeros_like(blk)
        return jnp.concatenate([jnp.where(lane_a, blk, zero), jnp.where(lane_a, zero, blk)], axis=0)

    def body(t, carry):
        k5 = pl.multiple_of(field(1, t), bk)
        contrib = _dot(a_scr[...], stack_heads(v_ref[0, pl.ds(k5, bk), :]))
        acc_scr[...] = jnp.where(field(3, t) == 1, contrib, acc_scr[...] + contrib)
        slot = t & 1
        restart = field(3, t + 1) == 1
        for h in range(2):
            cols = slice(h * bk, (h + 1) * bk)
            later = jnp.where(restart, 0.0, later_scr[:, cols])
            between = sums_scr[h, :, :bk] + later
            later_scr[:, cols] = later + sums_scr[h, :, bk:]
            a_scr[:, cols] = jnp.exp2(lb_scr[slot, :, cols] + between).astype(BF16)
        for h in range(2):
            sums_scr[h] = _dot(x2_scr[h], suffix_scr[...])
        bias = bias_scr[field(2, t + 3)]
        for h in range(2):
            cols = slice(h * bk, (h + 1) * bk)
            z = z_scr[:, cols] + bias
            soft = jnp.log2(1.0 + jnp.exp2(-jnp.abs(z)))
            log_beta = jnp.minimum(z, 0.0) - soft
            hi, lo = _split2(log_beta - z)
            lb_scr[slot, :, cols] = log_beta
            x2_scr[h] = jnp.concatenate([hi, lo], axis=1)
        q1 = pl.multiple_of(field(0, t + 4), bq)
        k1 = pl.multiple_of(field(1, t + 4), bk)
        z_scr[...] = _dot_nt(q_ref[0, pl.ds(q1, bq), :], stack_heads(k_ref[0, pl.ds(k1, bk), :]))

        @pl.when(field(4, t) == 1)
        def _():
            o_ref[0, pl.ds(pl.multiple_of(field(0, t), bq), bq), :] = acc_scr[...].astype(BF16)

        return carry

    lax.fori_loop(0, n_items + ATTN_STAGES - 1, body, 0)


def _sb_attention(q, k, v, bq, bk, q_start):
    b, tq, d = q.shape
    tk = k.shape[1]
    assert tq % bq == 0 and bq % bk == 0 and q_start % bk == 0 and tk == q_start + tq
    per_q = bq // bk
    items, n_items = _attn_schedule(tq // bq, per_q, q_start // bk, bq, bk)
    table = jnp.asarray(np.asarray(items, np.int32).T.reshape(-1))
    qspec = pl.BlockSpec((1, tq, PAIR), lambda bi, pi, tab: (bi, 0, pi))
    kspec = pl.BlockSpec((1, tk, PAIR), lambda bi, pi, tab: (bi, 0, pi))
    grid_spec = pltpu.PrefetchScalarGridSpec(
        num_scalar_prefetch=1,
        grid=(b, d // PAIR),
        in_specs=[qspec, kspec, kspec],
        out_specs=qspec,
        scratch_shapes=[pltpu.VMEM((bq, 2 * bk), F32), pltpu.VMEM((2, bq, 2 * bk), F32),
                        pltpu.VMEM((2, bq, 2 * bk), BF16), pltpu.VMEM((2, bq, 2 * bk), F32),
                        pltpu.VMEM((bq, 2 * bk), BF16), pltpu.VMEM((bq, 2 * bk), F32),
                        pltpu.VMEM((bq, PAIR), F32), pltpu.VMEM((per_q + 1, bq, bk), F32),
                        pltpu.VMEM((2 * bk, 2 * bk), BF16)],
    )
    return pl.pallas_call(
        functools.partial(_sb_attn_kernel, bq=bq, bk=bk, n_items=n_items, tab_len=len(items)),
        grid_spec=grid_spec,
        out_shape=jax.ShapeDtypeStruct((b, tq, d), BF16),
        compiler_params=_params("parallel", "parallel"),
        name="sb_attention",
    )(table, q, k, v)


def _mm_res_ln_kernel(a_ref, w_ref, x_ref, g_ref, b_ref, o_ref):
    y = _dot(a_ref[...], w_ref[...])
    o_ref[...] = _layer_norm(DEEPNORM_ALPHA * x_ref[...] + y, g_ref[...], b_ref[...])


def _mm_res_ln(a, w_bf16, x, g, b, tm):
    t, d = x.shape
    kdim = a.shape[1]
    vec = pl.BlockSpec((1, d), lambda i: (0, 0))
    return pl.pallas_call(
        _mm_res_ln_kernel,
        grid=(t // tm,),
        in_specs=[pl.BlockSpec((tm, kdim), lambda i: (i, 0)),
                  pl.BlockSpec((kdim, d), lambda i: (0, 0)),
                  pl.BlockSpec((tm, d), lambda i: (i, 0)), vec, vec],
        out_specs=pl.BlockSpec((tm, d), lambda i: (i, 0)),
        out_shape=jax.ShapeDtypeStruct((t, d), F32),
        compiler_params=_params("parallel"),
        name="attn_out_ln",
    )(a, w_bf16, x, g.reshape(1, d), b.reshape(1, d))


def _swiglu_step(xb, wg_ref, wu_ref, wd_ref):
    gate = _dot(xb, wg_ref[...])
    up = _dot(xb, wu_ref[...])
    hidden = gate * jax.nn.sigmoid(gate) * up
    return _dot(hidden.astype(BF16), wd_ref[...])


def _ffn_ln_kernel(x_ref, wg_ref, wu_ref, wd_ref, g_ref, b_ref, o_ref, xb_scr, acc_scr):
    j = pl.program_id(1)

    @pl.when(j == 0)
    def _():
        xb_scr[...] = x_ref[...].astype(BF16)
        acc_scr[...] = jnp.zeros_like(acc_scr)

    acc_scr[...] += _swiglu_step(xb_scr[...], wg_ref, wu_ref, wd_ref)

    @pl.when(j == pl.num_programs(1) - 1)
    def _():
        o_ref[...] = _layer_norm(DEEPNORM_ALPHA * x_ref[...] + acc_scr[...], g_ref[...], b_ref[...])


def _ffn_ln(x, w_gate_up_bf16, w_down_bf16, g, b, tm, tn):
    t, d = x.shape
    f = w_down_bf16.shape[0]
    nj = f // tn
    vec = pl.BlockSpec((1, d), lambda i, j: (0, 0))
    return pl.pallas_call(
        _ffn_ln_kernel,
        grid=(t // tm, nj),
        in_specs=[pl.BlockSpec((tm, d), lambda i, j: (i, 0)),
                  pl.BlockSpec((d, tn), lambda i, j: (0, j)),
                  pl.BlockSpec((d, tn), lambda i, j: (0, nj + j)),
                  pl.BlockSpec((tn, d), lambda i, j: (j, 0)), vec, vec],
        out_specs=pl.BlockSpec((tm, d), lambda i, j: (i, 0)),
        out_shape=jax.ShapeDtypeStruct((t, d), F32),
        scratch_shapes=[pltpu.VMEM((tm, d), BF16), pltpu.VMEM((tm, d), F32)],
        compiler_params=_params("parallel", "arbitrary"),
        name="dense_swiglu_ln",
    )(x, w_gate_up_bf16, w_gate_up_bf16, w_down_bf16, g.reshape(1, d), b.reshape(1, d))


def _moe_ffn_kernel(te_ref, nt_ref, x_ref, wg_ref, wu_ref, wd_ref, o_ref, xb_scr, acc_scr):
    i = pl.program_id(0)
    j = pl.program_id(1)
    used = i < nt_ref[0]

    @pl.when(used & (j == 0))
    def _():
        xb_scr[...] = x_ref[...].astype(BF16)
        acc_scr[...] = jnp.zeros_like(acc_scr)

    @pl.when(used)
    def _():
        acc_scr[...] += _swiglu_step(xb_scr[...], wg_ref.at[0], wu_ref.at[0], wd_ref.at[0])

    @pl.when(j == pl.num_programs(1) - 1)
    def _():
        o_ref[...] = jnp.where(used, acc_scr[...], 0.0)


def _moe_ffn(tile_expert, n_tiles_used, xg, w_gate_up_bf16, w_down_bf16, tm, tn):
    n, d = xg.shape
    f = w_down_bf16.shape[1]
    nj = f // tn
    grid_spec = pltpu.PrefetchScalarGridSpec(
        num_scalar_prefetch=2,
        grid=(n // tm, nj),
        in_specs=[pl.BlockSpec((tm, d), lambda i, j, te, nt: (i, 0)),
                  pl.BlockSpec((1, d, tn), lambda i, j, te, nt: (te[i], 0, j)),
                  pl.BlockSpec((1, d, tn), lambda i, j, te, nt: (te[i], 0, nj + j)),
                  pl.BlockSpec((1, tn, d), lambda i, j, te, nt: (te[i], j, 0))],
        out_specs=pl.BlockSpec((tm, d), lambda i, j, te, nt: (i, 0)),
        scratch_shapes=[pltpu.VMEM((tm, d), BF16), pltpu.VMEM((tm, d), F32)],
    )
    return pl.pallas_call(
        _moe_ffn_kernel,
        grid_spec=grid_spec,
        out_shape=jax.ShapeDtypeStruct((n, d), F32),
        compiler_params=_params("parallel", "arbitrary"),
        name="expert_swiglu",
    )(tile_expert, n_tiles_used, xg, w_gate_up_bf16, w_gate_up_bf16, w_down_bf16)


def _row_copy(src_hbm, dst_ref, sem, src_row, dst_row):
    return pltpu.make_async_copy(src_hbm.at[pl.ds(src_row, 1)], dst_ref.at[pl.ds(dst_row, 1)], sem)


def _gather_rows_into(idx_ref, src_hbm, dst_ref, sem, n):
    def start(r, c):
        _row_copy(src_hbm, dst_ref, sem, idx_ref[0, 0, r], r).start()
        return c

    lax.fori_loop(0, n, start, 0, unroll=DMA_ISSUE_UNROLL)
    pltpu.make_async_copy(src_hbm.at[pl.ds(0, n)], dst_ref, sem).wait()


def _scatter_copy(x_ref, dst_hbm, sem, src_row, dst_row):
    return pltpu.make_async_copy(x_ref.at[pl.ds(src_row, 1)], dst_hbm.at[pl.ds(dst_row, 1)], sem)


def _scatter_kernel(p0_ref, p1_ref, x_ref, dst_in_hbm_unused, dst_hbm, sem, *, tm):
    def start(r, c):
        _scatter_copy(x_ref, dst_hbm, sem, r, p0_ref[0, 0, r]).start()
        _scatter_copy(x_ref, dst_hbm, sem, r, p1_ref[0, 0, r]).start()
        return c

    lax.fori_loop(0, tm, start, 0, unroll=DMA_ISSUE_UNROLL)
    for _ in range(2):
        pltpu.make_async_copy(x_ref, dst_hbm.at[pl.ds(0, tm)], sem).wait()


def _scatter_rows(x, pos0, pos1, dst, tm):
    t, d = x.shape
    idx = pl.BlockSpec((1, 1, tm), lambda i: (i, 0, 0), memory_space=pltpu.SMEM)
    return pl.pallas_call(
        functools.partial(_scatter_kernel, tm=tm),
        grid=(t // tm,),
        in_specs=[idx, idx, pl.BlockSpec((tm, d), lambda i: (i, 0)), pl.BlockSpec(memory_space=pl.ANY)],
        out_specs=pl.BlockSpec(memory_space=pl.ANY),
        out_shape=jax.ShapeDtypeStruct(dst.shape, dst.dtype),
        input_output_aliases={3: 0},
        scratch_shapes=[pltpu.SemaphoreType.DMA(())],
        compiler_params=_params("arbitrary"),
        name="scatter_rows",
    )(pos0.reshape(-1, 1, tm), pos1.reshape(-1, 1, tm), x, dst)


def _combine_ln_kernel(p0_ref, p1_ref, x_ref, route_ref, y_hbm, g_ref, b_ref, op_ref, os_ref, a_scr, b_scr,
                       sem0, sem1, *, tm, n_prompt_tiles):
    _gather_rows_into(p0_ref, y_hbm, a_scr, sem0, tm)
    _gather_rows_into(p1_ref, y_hbm, b_scr, sem1, tm)
    moe = route_ref[:, 2:3] * a_scr[...] + route_ref[:, 3:4] * b_scr[...]
    out = _layer_norm(DEEPNORM_ALPHA * x_ref[...] + moe, g_ref[...], b_ref[...])
    i = pl.program_id(0)

    @pl.when(i < n_prompt_tiles)
    def _():
        op_ref[...] = out

    @pl.when(i >= n_prompt_tiles)
    def _():
        os_ref[...] = out


def _combine_ln(x, route, y_sorted, pos0, pos1, g, b, tm, n_prompt):
    t, d = x.shape
    npt = n_prompt // tm
    nst = (t - n_prompt) // tm
    idx = pl.BlockSpec((1, 1, tm), lambda i: (i, 0, 0), memory_space=pltpu.SMEM)
    vec = pl.BlockSpec((1, d), lambda i: (0, 0))
    return pl.pallas_call(
        functools.partial(_combine_ln_kernel, tm=tm, n_prompt_tiles=npt),
        grid=(npt + nst,),
        in_specs=[idx, idx, pl.BlockSpec((tm, d), lambda i: (i, 0)), pl.BlockSpec((tm, LANES), lambda i: (i, 0)),
                  pl.BlockSpec(memory_space=pl.ANY), vec, vec],
        out_specs=[pl.BlockSpec((tm, d), lambda i: (jnp.minimum(i, npt - 1), 0)),
                   pl.BlockSpec((tm, d), lambda i: (jnp.maximum(i - npt, 0), 0))],
        out_shape=[jax.ShapeDtypeStruct((n_prompt, d), F32), jax.ShapeDtypeStruct((t - n_prompt, d), F32)],
        scratch_shapes=[pltpu.VMEM((tm, d), F32), pltpu.VMEM((tm, d), F32),
                        pltpu.SemaphoreType.DMA(()), pltpu.SemaphoreType.DMA(())],
        compiler_params=_params("arbitrary"),
        name="moe_combine_ln",
    )(pos0.reshape(-1, 1, tm), pos1.reshape(-1, 1, tm), x, route, y_sorted, g.reshape(1, d), b.reshape(1, d))


def _rwkv_proj_kernel(x_ref, prev_ref, mu_ref, wr_ref, wk_ref, wv_ref, w1_ref, w2_ref, a1_ref, a2_ref,
                      g1_ref, g2_ref, vec_ref,
                      r_ref, lw_ref, k_ref, v_ref, kk_ref, b_ref, g_ref, bonus_ref, *, seg):
    x = x_ref[...]
    tm, d = x.shape
    rolled = pltpu.roll(x, 1, 0).reshape(tm // seg, seg, d)
    first = lax.broadcasted_iota(jnp.int32, (tm // seg, seg, d), 1) == 0
    x_prev = jnp.where(first, prev_ref[...], rolled).reshape(tm, d)
    xx = x_prev - x

    def mixed(i):
        return (x + xx * mu_ref[i:i + 1, :]).astype(BF16)

    r = _dot(mixed(0), wr_ref[...])
    k = _dot(mixed(2), wk_ref[...])
    v = _dot(mixed(3), wv_ref[...])
    w0, a0, k_k, k_a, r_k = (vec_ref[i:i + 1, :] for i in range(5))
    wl = w0 + _dot(jnp.tanh(_dot(mixed(1), w1_ref[...])).astype(BF16), w2_ref[...])
    softplus_neg = jnp.maximum(-wl, 0.0) + jnp.log(1.0 + jnp.exp(-jnp.abs(wl)))
    lw_ref[...] = -jnp.exp(-softplus_neg - 0.5)
    a = jax.nn.sigmoid(a0 + _dot(_dot(mixed(4), a1_ref[...]).astype(BF16), a2_ref[...]))
    g_ref[...] = _dot(jax.nn.sigmoid(_dot(mixed(5), g1_ref[...])).astype(BF16), g2_ref[...])
    ones_pair = _head_ones(PAIR)
    kk = k * k_k
    kk = kk / jnp.maximum(jnp.sqrt(_head_sum(kk * kk, ones_pair)), 1e-12)
    k = k * (1.0 + (a - 1.0) * k_a)
    r_ref[...] = r
    k_ref[...] = k
    v_ref[...] = v
    kk_ref[...] = kk
    b_ref[...] = kk * a
    bonus_ref[...] = _head_sum(r * k * r_k, ones_pair) * v


def _rwkv_proj(x, prev_rows, mu, w_r, w_k, w_v, w1, w2, a1, a2, g1, g2, vecs, tm, seg):
    t, d = x.shape
    row = pl.BlockSpec((tm, d), lambda i: (i, 0))

    def full(a):
        return pl.BlockSpec(a.shape, lambda i: (0,) * a.ndim)

    consts = (mu, w_r, w_k, w_v, w1, w2, a1, a2, g1, g2, vecs)
    return pl.pallas_call(
        functools.partial(_rwkv_proj_kernel, seg=seg),
        grid=(t // tm,),
        in_specs=[row, pl.BlockSpec((tm // seg, 1, d), lambda i: (i, 0, 0))] + [full(c) for c in consts],
        out_specs=[row] * 8,
        out_shape=[jax.ShapeDtypeStruct((t, d), F32)] * 8,
        compiler_params=_params("parallel"),
        name="rwkv_proj",
    )(x, prev_rows, *consts)


def _unit_lower_inverses(lows, limit):
    n = lows[0].shape[0]
    r = lax.broadcasted_iota(jnp.int32, (n, n), 0)
    c = lax.broadcasted_iota(jnp.int32, (n, n), 1)
    eye = (r == c).astype(F32)
    base = (r >> 3) == (c >> 3)
    ps = [jnp.where(base, -low, 0.0) for low in lows]
    invs = [eye + p for p in ps]
    for _ in range(2):
        ps = [_dot2(p, p) for p in ps]
        invs = [inv + _dot2(inv, p) for inv, p in zip(invs, ps)]
    shift = 3
    while (1 << shift) < limit:
        sel = ((r >> (shift + 1)) == (c >> (shift + 1))) & ((r >> shift) != (c >> shift))
        mids = [_dot2(inv, jnp.where(sel, low, 0.0)) for inv, low in zip(invs, lows)]
        invs = [inv - _dot2(mid, inv) for inv, mid in zip(invs, mids)]
        shift += 1
    return invs


def _rwkv_scan_kernel(r_ref, lw_ref, k_ref, v_ref, kk_ref, b_ref, h0_ref, o_ref, hT_ref, h_scr, *, n_chunks, n_pairs):
    cs = SCAN_CHUNK
    tb = pl.program_id(2)

    @pl.when(tb == 0)
    def _():
        h_scr[...] = h0_ref[0]

    lane = lax.broadcasted_iota(jnp.int32, (cs, PAIR), 1)
    row = lax.broadcasted_iota(jnp.int32, (cs, PAIR), 0)
    head_a = lane < HEAD_DIM
    strict = (lane & (HEAD_DIM - 1)) < row
    incl = (lane & (HEAD_DIM - 1)) <= row
    tri = (lax.broadcasted_iota(jnp.int32, (cs, cs), 1) <= lax.broadcasted_iota(jnp.int32, (cs, cs), 0)).astype(BF16)
    pr = lax.broadcasted_iota(jnp.int32, (PAIR, PAIR), 0) >> 6
    pc = lax.broadcasted_iota(jnp.int32, (PAIR, PAIR), 1) >> 6
    same_head = pr == pc

    def stack(x):
        return jnp.concatenate([jnp.where(head_a, x, 0.0), jnp.where(head_a, 0.0, x)], axis=0)

    def each(fn, *lists):
        return [fn(*xs) for xs in zip(*lists)]

    def cumulative(lw):
        l1 = lw.astype(BF16)
        rem = lw - l1.astype(F32)
        l2 = rem.astype(BF16)
        l3 = (rem - l2.astype(F32)).astype(BF16)
        parts = _dot(tri, jnp.concatenate([l1, l2, l3], axis=1))
        return parts[:, :PAIR] + parts[:, PAIR:2 * PAIR] + parts[:, 2 * PAIR:]

    def chunk(ci, carry):
        sl = pl.ds(pl.multiple_of(ci * cs, cs), cs)
        lanes = [slice(p * PAIR, (p + 1) * PAIR) for p in range(n_pairs)]
        r, lw, k, v, kk, b = ([ref[0, sl, ln] for ln in lanes] for ref in (r_ref, lw_ref, k_ref, v_ref, kk_ref, b_ref))
        h = [h_scr[p] for p in range(n_pairs)]
        cum = each(cumulative, lw)
        total = [c[cs - 1:cs, :] for c in cum]
        r_hat = each(lambda x, c: x * jnp.exp(c), r, cum)
        kk_hat = each(lambda x, c, l: x * jnp.exp(c - l), kk, cum, lw)
        k_hat = each(lambda x, c: x * jnp.exp(-c), k, cum)
        b_hat = each(lambda x, c: x * jnp.exp(-c), b, cum)
        gram = each(lambda kh, rh, bh, k2: _dot2_nt(jnp.concatenate([kh, rh], axis=0),
                                                    jnp.concatenate([stack(bh), stack(k2)], axis=0)),
                    kk_hat, r_hat, b_hat, k_hat)
        l_b = [jnp.where(strict, g[:cs, :PAIR], 0.0) for g in gram]
        l_k = [jnp.where(strict, g[:cs, PAIR:], 0.0) for g in gram]
        a_rb = [jnp.where(incl, g[cs:, :PAIR], 0.0) for g in gram]
        a_rk = [jnp.where(incl, g[cs:, PAIR:], 0.0) for g in gram]
        v_st = each(stack, v)
        w = each(lambda kh, lk, hh, vs: _dot2(jnp.concatenate([kh, lk], axis=1), jnp.concatenate([hh, vs], axis=0)),
                 kk_hat, l_k, h, v_st)
        inv = _unit_lower_inverses(each(stack, l_b), cs)
        u_st = each(lambda t_, w_: _dot2(t_, stack(w_)), inv, w)
        o = each(lambda rh, ak, ab, hh, vs, us: _dot2(jnp.concatenate([rh, ak, -ab], axis=1),
                                                      jnp.concatenate([hh, vs, us], axis=0)),
                 r_hat, a_rk, a_rb, h, v_st, u_st)
        for ln, o_p in zip(lanes, o):
            o_ref[0, sl, ln] = o_p
        u = [us[:cs] + us[cs:] for us in u_st]
        to_end = each(lambda t_, c: jnp.exp(t_ - c), total, cum)
        upd = each(lambda k_, b_, e_, v_, u_: _dot2_tn(jnp.concatenate([k_ * e_, -(b_ * e_)], axis=0),
                                                       jnp.concatenate([v_, u_], axis=0)),
                   k, b, to_end, v, u)
        for p in range(n_pairs):
            decay_col = jnp.transpose(jnp.broadcast_to(jnp.exp(total[p]), (PAIR, PAIR)))
            h_scr[p] = decay_col * h[p] + jnp.where(same_head, upd[p], 0.0)
        return carry

    lax.fori_loop(0, n_chunks, chunk, 0)

    @pl.when(tb == pl.num_programs(2) - 1)
    def _():
        hT_ref[0] = h_scr[...]


def _rwkv_scan(r, lw, k, v, kk, b, h0, tb):
    bsz, t, d = r.shape
    npair = d // PAIR
    group = math.gcd(npair, SCAN_PAIRS_PER_STEP)
    seq = pl.BlockSpec((1, tb, group * PAIR), lambda bi, pi, ti: (bi, ti, pi))
    st = pl.BlockSpec((1, group, PAIR, PAIR), lambda bi, pi, ti: (bi, pi, 0, 0))
    return pl.pallas_call(
        functools.partial(_rwkv_scan_kernel, n_chunks=tb // SCAN_CHUNK, n_pairs=group),
        grid=(bsz, npair // group, t // tb),
        in_specs=[seq] * 6 + [st],
        out_specs=[seq, st],
        out_shape=[jax.ShapeDtypeStruct((bsz, t, d), F32), jax.ShapeDtypeStruct((bsz, npair, PAIR, PAIR), F32)],
        scratch_shapes=[pltpu.VMEM((group, PAIR, PAIR), F32)],
        compiler_params=_params("parallel", "parallel", "arbitrary"),
        name="rwkv_scan",
    )(r, lw, k, v, kk, b, h0)


def _rwkv_out_kernel(o_ref, g_ref, bonus_ref, x_ref, wo_ref, vec_ref, dst_hbm_unused, out_ref):
    o = o_ref[...]
    ones_pair = _head_ones(PAIR)
    mean = _head_sum(o, ones_pair) * (1.0 / HEAD_DIM)
    c = o - mean
    var = _head_sum(c * c, ones_pair) * (1.0 / HEAD_DIM)
    on = c * lax.rsqrt(var + GN_EPS) * vec_ref[0:1, :] + vec_ref[1:2, :]
    y = _dot(((on + bonus_ref[...]) * g_ref[...]).astype(BF16), wo_ref[...])
    out_ref[...] = _layer_norm(DEEPNORM_ALPHA * x_ref[...] + y, vec_ref[2:3, :], vec_ref[3:4, :])


def _rwkv_out(o, g, bonus, x, w_o_bf16, vecs, dst, n_rows, row_offset, tm):
    t, d = x.shape
    row = pl.BlockSpec((tm, d), lambda i: (i, 0))
    off = row_offset // tm
    return pl.pallas_call(
        _rwkv_out_kernel,
        grid=(t // tm,),
        in_specs=[row, row, row, row, pl.BlockSpec((d, d), lambda i: (0, 0)),
                  pl.BlockSpec(vecs.shape, lambda i: (0, 0)), pl.BlockSpec(memory_space=pl.ANY)],
        out_specs=pl.BlockSpec((tm, d), lambda i: (off + i, 0)),
        out_shape=jax.ShapeDtypeStruct((n_rows, d), F32),
        input_output_aliases={6: 0},
        compiler_params=_params("parallel"),
        name="rwkv_out_ln",
    )(o, g, bonus, x, w_o_bf16, vecs, dst)


def _router_kernel(x_ref, w_ref, o_ref, *, n_experts):
    logits = _dot3(x_ref[...], w_ref[...])
    lane = lax.broadcasted_iota(jnp.int32, logits.shape, 1)
    lane_f = lane.astype(F32)
    lowest = jnp.float32(-3.0e38)
    logits = jnp.where(lane < n_experts, logits, lowest)
    m1 = jnp.max(logits, axis=1, keepdims=True)
    i1 = jnp.min(jnp.where(logits == m1, lane_f, float(LANES)), axis=1, keepdims=True)
    rest = jnp.where(lane_f == i1, lowest, logits)
    m2 = jnp.max(rest, axis=1, keepdims=True)
    i2 = jnp.min(jnp.where(rest == m2, lane_f, float(LANES)), axis=1, keepdims=True)
    e = jnp.exp(m2 - m1)
    g1 = 1.0 / (1.0 + e)
    g2 = e * g1
    out = jnp.where(lane == 0, i1, 0.0)
    out = jnp.where(lane == 1, i2, out)
    out = jnp.where(lane == 2, g1, out)
    o_ref[...] = jnp.where(lane == 3, g2, out)


def _router(x, w_router, tm):
    t, d = x.shape
    e = w_router.shape[1]
    w_pad = jnp.zeros((d, LANES), F32).at[:, :e].set(w_router)
    return pl.pallas_call(
        functools.partial(_router_kernel, n_experts=e),
        grid=(t // tm,),
        in_specs=[pl.BlockSpec((tm, d), lambda i: (i, 0)), pl.BlockSpec((d, LANES), lambda i: (0, 0))],
        out_specs=pl.BlockSpec((tm, LANES), lambda i: (i, 0)),
        out_shape=jax.ShapeDtypeStruct((t, LANES), F32),
        compiler_params=_params("parallel"),
        name="router_top2",
    )(x, w_pad)


def _ffn_chunk(f, cap=1536):
    best = LANES
    for tn in range(LANES, cap + 1, LANES):
        if f % tn == 0:
            best = tn
    return best


def _row_tile(t, cap):
    tm = cap
    while t % tm:
        tm //= 2
    return tm


def _pair_states(s):
    b, h, n, _ = s.shape
    st = jnp.swapaxes(s, 2, 3).reshape(b, h // 2, 2, n, n)
    z = jnp.zeros_like(st[:, :, 0])
    top = jnp.concatenate([st[:, :, 0], z], axis=-1)
    bot = jnp.concatenate([z, st[:, :, 1]], axis=-1)
    return jnp.concatenate([top, bot], axis=-2)


def _unpair_states(hp):
    b, p, _, _ = hp.shape
    n = HEAD_DIM
    st = jnp.stack([hp[:, :, :n, :n], hp[:, :, n:, n:]], axis=2).reshape(b, 2 * p, n, n)
    return jnp.swapaxes(st, 2, 3)


def _attention_layer(x, w_qkv, w_o, ln_g, ln_b, cache_k=None, cache_v=None):
    b, t, d = x.shape
    q, kb, vb, kf, vf = _qkv_proj(x, w_qkv, _row_tile(t, 512))
    if cache_k is None:
        ao = _sb_attention(q, kb, vb, _row_tile(t, 512), LANES, 0)
    else:
        past = cache_k.shape[2]
        merge = lambda c: jnp.swapaxes(c.astype(BF16), 1, 2).reshape(b, past, d)
        keys = jnp.concatenate([merge(cache_k), kb], axis=1)
        vals = jnp.concatenate([merge(cache_v), vb], axis=1)
        ao = _sb_attention(q, keys, vals, t, t, past)
    x1 = _mm_res_ln(ao.reshape(b * t, d), w_o, x.reshape(b * t, d), ln_g, ln_b, _row_tile(b * t, 512))
    return x1, kf, vf


def _rwkv_layer(x, x_last, s0, p, dst, n_rows, row_offset):
    b, t, d = x.shape
    seg = SCAN_CHUNK
    xs = x.reshape(b, t // seg, seg, d)
    prev = jnp.concatenate([x_last[:, None, :], xs[:, :-1, seg - 1, :]], axis=1).reshape(b * t // seg, 1, d)
    xf = x.reshape(b * t, d)
    tm = _row_tile(b * t, 256)
    r, lw, k, v, kk, bb, g, bonus = _rwkv_proj(xf, prev, p["mu"], p["w_r"], p["w_k"], p["w_v"], p["w1"], p["w2"],
                                               p["a1"], p["a2"], p["g1"], p["g2"], p["proj_vecs"], tm, seg)
    shp = (b, t, d)
    o, h_t = _rwkv_scan(*(z.reshape(shp) for z in (r, lw, k, v, kk, bb)), _pair_states(s0), _row_tile(t, 512))
    dst = _rwkv_out(o.reshape(b * t, d), g, bonus, xf, p["w_o"], p["out_vecs"], dst, n_rows, row_offset, tm)
    return dst, _unpair_states(h_t)


def _moe_layer(x, n_prompt, w_router, w_gate_up, w_down, ln_g, ln_b, tm):
    t, d = x.shape
    e = w_router.shape[1]
    route = _router(x, w_router, tm)
    experts = jnp.concatenate([route[:, 0], route[:, 1]]).astype(jnp.int32)
    onehot = (experts[:, None] == jnp.arange(e, dtype=jnp.int32)[None, :]).astype(jnp.int32)
    counts = jnp.sum(onehot, axis=0)
    padded = ((counts + tm - 1) // tm) * tm
    ends = jnp.cumsum(padded)
    starts = ends - padded
    rank = jnp.sum(jnp.cumsum(onehot, axis=0) * onehot, axis=1) - 1
    pos = (jnp.sum(starts[None, :] * onehot, axis=1) + rank).astype(jnp.int32)
    n_rows = ((2 * t + e * (tm - 1)) // tm) * tm
    tile_start = jnp.arange(n_rows // tm, dtype=jnp.int32) * tm
    tile_expert = jnp.minimum(jnp.sum((tile_start[:, None] >= ends[None, :]).astype(jnp.int32), axis=1), e - 1)
    n_tiles_used = (ends[-1] // tm).astype(jnp.int32).reshape(1)
    xg = _scatter_rows(x, pos[:t], pos[t:], jnp.zeros((n_rows, d), F32), tm)
    y = _moe_ffn(tile_expert, n_tiles_used, xg, w_gate_up, w_down, tm, _ffn_chunk(w_down.shape[1]))
    return _combine_ln(x, route, y, pos[:t], pos[t:], ln_g, ln_b, tm, n_prompt)


def kernel(x_prompt, x_sample, cache_k, cache_v, state_wkv, state_shift, att_w_qkv, att_w_o, ffn_w_gate_up, ffn_w_down, rwkv_mu, rwkv_w_rkv, rwkv_w0, rwkv_w1, rwkv_w2, rwkv_a0, rwkv_a1, rwkv_a2, rwkv_g1, rwkv_g2, rwkv_k_k, rwkv_k_a, rwkv_r_k, rwkv_gn_g, rwkv_gn_b, rwkv_w_o, moe_w_router, moe_w_gate_up, moe_w_down, ln_mix_g, ln_mix_b, ln_ffn_g, ln_ffn_b):
    bp, tp, d = x_prompt.shape
    bs, ts, _ = x_sample.shape
    n_prompt, n_sample = bp * tp, bs * ts
    bf = lambda a: a.astype(BF16)

    w_qkv, w_o = bf(att_w_qkv[0]), bf(att_w_o[0])
    xp1, k_p, v_p = _attention_layer(x_prompt, w_qkv, w_o, ln_mix_g[0], ln_mix_b[0])
    xs1, k_s, v_s = _attention_layer(x_sample, w_qkv, w_o, ln_mix_g[0], ln_mix_b[0], cache_k[0], cache_v[0])
    w_gu, w_dn = bf(ffn_w_gate_up[0]), bf(ffn_w_down[0])
    tn = _ffn_chunk(w_dn.shape[0])
    xp2 = _ffn_ln(xp1, w_gu, w_dn, ln_ffn_g[0], ln_ffn_b[0], _row_tile(n_prompt, 512), tn).reshape(bp, tp, d)
    xs2 = _ffn_ln(xs1, w_gu, w_dn, ln_ffn_g[0], ln_ffn_b[0], _row_tile(n_sample, 512), tn).reshape(bs, ts, d)

    p = {
        "mu": rwkv_mu[0], "w_r": bf(rwkv_w_rkv[0, 0]), "w_k": bf(rwkv_w_rkv[0, 1]), "w_v": bf(rwkv_w_rkv[0, 2]),
        "w1": bf(rwkv_w1[0]), "w2": bf(rwkv_w2[0]), "a1": bf(rwkv_a1[0]), "a2": bf(rwkv_a2[0]),
        "g1": bf(rwkv_g1[0]), "g2": bf(rwkv_g2[0]), "w_o": bf(rwkv_w_o[0]),
        "proj_vecs": jnp.stack([rwkv_w0[0], rwkv_a0[0], rwkv_k_k[0], rwkv_k_a[0], rwkv_r_k[0].reshape(d)]),
        "out_vecs": jnp.stack([rwkv_gn_g[0], rwkv_gn_b[0], ln_mix_g[1], ln_mix_b[1]]),
    }
    n_all = n_prompt + n_sample
    x3, wkv_p = _rwkv_layer(xp2, jnp.zeros((bp, d), F32), jnp.zeros((bp, d // HEAD_DIM, HEAD_DIM, HEAD_DIM), F32),
                            p, jnp.zeros((n_all, d), F32), n_all, 0)
    x3, wkv_s = _rwkv_layer(xs2, state_shift[0], state_wkv[0], p, x3, n_all, n_prompt)
    y_p, y_s = _moe_layer(x3, n_prompt, moe_w_router[0], bf(moe_w_gate_up[0]), bf(moe_w_down[0]),
                          ln_ffn_g[1], ln_ffn_b[1], _row_tile(n_sample, 512))
    return (y_p.reshape(bp, tp, d), y_s.reshape(bs, ts, d),
            k_p[None], v_p[None], wkv_p[None], xp2[:, -1][None],
            k_s[None], v_s[None], wkv_s[None], xs2[:, -1][None])
```

```python
import functools
import math

import jax
import jax.numpy as jnp
import numpy as np
from jax import lax
from jax.experimental import pallas as pl
from jax.experimental.pallas import tpu as pltpu

F32 = jnp.float32
BF16 = jnp.bfloat16

HEAD_DIM = 64
LANES = 128
PAIR = 2 * HEAD_DIM
LN_EPS = 1e-5
GN_EPS = 64e-5
DEEPNORM_ALPHA = 4.0 ** 0.25
Q_SCALE = HEAD_DIM ** -0.5 * 1.4426950408889634
SCAN_PAIRS_PER_STEP = 8
DMA_ISSUE_UNROLL = 8
SCAN_CHUNK = 64
V7X_VMEM_LIMIT = 56 * 1024 * 1024


def _params(*sem, vmem=V7X_VMEM_LIMIT):
    return pltpu.CompilerParams(dimension_semantics=sem, vmem_limit_bytes=vmem)


def _layer_norm(y, g, b):
    mu = jnp.mean(y, axis=-1, keepdims=True)
    c = y - mu
    var = jnp.mean(c * c, axis=-1, keepdims=True)
    return c * lax.rsqrt(var + LN_EPS) * g + b


def _split2(x):
    hi = x.astype(BF16)
    lo = (x - hi.astype(F32)).astype(BF16)
    return hi, lo


def _dot(a, b):
    return jnp.dot(a, b, preferred_element_type=F32)


def _dot_nt(a, b):
    return lax.dot_general(a, b, (((1,), (1,)), ((), ())), preferred_element_type=F32)


def _dot_tn(a, b):
    return lax.dot_general(a, b, (((0,), (0,)), ((), ())), preferred_element_type=F32)


def _dot3(a, b):
    ah, al = _split2(a)
    bh, bl = _split2(b)
    return _dot(jnp.concatenate([ah, ah, al], axis=1), jnp.concatenate([bh, bl, bh], axis=0))


def _dot2(a, b):
    ah, al = _split2(a)
    bb = b.astype(BF16)
    return _dot(jnp.concatenate([ah, al], axis=1), jnp.concatenate([bb, bb], axis=0))


def _dot2_nt(a, b):
    ah, al = _split2(a)
    bb = b.astype(BF16)
    return _dot_nt(jnp.concatenate([ah, al], axis=1), jnp.concatenate([bb, bb], axis=1))


def _dot2_tn(a, b):
    ah, al = _split2(a)
    bb = b.astype(BF16)
    return _dot_tn(jnp.concatenate([ah, al], axis=0), jnp.concatenate([bb, bb], axis=0))


def _dot2_exact_rhs(a, b_bf16):
    ah, al = _split2(a)
    return _dot(jnp.concatenate([ah, al], axis=1), jnp.concatenate([b_bf16, b_bf16], axis=0))


def _head_ones(n):
    r = lax.broadcasted_iota(jnp.int32, (n, n), 0) >> 6
    c = lax.broadcasted_iota(jnp.int32, (n, n), 1) >> 6
    return (r == c).astype(BF16)


def _head_sum(x, ones_pair):
    d = x.shape[1]
    cols = [_dot2_exact_rhs(x[:, c:c + PAIR], ones_pair) for c in range(0, d, PAIR)]
    return jnp.concatenate(cols, axis=1)


def _qkv_kernel(x_ref, wq_ref, wk_ref, wv_ref, qb_ref, kb_ref, vb_ref, kf_ref, vf_ref, xb_scr, *, hpb):
    @pl.when(pl.program_id(2) == 0)
    def _():
        xb_scr[...] = x_ref[0].astype(BF16)

    xb = xb_scr[...]
    q = _dot(xb, wq_ref[...]) * Q_SCALE
    k = _dot(xb, wk_ref[...])
    v = _dot(xb, wv_ref[...])
    qb_ref[0] = q.astype(BF16)
    kb_ref[0] = k.astype(BF16)
    vb_ref[0] = v.astype(BF16)
    for h in range(hpb):
        sl = slice(h * HEAD_DIM, (h + 1) * HEAD_DIM)
        kf_ref[0, h] = k[:, sl]
        vf_ref[0, h] = v[:, sl]


def _qkv_proj(x, w_bf16, tm):
    b, t, d = x.shape
    h = d // HEAD_DIM
    hpb = 4
    tn = hpb * HEAD_DIM
    nb = d // tn
    grid = (b, t // tm, nb)
    hspec = pl.BlockSpec((1, hpb, tm, HEAD_DIM), lambda bi, ti, ni: (bi, ni, ti, 0))
    shp = (b, h, t, HEAD_DIM)
    return pl.pallas_call(
        functools.partial(_qkv_kernel, hpb=hpb),
        grid=grid,
        in_specs=[
            pl.BlockSpec((1, tm, d), lambda bi, ti, ni: (bi, ti, 0)),
            pl.BlockSpec((d, tn), lambda bi, ti, ni: (0, ni)),
            pl.BlockSpec((d, tn), lambda bi, ti, ni: (0, nb + ni)),
            pl.BlockSpec((d, tn), lambda bi, ti, ni: (0, 2 * nb + ni)),
        ],
        out_specs=[pl.BlockSpec((1, tm, tn), lambda bi, ti, ni: (bi, ti, ni))] * 3 + [hspec] * 2,
        out_shape=[jax.ShapeDtypeStruct((b, t, d), BF16)] * 3 + [jax.ShapeDtypeStruct(shp, F32)] * 2,
        scratch_shapes=[pltpu.VMEM((tm, d), BF16)],
        compiler_params=_params("parallel", "parallel", "arbitrary"),
        name="qkv_proj",
    )(x, w_bf16, w_bf16, w_bf16)


def _aligned(x, m):
    return x if isinstance(x, int) else pl.multiple_of(x, m)


ATTN_STAGES = 5
MASKED_SCORE = -1.0e30


DEAD_LOG2 = -160.0
F_Q0, F_K0, F_BIAS, F_FIRST, F_REAL, F_QB, F_NEXT = range(7)


def _attn_schedule(nq, per_q, past_blocks, bq, bk):
    items = [[0, 0, 0, 1, 0, -1, 0]]
    starts = []
    for qb in range(nq):
        n_full = past_blocks + qb * per_q
        blocks = [(n_full + j, j + 1) for j in reversed(range(per_q))] + [(kb, 0) for kb in reversed(range(n_full))]
        starts.append(len(items))
        for i, (kb, bias_id) in enumerate(blocks):
            items.append([qb * bq, kb * bk, bias_id, int(i == 0), 1, qb, 0])
    starts.append(len(items))
    for it in items[1:]:
        it[F_NEXT] = starts[it[F_QB] + 1]
    return items


def _sb_attn_kernel(tab_ref, q_ref, k_ref, v_ref, o_ref, z_scr, lb_scr, x2_scr, sums_scr, a_scr, later_scr,
                    acc_scr, bias_scr, suffix_scr, dead_ref, *, bq, bk, tab_len):
    per_q = bq // bk
    dead_ref[0] = -1
    r = lax.broadcasted_iota(jnp.int32, (2 * bk, 2 * bk), 0)
    c = lax.broadcasted_iota(jnp.int32, (2 * bk, 2 * bk), 1)
    r = jnp.where(r >= bk, r - bk, r)
    suffix_scr[...] = ((c >= bk) | (r > c)).astype(BF16)
    qi = lax.broadcasted_iota(jnp.int32, (bq, bk), 0)
    kj = lax.broadcasted_iota(jnp.int32, (bq, bk), 1)
    bias_scr[0] = jnp.zeros((bq, bk), F32)
    for j in range(per_q):
        bias_scr[j + 1] = jnp.where((kj + j * bk) < qi, 0.0, MASKED_SCORE)
    for scr in (z_scr, lb_scr, x2_scr, sums_scr, a_scr, later_scr, acc_scr):
        scr[...] = jnp.zeros_like(scr)
    lane_a = lax.broadcasted_iota(jnp.int32, (bk, PAIR), 1) < HEAD_DIM

    def field(f, i):
        return tab_ref[f * tab_len + i]

    def stack_heads(blk):
        zero = jnp.zeros_like(blk)
        return jnp.concatenate([jnp.where(lane_a, blk, zero), jnp.where(lane_a, zero, blk)], axis=0)

    def keep_going(c):
        return c[8] < ATTN_STAGES

    def body(c):
        trip, nxt, i1, i2, i3, i4, q_prev, valid_prev, idle = c
        peek = jnp.minimum(nxt, tab_len - 1)
        nxt = jnp.where((nxt < tab_len) & (field(F_QB, peek) == dead_ref[0]), field(F_NEXT, peek), nxt)
        more = nxt < tab_len
        i0 = jnp.where(more, nxt, 0)
        nxt = jnp.where(more, nxt + 1, nxt)
        idle = jnp.where(more, 0, idle + 1)
        first5 = field(F_FIRST, i4) == 1

        @pl.when(first5 & (valid_prev == 1))
        def _():
            o_ref[0, pl.ds(pl.multiple_of(q_prev, bq), bq), :] = acc_scr[...].astype(BF16)

        k5 = pl.multiple_of(field(F_K0, i4), bk)
        contrib = _dot(a_scr[...], stack_heads(v_ref[0, pl.ds(k5, bk), :]))
        acc_scr[...] = jnp.where(first5, contrib, acc_scr[...] + contrib)
        slot = trip & 1
        restart = field(F_FIRST, i3) == 1
        lowest = None
        for h in range(2):
            cols = slice(h * bk, (h + 1) * bk)
            later = jnp.where(restart, 0.0, later_scr[:, cols])
            between = sums_scr[h, :, :bk] + later
            later = later + sums_scr[h, :, bk:]
            later_scr[:, cols] = later
            lowest = later if lowest is None else jnp.maximum(lowest, later)
            a_scr[:, cols] = jnp.exp2(lb_scr[slot, :, cols] + between).astype(BF16)
        dead_ref[0] = jnp.where(jnp.max(lowest) < DEAD_LOG2, field(F_QB, i3), -1)
        for h in range(2):
            sums_scr[h] = _dot(x2_scr[h], suffix_scr[...])
        bias = bias_scr[field(F_BIAS, i1)]
        for h in range(2):
            cols = slice(h * bk, (h + 1) * bk)
            z = z_scr[:, cols] + bias
            soft = jnp.log2(1.0 + jnp.exp2(-jnp.abs(z)))
            log_beta = jnp.minimum(z, 0.0) - soft
            hi, lo = _split2(log_beta - z)
            lb_scr[slot, :, cols] = log_beta
            x2_scr[h] = jnp.concatenate([hi, lo], axis=1)
        q1 = pl.multiple_of(field(F_Q0, i0), bq)
        k1 = pl.multiple_of(field(F_K0, i0), bk)
        z_scr[...] = _dot_nt(q_ref[0, pl.ds(q1, bq), :], stack_heads(k_ref[0, pl.ds(k1, bk), :]))
        return (trip + 1, nxt, i0, i1, i2, i3, field(F_Q0, i4), field(F_REAL, i4), idle)

    zero = jnp.int32(0)
    lax.while_loop(keep_going, body, (zero, jnp.int32(1), zero, zero, zero, zero, zero, zero, zero))


def _sb_attention(q, k, v, bq, bk, q_start):
    b, tq, d = q.shape
    tk = k.shape[1]
    assert tq % bq == 0 and bq % bk == 0 and q_start % bk == 0 and tk == q_start + tq
    per_q = bq // bk
    items = _attn_schedule(tq // bq, per_q, q_start // bk, bq, bk)
    table = jnp.asarray(np.asarray(items, np.int32).T.reshape(-1))
    qspec = pl.BlockSpec((1, tq, PAIR), lambda bi, pi, tab: (bi, 0, pi))
    kspec = pl.BlockSpec((1, tk, PAIR), lambda bi, pi, tab: (bi, 0, pi))
    grid_spec = pltpu.PrefetchScalarGridSpec(
        num_scalar_prefetch=1,
        grid=(b, d // PAIR),
        in_specs=[qspec, kspec, kspec],
        out_specs=qspec,
        scratch_shapes=[pltpu.VMEM((bq, 2 * bk), F32), pltpu.VMEM((2, bq, 2 * bk), F32),
                        pltpu.VMEM((2, bq, 2 * bk), BF16), pltpu.VMEM((2, bq, 2 * bk), F32),
                        pltpu.VMEM((bq, 2 * bk), BF16), pltpu.VMEM((bq, 2 * bk), F32),
                        pltpu.VMEM((bq, PAIR), F32), pltpu.VMEM((per_q + 1, bq, bk), F32),
                        pltpu.VMEM((2 * bk, 2 * bk), BF16), pltpu.SMEM((1,), jnp.int32)],
    )
    return pl.pallas_call(
        functools.partial(_sb_attn_kernel, bq=bq, bk=bk, tab_len=len(items)),
        grid_spec=grid_spec,
        out_shape=jax.ShapeDtypeStruct((b, tq, d), BF16),
        compiler_params=_params("parallel", "parallel"),
        name="sb_attention",
    )(table, q, k, v)


def _mm_res_ln_kernel(a_ref, w_ref, x_ref, g_ref, b_ref, o_ref):
    y = _dot(a_ref[...], w_ref[...])
    o_ref[...] = _layer_norm(DEEPNORM_ALPHA * x_ref[...] + y, g_ref[...], b_ref[...])


def _mm_res_ln(a, w_bf16, x, g, b, tm):
    t, d = x.shape
    kdim = a.shape[1]
    vec = pl.BlockSpec((1, d), lambda i: (0, 0))
    return pl.pallas_call(
        _mm_res_ln_kernel,
        grid=(t // tm,),
        in_specs=[pl.BlockSpec((tm, kdim), lambda i: (i, 0)),
                  pl.BlockSpec((kdim, d), lambda i: (0, 0)),
                  pl.BlockSpec((tm, d), lambda i: (i, 0)), vec, vec],
        out_specs=pl.BlockSpec((tm, d), lambda i: (i, 0)),
        out_shape=jax.ShapeDtypeStruct((t, d), F32),
        compiler_params=_params("parallel"),
        name="attn_out_ln",
    )(a, w_bf16, x, g.reshape(1, d), b.reshape(1, d))


def _swiglu_step(xb, wg_ref, wu_ref, wd_ref):
    gate = _dot(xb, wg_ref[...])
    up = _dot(xb, wu_ref[...])
    hidden = gate * jax.nn.sigmoid(gate) * up
    return _dot(hidden.astype(BF16), wd_ref[...])


def _ffn_ln_kernel(x_ref, wg_ref, wu_ref, wd_ref, g_ref, b_ref, o_ref, xb_scr, acc_scr):
    j = pl.program_id(1)

    @pl.when(j == 0)
    def _():
        xb_scr[...] = x_ref[...].astype(BF16)
        acc_scr[...] = jnp.zeros_like(acc_scr)

    acc_scr[...] += _swiglu_step(xb_scr[...], wg_ref, wu_ref, wd_ref)

    @pl.when(j == pl.num_programs(1) - 1)
    def _():
        o_ref[...] = _layer_norm(DEEPNORM_ALPHA * x_ref[...] + acc_scr[...], g_ref[...], b_ref[...])


def _ffn_ln(x, w_gate_up_bf16, w_down_bf16, g, b, tm, tn):
    t, d = x.shape
    f = w_down_bf16.shape[0]
    nj = f // tn
    vec = pl.BlockSpec((1, d), lambda i, j: (0, 0))
    return pl.pallas_call(
        _ffn_ln_kernel,
        grid=(t // tm, nj),
        in_specs=[pl.BlockSpec((tm, d), lambda i, j: (i, 0)),
                  pl.BlockSpec((d, tn), lambda i, j: (0, j)),
                  pl.BlockSpec((d, tn), lambda i, j: (0, nj + j)),
                  pl.BlockSpec((tn, d), lambda i, j: (j, 0)), vec, vec],
        out_specs=pl.BlockSpec((tm, d), lambda i, j: (i, 0)),
        out_shape=jax.ShapeDtypeStruct((t, d), F32),
        scratch_shapes=[pltpu.VMEM((tm, d), BF16), pltpu.VMEM((tm, d), F32)],
        compiler_params=_params("parallel", "arbitrary"),
        name="dense_swiglu_ln",
    )(x, w_gate_up_bf16, w_gate_up_bf16, w_down_bf16, g.reshape(1, d), b.reshape(1, d))


def _moe_ffn_kernel(te_ref, nt_ref, x_ref, wg_ref, wu_ref, wd_ref, o_ref, xb_scr, acc_scr):
    i = pl.program_id(0)
    j = pl.program_id(1)
    used = i < nt_ref[0]

    @pl.when(used & (j == 0))
    def _():
        xb_scr[...] = x_ref[...].astype(BF16)
        acc_scr[...] = jnp.zeros_like(acc_scr)

    @pl.when(used)
    def _():
        acc_scr[...] += _swiglu_step(xb_scr[...], wg_ref.at[0], wu_ref.at[0], wd_ref.at[0])

    @pl.when(j == pl.num_programs(1) - 1)
    def _():
        o_ref[...] = jnp.where(used, acc_scr[...], 0.0)


def _moe_ffn(tile_expert, n_tiles_used, xg, w_gate_up_bf16, w_down_bf16, tm, tn):
    n, d = xg.shape
    f = w_down_bf16.shape[1]
    nj = f // tn
    grid_spec = pltpu.PrefetchScalarGridSpec(
        num_scalar_prefetch=2,
        grid=(n // tm, nj),
        in_specs=[pl.BlockSpec((tm, d), lambda i, j, te, nt: (i, 0)),
                  pl.BlockSpec((1, d, tn), lambda i, j, te, nt: (te[i], 0, j)),
                  pl.BlockSpec((1, d, tn), lambda i, j, te, nt: (te[i], 0, nj + j)),
                  pl.BlockSpec((1, tn, d), lambda i, j, te, nt: (te[i], j, 0))],
        out_specs=pl.BlockSpec((tm, d), lambda i, j, te, nt: (i, 0)),
        scratch_shapes=[pltpu.VMEM((tm, d), BF16), pltpu.VMEM((tm, d), F32)],
    )
    return pl.pallas_call(
        _moe_ffn_kernel,
        grid_spec=grid_spec,
        out_shape=jax.ShapeDtypeStruct((n, d), F32),
        compiler_params=_params("parallel", "arbitrary"),
        name="expert_swiglu",
    )(tile_expert, n_tiles_used, xg, w_gate_up_bf16, w_gate_up_bf16, w_down_bf16)


def _row_copy(src_hbm, dst_ref, sem, src_row, dst_row):
    return pltpu.make_async_copy(src_hbm.at[pl.ds(src_row, 1)], dst_ref.at[pl.ds(dst_row, 1)], sem)


def _gather_rows_into(idx_ref, src_hbm, dst_ref, sem, n):
    def start(r, c):
        _row_copy(src_hbm, dst_ref, sem, idx_ref[0, 0, r], r).start()
        return c

    lax.fori_loop(0, n, start, 0, unroll=DMA_ISSUE_UNROLL)
    pltpu.make_async_copy(src_hbm.at[pl.ds(0, n)], dst_ref, sem).wait()


def _scatter_copy(x_ref, dst_hbm, sem, src_row, dst_row):
    return pltpu.make_async_copy(x_ref.at[pl.ds(src_row, 1)], dst_hbm.at[pl.ds(dst_row, 1)], sem)


def _scatter_kernel(p0_ref, p1_ref, x_ref, dst_in_hbm_unused, dst_hbm, sem, *, tm):
    def start(r, c):
        _scatter_copy(x_ref, dst_hbm, sem, r, p0_ref[0, 0, r]).start()
        _scatter_copy(x_ref, dst_hbm, sem, r, p1_ref[0, 0, r]).start()
        return c

    lax.fori_loop(0, tm, start, 0, unroll=DMA_ISSUE_UNROLL)
    for _ in range(2):
        pltpu.make_async_copy(x_ref, dst_hbm.at[pl.ds(0, tm)], sem).wait()


def _scatter_rows(x, pos0, pos1, dst, tm):
    t, d = x.shape
    idx = pl.BlockSpec((1, 1, tm), lambda i: (i, 0, 0), memory_space=pltpu.SMEM)
    return pl.pallas_call(
        functools.partial(_scatter_kernel, tm=tm),
        grid=(t // tm,),
        in_specs=[idx, idx, pl.BlockSpec((tm, d), lambda i: (i, 0)), pl.BlockSpec(memory_space=pl.ANY)],
        out_specs=pl.BlockSpec(memory_space=pl.ANY),
        out_shape=jax.ShapeDtypeStruct(dst.shape, dst.dtype),
        input_output_aliases={3: 0},
        scratch_shapes=[pltpu.SemaphoreType.DMA(())],
        compiler_params=_params("arbitrary"),
        name="scatter_rows",
    )(pos0.reshape(-1, 1, tm), pos1.reshape(-1, 1, tm), x, dst)


def _combine_ln_kernel(p0_ref, p1_ref, x_ref, route_ref, y_hbm, g_ref, b_ref, op_ref, os_ref, a_scr, b_scr,
                       sem0, sem1, *, tm, n_prompt_tiles):
    _gather_rows_into(p0_ref, y_hbm, a_scr, sem0, tm)
    _gather_rows_into(p1_ref, y_hbm, b_scr, sem1, tm)
    moe = route_ref[:, 2:3] * a_scr[...] + route_ref[:, 3:4] * b_scr[...]
    out = _layer_norm(DEEPNORM_ALPHA * x_ref[...] + moe, g_ref[...], b_ref[...])
    i = pl.program_id(0)

    @pl.when(i < n_prompt_tiles)
    def _():
        op_ref[...] = out

    @pl.when(i >= n_prompt_tiles)
    def _():
        os_ref[...] = out


def _combine_ln(x, route, y_sorted, pos0, pos1, g, b, tm, n_prompt):
    t, d = x.shape
    npt = n_prompt // tm
    nst = (t - n_prompt) // tm
    idx = pl.BlockSpec((1, 1, tm), lambda i: (i, 0, 0), memory_space=pltpu.SMEM)
    vec = pl.BlockSpec((1, d), lambda i: (0, 0))
    return pl.pallas_call(
        functools.partial(_combine_ln_kernel, tm=tm, n_prompt_tiles=npt),
        grid=(npt + nst,),
        in_specs=[idx, idx, pl.BlockSpec((tm, d), lambda i: (i, 0)), pl.BlockSpec((tm, LANES), lambda i: (i, 0)),
                  pl.BlockSpec(memory_space=pl.ANY), vec, vec],
        out_specs=[pl.BlockSpec((tm, d), lambda i: (jnp.minimum(i, npt - 1), 0)),
                   pl.BlockSpec((tm, d), lambda i: (jnp.maximum(i - npt, 0), 0))],
        out_shape=[jax.ShapeDtypeStruct((n_prompt, d), F32), jax.ShapeDtypeStruct((t - n_prompt, d), F32)],
        scratch_shapes=[pltpu.VMEM((tm, d), F32), pltpu.VMEM((tm, d), F32),
                        pltpu.SemaphoreType.DMA(()), pltpu.SemaphoreType.DMA(())],
        compiler_params=_params("arbitrary"),
        name="moe_combine_ln",
    )(pos0.reshape(-1, 1, tm), pos1.reshape(-1, 1, tm), x, route, y_sorted, g.reshape(1, d), b.reshape(1, d))


def _rwkv_proj_kernel(x_ref, prev_ref, mu_ref, wr_ref, wk_ref, wv_ref, w1_ref, w2_ref, a1_ref, a2_ref,
                      g1_ref, g2_ref, vec_ref,
                      r_ref, lw_ref, k_ref, v_ref, kk_ref, b_ref, g_ref, bonus_ref, *, seg):
    x = x_ref[...]
    tm, d = x.shape
    rolled = pltpu.roll(x, 1, 0).reshape(tm // seg, seg, d)
    first = lax.broadcasted_iota(jnp.int32, (tm // seg, seg, d), 1) == 0
    x_prev = jnp.where(first, prev_ref[...], rolled).reshape(tm, d)
    xx = x_prev - x

    def mixed(i):
        return (x + xx * mu_ref[i:i + 1, :]).astype(BF16)

    r = _dot(mixed(0), wr_ref[...])
    k = _dot(mixed(2), wk_ref[...])
    v = _dot(mixed(3), wv_ref[...])
    w0, a0, k_k, k_a, r_k = (vec_ref[i:i + 1, :] for i in range(5))
    wl = w0 + _dot(jnp.tanh(_dot(mixed(1), w1_ref[...])).astype(BF16), w2_ref[...])
    softplus_neg = jnp.maximum(-wl, 0.0) + jnp.log(1.0 + jnp.exp(-jnp.abs(wl)))
    lw_ref[...] = -jnp.exp(-softplus_neg - 0.5)
    a = jax.nn.sigmoid(a0 + _dot(_dot(mixed(4), a1_ref[...]).astype(BF16), a2_ref[...]))
    g_ref[...] = _dot(jax.nn.sigmoid(_dot(mixed(5), g1_ref[...])).astype(BF16), g2_ref[...])
    ones_pair = _head_ones(PAIR)
    kk = k * k_k
    kk = kk / jnp.maximum(jnp.sqrt(_head_sum(kk * kk, ones_pair)), 1e-12)
    k = k * (1.0 + (a - 1.0) * k_a)
    r_ref[...] = r
    k_ref[...] = k
    v_ref[...] = v
    kk_ref[...] = kk
    b_ref[...] = kk * a
    bonus_ref[...] = _head_sum(r * k * r_k, ones_pair) * v


def _rwkv_proj(x, prev_rows, mu, w_r, w_k, w_v, w1, w2, a1, a2, g1, g2, vecs, tm, seg):
    t, d = x.shape
    row = pl.BlockSpec((tm, d), lambda i: (i, 0))

    def full(a):
        return pl.BlockSpec(a.shape, lambda i: (0,) * a.ndim)

    consts = (mu, w_r, w_k, w_v, w1, w2, a1, a2, g1, g2, vecs)
    return pl.pallas_call(
        functools.partial(_rwkv_proj_kernel, seg=seg),
        grid=(t // tm,),
        in_specs=[row, pl.BlockSpec((tm // seg, 1, d), lambda i: (i, 0, 0))] + [full(c) for c in consts],
        out_specs=[row] * 8,
        out_shape=[jax.ShapeDtypeStruct((t, d), F32)] * 8,
        compiler_params=_params("parallel"),
        name="rwkv_proj",
    )(x, prev_rows, *consts)


def _unit_lower_inverses(lows, limit):
    n = lows[0].shape[0]
    r = lax.broadcasted_iota(jnp.int32, (n, n), 0)
    c = lax.broadcasted_iota(jnp.int32, (n, n), 1)
    eye = (r == c).astype(F32)
    base = (r >> 3) == (c >> 3)
    ps = [jnp.where(base, -low, 0.0) for low in lows]
    invs = [eye + p for p in ps]
    for _ in range(2):
        ps = [_dot2(p, p) for p in ps]
        invs = [inv + _dot2(inv, p) for inv, p in zip(invs, ps)]
    shift = 3
    while (1 << shift) < limit:
        sel = ((r >> (shift + 1)) == (c >> (shift + 1))) & ((r >> shift) != (c >> shift))
        mids = [_dot2(inv, jnp.where(sel, low, 0.0)) for inv, low in zip(invs, lows)]
        invs = [inv - _dot2(mid, inv) for inv, mid in zip(invs, mids)]
        shift += 1
    return invs


def _rwkv_scan_kernel(r_ref, lw_ref, k_ref, v_ref, kk_ref, b_ref, h0_ref, o_ref, hT_ref, h_scr, *, n_chunks, n_pairs):
    cs = SCAN_CHUNK
    tb = pl.program_id(2)

    @pl.when(tb == 0)
    def _():
        h_scr[...] = h0_ref[0]

    lane = lax.broadcasted_iota(jnp.int32, (cs, PAIR), 1)
    row = lax.broadcasted_iota(jnp.int32, (cs, PAIR), 0)
    head_a = lane < HEAD_DIM
    strict = (lane & (HEAD_DIM - 1)) < row
    incl = (lane & (HEAD_DIM - 1)) <= row
    tri = (lax.broadcasted_iota(jnp.int32, (cs, cs), 1) <= lax.broadcasted_iota(jnp.int32, (cs, cs), 0)).astype(BF16)
    pr = lax.broadcasted_iota(jnp.int32, (PAIR, PAIR), 0) >> 6
    pc = lax.broadcasted_iota(jnp.int32, (PAIR, PAIR), 1) >> 6
    same_head = pr == pc

    def stack(x):
        return jnp.concatenate([jnp.where(head_a, x, 0.0), jnp.where(head_a, 0.0, x)], axis=0)

    def each(fn, *lists):
        return [fn(*xs) for xs in zip(*lists)]

    def cumulative(lw):
        l1 = lw.astype(BF16)
        rem = lw - l1.astype(F32)
        l2 = rem.astype(BF16)
        l3 = (rem - l2.astype(F32)).astype(BF16)
        parts = _dot(tri, jnp.concatenate([l1, l2, l3], axis=1))
        return parts[:, :PAIR] + parts[:, PAIR:2 * PAIR] + parts[:, 2 * PAIR:]

    def chunk(ci, carry):
        sl = pl.ds(pl.multiple_of(ci * cs, cs), cs)
        lanes = [slice(p * PAIR, (p + 1) * PAIR) for p in range(n_pairs)]
        r, lw, k, v, kk, b = ([ref[0, sl, ln] for ln in lanes] for ref in (r_ref, lw_ref, k_ref, v_ref, kk_ref, b_ref))
        h = [h_scr[p] for p in range(n_pairs)]
        cum = each(cumulative, lw)
        total = [c[cs - 1:cs, :] for c in cum]
        r_hat = each(lambda x, c: x * jnp.exp(c), r, cum)
        kk_hat = each(lambda x, c, l: x * jnp.exp(c - l), kk, cum, lw)
        k_hat = each(lambda x, c: x * jnp.exp(-c), k, cum)
        b_hat = each(lambda x, c: x * jnp.exp(-c), b, cum)
        gram = each(lambda kh, rh, bh, k2: _dot2_nt(jnp.concatenate([kh, rh], axis=0),
                                                    jnp.concatenate([stack(bh), stack(k2)], axis=0)),
                    kk_hat, r_hat, b_hat, k_hat)
        l_b = [jnp.where(strict, g[:cs, :PAIR], 0.0) for g in gram]
        l_k = [jnp.where(strict, g[:cs, PAIR:], 0.0) for g in gram]
        a_rb = [jnp.where(incl, g[cs:, :PAIR], 0.0) for g in gram]
        a_rk = [jnp.where(incl, g[cs:, PAIR:], 0.0) for g in gram]
        v_st = each(stack, v)
        w = each(lambda kh, lk, hh, vs: _dot2(jnp.concatenate([kh, lk], axis=1), jnp.concatenate([hh, vs], axis=0)),
                 kk_hat, l_k, h, v_st)
        inv = _unit_lower_inverses(each(stack, l_b), cs)
        u_st = each(lambda t_, w_: _dot2(t_, stack(w_)), inv, w)
        o = each(lambda rh, ak, ab, hh, vs, us: _dot2(jnp.concatenate([rh, ak, -ab], axis=1),
                                                      jnp.concatenate([hh, vs, us], axis=0)),
                 r_hat, a_rk, a_rb, h, v_st, u_st)
        for ln, o_p in zip(lanes, o):
            o_ref[0, sl, ln] = o_p
        u = [us[:cs] + us[cs:] for us in u_st]
        to_end = each(lambda t_, c: jnp.exp(t_ - c), total, cum)
        upd = each(lambda k_, b_, e_, v_, u_: _dot2_tn(jnp.concatenate([k_ * e_, -(b_ * e_)], axis=0),
                                                       jnp.concatenate([v_, u_], axis=0)),
                   k, b, to_end, v, u)
        for p in range(n_pairs):
            decay_col = jnp.transpose(jnp.broadcast_to(jnp.exp(total[p]), (PAIR, PAIR)))
            h_scr[p] = decay_col * h[p] + jnp.where(same_head, upd[p], 0.0)
        return carry

    lax.fori_loop(0, n_chunks, chunk, 0)

    @pl.when(tb == pl.num_programs(2) - 1)
    def _():
        hT_ref[0] = h_scr[...]


def _rwkv_scan(r, lw, k, v, kk, b, h0, tb):
    bsz, t, d = r.shape
    npair = d // PAIR
    group = math.gcd(npair, SCAN_PAIRS_PER_STEP)
    seq = pl.BlockSpec((1, tb, group * PAIR), lambda bi, pi, ti: (bi, ti, pi))
    st = pl.BlockSpec((1, group, PAIR, PAIR), lambda bi, pi, ti: (bi, pi, 0, 0))
    return pl.pallas_call(
        functools.partial(_rwkv_scan_kernel, n_chunks=tb // SCAN_CHUNK, n_pairs=group),
        grid=(bsz, npair // group, t // tb),
        in_specs=[seq] * 6 + [st],
        out_specs=[seq, st],
        out_shape=[jax.ShapeDtypeStruct((bsz, t, d), F32), jax.ShapeDtypeStruct((bsz, npair, PAIR, PAIR), F32)],
        scratch_shapes=[pltpu.VMEM((group, PAIR, PAIR), F32)],
        compiler_params=_params("parallel", "parallel", "arbitrary"),
        name="rwkv_scan",
    )(r, lw, k, v, kk, b, h0)


def _rwkv_out_kernel(o_ref, g_ref, bonus_ref, x_ref, wo_ref, vec_ref, dst_hbm_unused, out_ref):
    o = o_ref[...]
    ones_pair = _head_ones(PAIR)
    mean = _head_sum(o, ones_pair) * (1.0 / HEAD_DIM)
    c = o - mean
    var = _head_sum(c * c, ones_pair) * (1.0 / HEAD_DIM)
    on = c * lax.rsqrt(var + GN_EPS) * vec_ref[0:1, :] + vec_ref[1:2, :]
    y = _dot(((on + bonus_ref[...]) * g_ref[...]).astype(BF16), wo_ref[...])
    out_ref[...] = _layer_norm(DEEPNORM_ALPHA * x_ref[...] + y, vec_ref[2:3, :], vec_ref[3:4, :])


def _rwkv_out(o, g, bonus, x, w_o_bf16, vecs, dst, n_rows, row_offset, tm):
    t, d = x.shape
    row = pl.BlockSpec((tm, d), lambda i: (i, 0))
    off = row_offset // tm
    return pl.pallas_call(
        _rwkv_out_kernel,
        grid=(t // tm,),
        in_specs=[row, row, row, row, pl.BlockSpec((d, d), lambda i: (0, 0)),
                  pl.BlockSpec(vecs.shape, lambda i: (0, 0)), pl.BlockSpec(memory_space=pl.ANY)],
        out_specs=pl.BlockSpec((tm, d), lambda i: (off + i, 0)),
        out_shape=jax.ShapeDtypeStruct((n_rows, d), F32),
        input_output_aliases={6: 0},
        compiler_params=_params("parallel"),
        name="rwkv_out_ln",
    )(o, g, bonus, x, w_o_bf16, vecs, dst)


def _router_kernel(x_ref, w_ref, o_ref, *, n_experts):
    logits = _dot3(x_ref[...], w_ref[...])
    lane = lax.broadcasted_iota(jnp.int32, logits.shape, 1)
    lane_f = lane.astype(F32)
    lowest = jnp.float32(-3.0e38)
    logits = jnp.where(lane < n_experts, logits, lowest)
    m1 = jnp.max(logits, axis=1, keepdims=True)
    i1 = jnp.min(jnp.where(logits == m1, lane_f, float(LANES)), axis=1, keepdims=True)
    rest = jnp.where(lane_f == i1, lowest, logits)
    m2 = jnp.max(rest, axis=1, keepdims=True)
    i2 = jnp.min(jnp.where(rest == m2, lane_f, float(LANES)), axis=1, keepdims=True)
    e = jnp.exp(m2 - m1)
    g1 = 1.0 / (1.0 + e)
    g2 = e * g1
    out = jnp.where(lane == 0, i1, 0.0)
    out = jnp.where(lane == 1, i2, out)
    out = jnp.where(lane == 2, g1, out)
    o_ref[...] = jnp.where(lane == 3, g2, out)


def _router(x, w_router, tm):
    t, d = x.shape
    e = w_router.shape[1]
    w_pad = jnp.zeros((d, LANES), F32).at[:, :e].set(w_router)
    return pl.pallas_call(
        functools.partial(_router_kernel, n_experts=e),
        grid=(t // tm,),
        in_specs=[pl.BlockSpec((tm, d), lambda i: (i, 0)), pl.BlockSpec((d, LANES), lambda i: (0, 0))],
        out_specs=pl.BlockSpec((tm, LANES), lambda i: (i, 0)),
        out_shape=jax.ShapeDtypeStruct((t, LANES), F32),
        compiler_params=_params("parallel"),
        name="router_top2",
    )(x, w_pad)


def _ffn_chunk(f, cap=1536):
    best = LANES
    for tn in range(LANES, cap + 1, LANES):
        if f % tn == 0:
            best = tn
    return best


def _row_tile(t, cap):
    tm = cap
    while t % tm:
        tm //= 2
    return tm


def _pair_states(s):
    b, h, n, _ = s.shape
    st = jnp.swapaxes(s, 2, 3).reshape(b, h // 2, 2, n, n)
    z = jnp.zeros_like(st[:, :, 0])
    top = jnp.concatenate([st[:, :, 0], z], axis=-1)
    bot = jnp.concatenate([z, st[:, :, 1]], axis=-1)
    return jnp.concatenate([top, bot], axis=-2)


def _unpair_states(hp):
    b, p, _, _ = hp.shape
    n = HEAD_DIM
    st = jnp.stack([hp[:, :, :n, :n], hp[:, :, n:, n:]], axis=2).reshape(b, 2 * p, n, n)
    return jnp.swapaxes(st, 2, 3)


def _attention_layer(x, w_qkv, w_o, ln_g, ln_b, cache_k=None, cache_v=None):
    b, t, d = x.shape
    q, kb, vb, kf, vf = _qkv_proj(x, w_qkv, _row_tile(t, 512))
    if cache_k is None:
        ao = _sb_attention(q, kb, vb, _row_tile(t, 512), LANES, 0)
    else:
        past = cache_k.shape[2]
        merge = lambda c: jnp.swapaxes(c.astype(BF16), 1, 2).reshape(b, past, d)
        keys = jnp.concatenate([merge(cache_k), kb], axis=1)
        vals = jnp.concatenate([merge(cache_v), vb], axis=1)
        ao = _sb_attention(q, keys, vals, t, t, past)
    x1 = _mm_res_ln(ao.reshape(b * t, d), w_o, x.reshape(b * t, d), ln_g, ln_b, _row_tile(b * t, 512))
    return x1, kf, vf


def _rwkv_layer(x, x_last, s0, p, dst, n_rows, row_offset):
    b, t, d = x.shape
    seg = SCAN_CHUNK
    xs = x.reshape(b, t // seg, seg, d)
    prev = jnp.concatenate([x_last[:, None, :], xs[:, :-1, seg - 1, :]], axis=1).reshape(b * t // seg, 1, d)
    xf = x.reshape(b * t, d)
    tm = _row_tile(b * t, 256)
    r, lw, k, v, kk, bb, g, bonus = _rwkv_proj(xf, prev, p["mu"], p["w_r"], p["w_k"], p["w_v"], p["w1"], p["w2"],
                                               p["a1"], p["a2"], p["g1"], p["g2"], p["proj_vecs"], tm, seg)
    shp = (b, t, d)
    o, h_t = _rwkv_scan(*(z.reshape(shp) for z in (r, lw, k, v, kk, bb)), _pair_states(s0), _row_tile(t, 512))
    dst = _rwkv_out(o.reshape(b * t, d), g, bonus, xf, p["w_o"], p["out_vecs"], dst, n_rows, row_offset, tm)
    return dst, _unpair_states(h_t)


def _moe_layer(x, n_prompt, w_router, w_gate_up, w_down, ln_g, ln_b, tm):
    t, d = x.shape
    e = w_router.shape[1]
    route = _router(x, w_router, tm)
    experts = jnp.concatenate([route[:, 0], route[:, 1]]).astype(jnp.int32)
    onehot = (experts[:, None] == jnp.arange(e, dtype=jnp.int32)[None, :]).astype(jnp.int32)
    counts = jnp.sum(onehot, axis=0)
    padded = ((counts + tm - 1) // tm) * tm
    ends = jnp.cumsum(padded)
    starts = ends - padded
    rank = jnp.sum(jnp.cumsum(onehot, axis=0) * onehot, axis=1) - 1
    pos = (jnp.sum(starts[None, :] * onehot, axis=1) + rank).astype(jnp.int32)
    n_rows = ((2 * t + e * (tm - 1)) // tm) * tm
    tile_start = jnp.arange(n_rows // tm, dtype=jnp.int32) * tm
    tile_expert = jnp.minimum(jnp.sum((tile_start[:, None] >= ends[None, :]).astype(jnp.int32), axis=1), e - 1)
    n_tiles_used = (ends[-1] // tm).astype(jnp.int32).reshape(1)
    xg = _scatter_rows(x, pos[:t], pos[t:], jnp.zeros((n_rows, d), F32), tm)
    y = _moe_ffn(tile_expert, n_tiles_used, xg, w_gate_up, w_down, tm, _ffn_chunk(w_down.shape[1]))
    return _combine_ln(x, route, y, pos[:t], pos[t:], ln_g, ln_b, tm, n_prompt)


def kernel(x_prompt, x_sample, cache_k, cache_v, state_wkv, state_shift, att_w_qkv, att_w_o, ffn_w_gate_up, ffn_w_down, rwkv_mu, rwkv_w_rkv, rwkv_w0, rwkv_w1, rwkv_w2, rwkv_a0, rwkv_a1, rwkv_a2, rwkv_g1, rwkv_g2, rwkv_k_k, rwkv_k_a, rwkv_r_k, rwkv_gn_g, rwkv_gn_b, rwkv_w_o, moe_w_router, moe_w_gate_up, moe_w_down, ln_mix_g, ln_mix_b, ln_ffn_g, ln_ffn_b):
    bp, tp, d = x_prompt.shape
    bs, ts, _ = x_sample.shape
    n_prompt, n_sample = bp * tp, bs * ts
    bf = lambda a: a.astype(BF16)

    w_qkv, w_o = bf(att_w_qkv[0]), bf(att_w_o[0])
    xp1, k_p, v_p = _attention_layer(x_prompt, w_qkv, w_o, ln_mix_g[0], ln_mix_b[0])
    xs1, k_s, v_s = _attention_layer(x_sample, w_qkv, w_o, ln_mix_g[0], ln_mix_b[0], cache_k[0], cache_v[0])
    w_gu, w_dn = bf(ffn_w_gate_up[0]), bf(ffn_w_down[0])
    tn = _ffn_chunk(w_dn.shape[0])
    xp2 = _ffn_ln(xp1, w_gu, w_dn, ln_ffn_g[0], ln_ffn_b[0], _row_tile(n_prompt, 512), tn).reshape(bp, tp, d)
    xs2 = _ffn_ln(xs1, w_gu, w_dn, ln_ffn_g[0], ln_ffn_b[0], _row_tile(n_sample, 512), tn).reshape(bs, ts, d)

    p = {
        "mu": rwkv_mu[0], "w_r": bf(rwkv_w_rkv[0, 0]), "w_k": bf(rwkv_w_rkv[0, 1]), "w_v": bf(rwkv_w_rkv[0, 2]),
        "w1": bf(rwkv_w1[0]), "w2": bf(rwkv_w2[0]), "a1": bf(rwkv_a1[0]), "a2": bf(rwkv_a2[0]),
        "g1": bf(rwkv_g1[0]), "g2": bf(rwkv_g2[0]), "w_o": bf(rwkv_w_o[0]),
        "proj_vecs": jnp.stack([rwkv_w0[0], rwkv_a0[0], rwkv_k_k[0], rwkv_k_a[0], rwkv_r_k[0].reshape(d)]),
        "out_vecs": jnp.stack([rwkv_gn_g[0], rwkv_gn_b[0], ln_mix_g[1], ln_mix_b[1]]),
    }
    n_all = n_prompt + n_sample
    x3, wkv_p = _rwkv_layer(xp2, jnp.zeros((bp, d), F32), jnp.zeros((bp, d // HEAD_DIM, HEAD_DIM, HEAD_DIM), F32),
                            p, jnp.zeros((n_all, d), F32), n_all, 0)
    x3, wkv_s = _rwkv_layer(xs2, state_shift[0], state_wkv[0], p, x3, n_all, n_prompt)
    y_p, y_s = _moe_layer(x3, n_prompt, moe_w_router[0], bf(moe_w_gate_up[0]), bf(moe_w_down[0]),
                          ln_ffn_g[1], ln_ffn_b[1], _row_tile(n_sample, 512))
    return (y_p.reshape(bp, tp, d), y_s.reshape(bs, ts, d),
            k_p[None], v_p[None], wkv_p[None], xp2[:, -1][None],
            k_s[None], v_s[None], wkv_s[None], xs2[:, -1][None])
```

```python
import functools
import math

import jax
import jax.numpy as jnp
import numpy as np
from jax import lax
from jax.experimental import pallas as pl
from jax.experimental.pallas import tpu as pltpu

F32 = jnp.float32
BF16 = jnp.bfloat16

HEAD_DIM = 64
LANES = 128
PAIR = 2 * HEAD_DIM
LN_EPS = 1e-5
GN_EPS = 64e-5
DEEPNORM_ALPHA = 4.0 ** 0.25
Q_SCALE = HEAD_DIM ** -0.5 * 1.4426950408889634
SCAN_PAIRS_PER_STEP = 8
DMA_ISSUE_UNROLL = 8
SCAN_CHUNK = 64
V7X_VMEM_LIMIT = 56 * 1024 * 1024


def _params(*sem, vmem=V7X_VMEM_LIMIT):
    return pltpu.CompilerParams(dimension_semantics=sem, vmem_limit_bytes=vmem)


def _layer_norm(y, g, b):
    mu = jnp.mean(y, axis=-1, keepdims=True)
    c = y - mu
    var = jnp.mean(c * c, axis=-1, keepdims=True)
    return c * lax.rsqrt(var + LN_EPS) * g + b


def _split2(x):
    hi = x.astype(BF16)
    lo = (x - hi.astype(F32)).astype(BF16)
    return hi, lo


def _dot(a, b):
    return jnp.dot(a, b, preferred_element_type=F32)


def _dot_nt(a, b):
    return lax.dot_general(a, b, (((1,), (1,)), ((), ())), preferred_element_type=F32)


def _dot_tn(a, b):
    return lax.dot_general(a, b, (((0,), (0,)), ((), ())), preferred_element_type=F32)


def _dot3(a, b):
    ah, al = _split2(a)
    bh, bl = _split2(b)
    return _dot(jnp.concatenate([ah, ah, al], axis=1), jnp.concatenate([bh, bl, bh], axis=0))


def _dot2(a, b):
    ah, al = _split2(a)
    bb = b.astype(BF16)
    return _dot(jnp.concatenate([ah, al], axis=1), jnp.concatenate([bb, bb], axis=0))


def _dot2_nt(a, b):
    ah, al = _split2(a)
    bb = b.astype(BF16)
    return _dot_nt(jnp.concatenate([ah, al], axis=1), jnp.concatenate([bb, bb], axis=1))


def _dot2_tn(a, b):
    ah, al = _split2(a)
    bb = b.astype(BF16)
    return _dot_tn(jnp.concatenate([ah, al], axis=0), jnp.concatenate([bb, bb], axis=0))


def _dot2_exact_rhs(a, b_bf16):
    ah, al = _split2(a)
    return _dot(jnp.concatenate([ah, al], axis=1), jnp.concatenate([b_bf16, b_bf16], axis=0))


def _head_ones(n):
    r = lax.broadcasted_iota(jnp.int32, (n, n), 0) >> 6
    c = lax.broadcasted_iota(jnp.int32, (n, n), 1) >> 6
    return (r == c).astype(BF16)


def _head_sum(x, ones_pair):
    d = x.shape[1]
    cols = [_dot2_exact_rhs(x[:, c:c + PAIR], ones_pair) for c in range(0, d, PAIR)]
    return jnp.concatenate(cols, axis=1)


def _qkv_kernel(x_ref, wq_ref, wk_ref, wv_ref, qb_ref, kb_ref, vb_ref, kf_ref, vf_ref, xb_scr, *, hpb):
    @pl.when(pl.program_id(2) == 0)
    def _():
        xb_scr[...] = x_ref[0].astype(BF16)

    xb = xb_scr[...]
    q = _dot(xb, wq_ref[...]) * Q_SCALE
    k = _dot(xb, wk_ref[...])
    v = _dot(xb, wv_ref[...])
    qb_ref[0] = q.astype(BF16)
    kb_ref[0] = k.astype(BF16)
    vb_ref[0] = v.astype(BF16)
    for h in range(hpb):
        sl = slice(h * HEAD_DIM, (h + 1) * HEAD_DIM)
        kf_ref[0, h] = k[:, sl]
        vf_ref[0, h] = v[:, sl]


def _qkv_proj(x, w_bf16, tm):
    b, t, d = x.shape
    h = d // HEAD_DIM
    hpb = 4
    tn = hpb * HEAD_DIM
    nb = d // tn
    grid = (b, t // tm, nb)
    hspec = pl.BlockSpec((1, hpb, tm, HEAD_DIM), lambda bi, ti, ni: (bi, ni, ti, 0))
    shp = (b, h, t, HEAD_DIM)
    return pl.pallas_call(
        functools.partial(_qkv_kernel, hpb=hpb),
        grid=grid,
        in_specs=[
            pl.BlockSpec((1, tm, d), lambda bi, ti, ni: (bi, ti, 0)),
            pl.BlockSpec((d, tn), lambda bi, ti, ni: (0, ni)),
            pl.BlockSpec((d, tn), lambda bi, ti, ni: (0, nb + ni)),
            pl.BlockSpec((d, tn), lambda bi, ti, ni: (0, 2 * nb + ni)),
        ],
        out_specs=[pl.BlockSpec((1, tm, tn), lambda bi, ti, ni: (bi, ti, ni))] * 3 + [hspec] * 2,
        out_shape=[jax.ShapeDtypeStruct((b, t, d), BF16)] * 3 + [jax.ShapeDtypeStruct(shp, F32)] * 2,
        scratch_shapes=[pltpu.VMEM((tm, d), BF16)],
        compiler_params=_params("parallel", "parallel", "arbitrary"),
        name="qkv_proj",
    )(x, w_bf16, w_bf16, w_bf16)


def _aligned(x, m):
    return x if isinstance(x, int) else pl.multiple_of(x, m)


ATTN_STAGES = 5
ATTN_QUERY_BLOCK = 256
MASKED_SCORE = -1.0e30


DEAD_LOG2 = -160.0
F_Q0, F_K0, F_BIAS, F_FIRST, F_REAL, F_QB, F_NEXT, F_LANE = range(8)
ATTN_MAX_LANES = 4


def _attn_schedule(nq, per_q, past_blocks, bq, bk):
    n_lanes = 1
    while n_lanes * 2 <= min(ATTN_MAX_LANES, nq):
        n_lanes *= 2
    items = [[0, 0, 0, 1, 0, -1, 0, lane] for lane in range(n_lanes)]
    lane_start, lane_end = [], []
    for lane in range(n_lanes):
        lane_start.append(len(items))
        mine = [qb for qb in range(nq)
                if (qb % (2 * n_lanes) if qb % (2 * n_lanes) < n_lanes else 2 * n_lanes - 1 - qb % (2 * n_lanes)) == lane]
        for qb in mine:
            n_full = past_blocks + qb * per_q
            blocks = [(n_full + j, j + 1) for j in reversed(range(per_q))] + [(kb, 0) for kb in reversed(range(n_full))]
            nxt = len(items) + len(blocks)
            for i, (kb, bias_id) in enumerate(blocks):
                items.append([qb * bq, kb * bk, bias_id, int(i == 0), 1, qb, nxt, lane])
        lane_end.append(len(items))
    return items, n_lanes, lane_start, lane_end


def _sb_attn_kernel(tab_ref, q_ref, k_ref, v_ref, o_ref, z_scr, lb_scr, x2_scr, sums_scr, a_scr, later_scr,
                    acc_scr, bias_scr, suffix_scr, cur_ref, dead_ref, qprev_ref, valid_ref,
                    *, bq, bk, tab_len, n_lanes, lane_start, lane_end):
    per_q = bq // bk
    for ln in range(n_lanes):
        cur_ref[ln] = lane_start[ln]
        dead_ref[ln] = -1
        qprev_ref[ln] = 0
        valid_ref[ln] = 0
    r = lax.broadcasted_iota(jnp.int32, (2 * bk, 2 * bk), 0)
    c = lax.broadcasted_iota(jnp.int32, (2 * bk, 2 * bk), 1)
    r = jnp.where(r >= bk, r - bk, r)
    suffix_scr[...] = ((c >= bk) | (r > c)).astype(BF16)
    qi = lax.broadcasted_iota(jnp.int32, (bq, bk), 0)
    kj = lax.broadcasted_iota(jnp.int32, (bq, bk), 1)
    bias_scr[0] = jnp.zeros((bq, bk), F32)
    for j in range(per_q):
        bias_scr[j + 1] = jnp.where((kj + j * bk) < qi, 0.0, MASKED_SCORE)
    for scr in (z_scr, lb_scr, x2_scr, sums_scr, a_scr, later_scr, acc_scr):
        scr[...] = jnp.zeros_like(scr)
    lane_a = lax.broadcasted_iota(jnp.int32, (bk, PAIR), 1) < HEAD_DIM

    def field(f, i):
        return tab_ref[f * tab_len + i]

    def stack_heads(blk):
        zero = jnp.zeros_like(blk)
        return jnp.concatenate([jnp.where(lane_a, blk, zero), jnp.where(lane_a, zero, blk)], axis=0)

    def keep_going(c):
        return c[5] < ATTN_STAGES + n_lanes

    def body(c):
        trip, i1, i2, i3, i4, idle = c
        lane = trip & (n_lanes - 1)
        end = jnp.int32(lane_end[0])
        for ln in range(1, n_lanes):
            end = jnp.where(lane == ln, lane_end[ln], end)
        nxt = cur_ref[lane]
        peek = jnp.minimum(nxt, tab_len - 1)
        nxt = jnp.where((nxt < end) & (field(F_QB, peek) == dead_ref[lane]), field(F_NEXT, peek), nxt)
        more = nxt < end
        i0 = jnp.where(more, nxt, lane)
        cur_ref[lane] = jnp.where(more, nxt + 1, nxt)
        idle = jnp.where(more, 0, idle + 1)
        first5 = field(F_FIRST, i4) == 1
        lane5 = field(F_LANE, i4)

        @pl.when(first5 & (valid_ref[lane5] == 1))
        def _():
            o_ref[0, pl.ds(pl.multiple_of(qprev_ref[lane5], bq), bq), :] = acc_scr[lane5].astype(BF16)

        qprev_ref[lane5] = field(F_Q0, i4)
        valid_ref[lane5] = field(F_REAL, i4)
        k5 = pl.multiple_of(field(F_K0, i4), bk)
        contrib = _dot(a_scr[...], stack_heads(v_ref[0, pl.ds(k5, bk), :]))
        acc_scr[lane5] = jnp.where(first5, contrib, acc_scr[lane5] + contrib)
        slot = trip & 1
        restart = field(F_FIRST, i3) == 1
        lane4 = field(F_LANE, i3)
        lowest = None
        for h in range(2):
            cols = slice(h * bk, (h + 1) * bk)
            later = jnp.where(restart, 0.0, later_scr[lane4, :, cols])
            between = sums_scr[h, :, :bk] + later
            later = later + sums_scr[h, :, bk:]
            later_scr[lane4, :, cols] = later
            lowest = later if lowest is None else jnp.maximum(lowest, later)
            a_scr[:, cols] = jnp.exp2(lb_scr[slot, :, cols] + between).astype(BF16)
        dead_ref[lane4] = jnp.where(jnp.max(lowest) < DEAD_LOG2, field(F_QB, i3), -1)
        for h in range(2):
            sums_scr[h] = _dot(x2_scr[h], suffix_scr[...])
        bias = bias_scr[field(F_BIAS, i1)]
        for h in range(2):
            cols = slice(h * bk, (h + 1) * bk)
            z = z_scr[:, cols] + bias
            soft = jnp.log2(1.0 + jnp.exp2(-jnp.abs(z)))
            log_beta = jnp.minimum(z, 0.0) - soft
            hi, lo = _split2(log_beta - z)
            lb_scr[slot, :, cols] = log_beta
            x2_scr[h] = jnp.concatenate([hi, lo], axis=1)
        q1 = pl.multiple_of(field(F_Q0, i0), bq)
        k1 = pl.multiple_of(field(F_K0, i0), bk)
        z_scr[...] = _dot_nt(q_ref[0, pl.ds(q1, bq), :], stack_heads(k_ref[0, pl.ds(k1, bk), :]))
        return (trip + 1, i0, i1, i2, i3, idle)

    zero = jnp.int32(0)
    lax.while_loop(keep_going, body, (zero, zero, zero, zero, zero, zero))


def _sb_attention(q, k, v, bq, bk, q_start):
    b, tq, d = q.shape
    tk = k.shape[1]
    assert tq % bq == 0 and bq % bk == 0 and q_start % bk == 0 and tk == q_start + tq
    per_q = bq // bk
    items, n_lanes, lane_start, lane_end = _attn_schedule(tq // bq, per_q, q_start // bk, bq, bk)
    lane_smem = pltpu.SMEM((n_lanes,), jnp.int32)
    table = jnp.asarray(np.asarray(items, np.int32).T.reshape(-1))
    qspec = pl.BlockSpec((1, tq, PAIR), lambda bi, pi, tab: (bi, 0, pi))
    kspec = pl.BlockSpec((1, tk, PAIR), lambda bi, pi, tab: (bi, 0, pi))
    grid_spec = pltpu.PrefetchScalarGridSpec(
        num_scalar_prefetch=1,
        grid=(b, d // PAIR),
        in_specs=[qspec, kspec, kspec],
        out_specs=qspec,
        scratch_shapes=[pltpu.VMEM((bq, 2 * bk), F32), pltpu.VMEM((2, bq, 2 * bk), F32),
                        pltpu.VMEM((2, bq, 2 * bk), BF16), pltpu.VMEM((2, bq, 2 * bk), F32),
                        pltpu.VMEM((bq, 2 * bk), BF16), pltpu.VMEM((n_lanes, bq, 2 * bk), F32),
                        pltpu.VMEM((n_lanes, bq, PAIR), F32), pltpu.VMEM((per_q + 1, bq, bk), F32),
                        pltpu.VMEM((2 * bk, 2 * bk), BF16), lane_smem, lane_smem, lane_smem, lane_smem],
    )
    return pl.pallas_call(
        functools.partial(_sb_attn_kernel, bq=bq, bk=bk, tab_len=len(items), n_lanes=n_lanes,
                          lane_start=tuple(lane_start), lane_end=tuple(lane_end)),
        grid_spec=grid_spec,
        out_shape=jax.ShapeDtypeStruct((b, tq, d), BF16),
        compiler_params=_params("parallel", "parallel"),
        name="sb_attention",
    )(table, q, k, v)


def _mm_res_ln_kernel(a_ref, w_ref, x_ref, g_ref, b_ref, o_ref):
    y = _dot(a_ref[...], w_ref[...])
    o_ref[...] = _layer_norm(DEEPNORM_ALPHA * x_ref[...] + y, g_ref[...], b_ref[...])


def _mm_res_ln(a, w_bf16, x, g, b, tm):
    t, d = x.shape
    kdim = a.shape[1]
    vec = pl.BlockSpec((1, d), lambda i: (0, 0))
    return pl.pallas_call(
        _mm_res_ln_kernel,
        grid=(t // tm,),
        in_specs=[pl.BlockSpec((tm, kdim), lambda i: (i, 0)),
                  pl.BlockSpec((kdim, d), lambda i: (0, 0)),
                  pl.BlockSpec((tm, d), lambda i: (i, 0)), vec, vec],
        out_specs=pl.BlockSpec((tm, d), lambda i: (i, 0)),
        out_shape=jax.ShapeDtypeStruct((t, d), F32),
        compiler_params=_params("parallel"),
        name="attn_out_ln",
    )(a, w_bf16, x, g.reshape(1, d), b.reshape(1, d))


def _swiglu_step(xb, wg_ref, wu_ref, wd_ref):
    gate = _dot(xb, wg_ref[...])
    up = _dot(xb, wu_ref[...])
    hidden = gate * jax.nn.sigmoid(gate) * up
    return _dot(hidden.astype(BF16), wd_ref[...])


def _ffn_ln_kernel(x_ref, wg_ref, wu_ref, wd_ref, g_ref, b_ref, o_ref, xb_scr, acc_scr):
    j = pl.program_id(1)

    @pl.when(j == 0)
    def _():
        xb_scr[...] = x_ref[...].astype(BF16)
        acc_scr[...] = jnp.zeros_like(acc_scr)

    acc_scr[...] += _swiglu_step(xb_scr[...], wg_ref, wu_ref, wd_ref)

    @pl.when(j == pl.num_programs(1) - 1)
    def _():
        o_ref[...] = _layer_norm(DEEPNORM_ALPHA * x_ref[...] + acc_scr[...], g_ref[...], b_ref[...])


def _ffn_ln(x, w_gate_up_bf16, w_down_bf16, g, b, tm, tn):
    t, d = x.shape
    f = w_down_bf16.shape[0]
    nj = f // tn
    vec = pl.BlockSpec((1, d), lambda i, j: (0, 0))
    return pl.pallas_call(
        _ffn_ln_kernel,
        grid=(t // tm, nj),
        in_specs=[pl.BlockSpec((tm, d), lambda i, j: (i, 0)),
                  pl.BlockSpec((d, tn), lambda i, j: (0, j)),
                  pl.BlockSpec((d, tn), lambda i, j: (0, nj + j)),
                  pl.BlockSpec((tn, d), lambda i, j: (j, 0)), vec, vec],
        out_specs=pl.BlockSpec((tm, d), lambda i, j: (i, 0)),
        out_shape=jax.ShapeDtypeStruct((t, d), F32),
        scratch_shapes=[pltpu.VMEM((tm, d), BF16), pltpu.VMEM((tm, d), F32)],
        compiler_params=_params("parallel", "arbitrary"),
        name="dense_swiglu_ln",
    )(x, w_gate_up_bf16, w_gate_up_bf16, w_down_bf16, g.reshape(1, d), b.reshape(1, d))


def _moe_ffn_kernel(te_ref, nt_ref, x_ref, wg_ref, wu_ref, wd_ref, o_ref, xb_scr, acc_scr):
    i = pl.program_id(0)
    j = pl.program_id(1)
    used = i < nt_ref[0]

    @pl.when(used & (j == 0))
    def _():
        xb_scr[...] = x_ref[...].astype(BF16)
        acc_scr[...] = jnp.zeros_like(acc_scr)

    @pl.when(used)
    def _():
        acc_scr[...] += _swiglu_step(xb_scr[...], wg_ref.at[0], wu_ref.at[0], wd_ref.at[0])

    @pl.when(j == pl.num_programs(1) - 1)
    def _():
        o_ref[...] = jnp.where(used, acc_scr[...], 0.0)


def _moe_ffn(tile_expert, n_tiles_used, xg, w_gate_up_bf16, w_down_bf16, tm, tn):
    n, d = xg.shape
    f = w_down_bf16.shape[1]
    nj = f // tn
    grid_spec = pltpu.PrefetchScalarGridSpec(
        num_scalar_prefetch=2,
        grid=(n // tm, nj),
        in_specs=[pl.BlockSpec((tm, d), lambda i, j, te, nt: (i, 0)),
                  pl.BlockSpec((1, d, tn), lambda i, j, te, nt: (te[i], 0, j)),
                  pl.BlockSpec((1, d, tn), lambda i, j, te, nt: (te[i], 0, nj + j)),
                  pl.BlockSpec((1, tn, d), lambda i, j, te, nt: (te[i], j, 0))],
        out_specs=pl.BlockSpec((tm, d), lambda i, j, te, nt: (i, 0)),
        scratch_shapes=[pltpu.VMEM((tm, d), BF16), pltpu.VMEM((tm, d), F32)],
    )
    return pl.pallas_call(
        _moe_ffn_kernel,
        grid_spec=grid_spec,
        out_shape=jax.ShapeDtypeStruct((n, d), F32),
        compiler_params=_params("parallel", "arbitrary"),
        name="expert_swiglu",
    )(tile_expert, n_tiles_used, xg, w_gate_up_bf16, w_gate_up_bf16, w_down_bf16)


def _row_copy(src_hbm, dst_ref, sem, src_row, dst_row):
    return pltpu.make_async_copy(src_hbm.at[pl.ds(src_row, 1)], dst_ref.at[pl.ds(dst_row, 1)], sem)


def _gather_rows_into(idx_ref, src_hbm, dst_ref, sem, n):
    def start(r, c):
        _row_copy(src_hbm, dst_ref, sem, idx_ref[0, 0, r], r).start()
        return c

    lax.fori_loop(0, n, start, 0, unroll=DMA_ISSUE_UNROLL)
    pltpu.make_async_copy(src_hbm.at[pl.ds(0, n)], dst_ref, sem).wait()


def _scatter_copy(x_ref, dst_hbm, sem, src_row, dst_row):
    return pltpu.make_async_copy(x_ref.at[pl.ds(src_row, 1)], dst_hbm.at[pl.ds(dst_row, 1)], sem)


def _scatter_kernel(p0_ref, p1_ref, x_ref, dst_in_hbm_unused, dst_hbm, sem, *, tm):
    def start(r, c):
        _scatter_copy(x_ref, dst_hbm, sem, r, p0_ref[0, 0, r]).start()
        _scatter_copy(x_ref, dst_hbm, sem, r, p1_ref[0, 0, r]).start()
        return c

    lax.fori_loop(0, tm, start, 0, unroll=DMA_ISSUE_UNROLL)
    for _ in range(2):
        pltpu.make_async_copy(x_ref, dst_hbm.at[pl.ds(0, tm)], sem).wait()


def _scatter_rows(x, pos0, pos1, dst, tm):
    t, d = x.shape
    idx = pl.BlockSpec((1, 1, tm), lambda i: (i, 0, 0), memory_space=pltpu.SMEM)
    return pl.pallas_call(
        functools.partial(_scatter_kernel, tm=tm),
        grid=(t // tm,),
        in_specs=[idx, idx, pl.BlockSpec((tm, d), lambda i: (i, 0)), pl.BlockSpec(memory_space=pl.ANY)],
        out_specs=pl.BlockSpec(memory_space=pl.ANY),
        out_shape=jax.ShapeDtypeStruct(dst.shape, dst.dtype),
        input_output_aliases={3: 0},
        scratch_shapes=[pltpu.SemaphoreType.DMA(())],
        compiler_params=_params("arbitrary"),
        name="scatter_rows",
    )(pos0.reshape(-1, 1, tm), pos1.reshape(-1, 1, tm), x, dst)


def _combine_ln_kernel(p0_ref, p1_ref, x_ref, route_ref, y_hbm, g_ref, b_ref, op_ref, os_ref, a_scr, b_scr,
                       sem0, sem1, *, tm, n_prompt_tiles):
    _gather_rows_into(p0_ref, y_hbm, a_scr, sem0, tm)
    _gather_rows_into(p1_ref, y_hbm, b_scr, sem1, tm)
    moe = route_ref[:, 2:3] * a_scr[...] + route_ref[:, 3:4] * b_scr[...]
    out = _layer_norm(DEEPNORM_ALPHA * x_ref[...] + moe, g_ref[...], b_ref[...])
    i = pl.program_id(0)

    @pl.when(i < n_prompt_tiles)
    def _():
        op_ref[...] = out

    @pl.when(i >= n_prompt_tiles)
    def _():
        os_ref[...] = out


def _combine_ln(x, route, y_sorted, pos0, pos1, g, b, tm, n_prompt):
    t, d = x.shape
    npt = n_prompt // tm
    nst = (t - n_prompt) // tm
    idx = pl.BlockSpec((1, 1, tm), lambda i: (i, 0, 0), memory_space=pltpu.SMEM)
    vec = pl.BlockSpec((1, d), lambda i: (0, 0))
    return pl.pallas_call(
        functools.partial(_combine_ln_kernel, tm=tm, n_prompt_tiles=npt),
        grid=(npt + nst,),
        in_specs=[idx, idx, pl.BlockSpec((tm, d), lambda i: (i, 0)), pl.BlockSpec((tm, LANES), lambda i: (i, 0)),
                  pl.BlockSpec(memory_space=pl.ANY), vec, vec],
        out_specs=[pl.BlockSpec((tm, d), lambda i: (jnp.minimum(i, npt - 1), 0)),
                   pl.BlockSpec((tm, d), lambda i: (jnp.maximum(i - npt, 0), 0))],
        out_shape=[jax.ShapeDtypeStruct((n_prompt, d), F32), jax.ShapeDtypeStruct((t - n_prompt, d), F32)],
        scratch_shapes=[pltpu.VMEM((tm, d), F32), pltpu.VMEM((tm, d), F32),
                        pltpu.SemaphoreType.DMA(()), pltpu.SemaphoreType.DMA(())],
        compiler_params=_params("arbitrary"),
        name="moe_combine_ln",
    )(pos0.reshape(-1, 1, tm), pos1.reshape(-1, 1, tm), x, route, y_sorted, g.reshape(1, d), b.reshape(1, d))


def _rwkv_proj_kernel(x_ref, prev_ref, mu_ref, wr_ref, wk_ref, wv_ref, w1_ref, w2_ref, a1_ref, a2_ref,
                      g1_ref, g2_ref, vec_ref,
                      r_ref, lw_ref, k_ref, v_ref, kk_ref, b_ref, g_ref, bonus_ref, *, seg):
    x = x_ref[...]
    tm, d = x.shape
    rolled = pltpu.roll(x, 1, 0).reshape(tm // seg, seg, d)
    first = lax.broadcasted_iota(jnp.int32, (tm // seg, seg, d), 1) == 0
    x_prev = jnp.where(first, prev_ref[...], rolled).reshape(tm, d)
    xx = x_prev - x

    def mixed(i):
        return (x + xx * mu_ref[i:i + 1, :]).astype(BF16)

    r = _dot(mixed(0), wr_ref[...])
    k = _dot(mixed(2), wk_ref[...])
    v = _dot(mixed(3), wv_ref[...])
    w0, a0, k_k, k_a, r_k = (vec_ref[i:i + 1, :] for i in range(5))
    wl = w0 + _dot(jnp.tanh(_dot(mixed(1), w1_ref[...])).astype(BF16), w2_ref[...])
    softplus_neg = jnp.maximum(-wl, 0.0) + jnp.log(1.0 + jnp.exp(-jnp.abs(wl)))
    lw_ref[...] = -jnp.exp(-softplus_neg - 0.5)
    a = jax.nn.sigmoid(a0 + _dot(_dot(mixed(4), a1_ref[...]).astype(BF16), a2_ref[...]))
    g_ref[...] = _dot(jax.nn.sigmoid(_dot(mixed(5), g1_ref[...])).astype(BF16), g2_ref[...])
    ones_pair = _head_ones(PAIR)
    kk = k * k_k
    kk = kk / jnp.maximum(jnp.sqrt(_head_sum(kk * kk, ones_pair)), 1e-12)
    k = k * (1.0 + (a - 1.0) * k_a)
    r_ref[...] = r
    k_ref[...] = k
    v_ref[...] = v
    kk_ref[...] = kk
    b_ref[...] = kk * a
    bonus_ref[...] = _head_sum(r * k * r_k, ones_pair) * v


def _rwkv_proj(x, prev_rows, mu, w_r, w_k, w_v, w1, w2, a1, a2, g1, g2, vecs, tm, seg):
    t, d = x.shape
    row = pl.BlockSpec((tm, d), lambda i: (i, 0))

    def full(a):
        return pl.BlockSpec(a.shape, lambda i: (0,) * a.ndim)

    consts = (mu, w_r, w_k, w_v, w1, w2, a1, a2, g1, g2, vecs)
    return pl.pallas_call(
        functools.partial(_rwkv_proj_kernel, seg=seg),
        grid=(t // tm,),
        in_specs=[row, pl.BlockSpec((tm // seg, 1, d), lambda i: (i, 0, 0))] + [full(c) for c in consts],
        out_specs=[row] * 8,
        out_shape=[jax.ShapeDtypeStruct((t, d), F32)] * 8,
        compiler_params=_params("parallel"),
        name="rwkv_proj",
    )(x, prev_rows, *consts)


def _unit_lower_inverses(lows, limit):
    n = lows[0].shape[0]
    r = lax.broadcasted_iota(jnp.int32, (n, n), 0)
    c = lax.broadcasted_iota(jnp.int32, (n, n), 1)
    eye = (r == c).astype(F32)
    base = (r >> 3) == (c >> 3)
    ps = [jnp.where(base, -low, 0.0) for low in lows]
    invs = [eye + p for p in ps]
    for _ in range(2):
        ps = [_dot2(p, p) for p in ps]
        invs = [inv + _dot2(inv, p) for inv, p in zip(invs, ps)]
    shift = 3
    while (1 << shift) < limit:
        sel = ((r >> (shift + 1)) == (c >> (shift + 1))) & ((r >> shift) != (c >> shift))
        mids = [_dot2(inv, jnp.where(sel, low, 0.0)) for inv, low in zip(invs, lows)]
        invs = [inv - _dot2(mid, inv) for inv, mid in zip(invs, mids)]
        shift += 1
    return invs


def _rwkv_scan_kernel(r_ref, lw_ref, k_ref, v_ref, kk_ref, b_ref, h0_ref, o_ref, hT_ref, h_scr, *, n_chunks, n_pairs):
    cs = SCAN_CHUNK
    tb = pl.program_id(2)

    @pl.when(tb == 0)
    def _():
        h_scr[...] = h0_ref[0]

    lane = lax.broadcasted_iota(jnp.int32, (cs, PAIR), 1)
    row = lax.broadcasted_iota(jnp.int32, (cs, PAIR), 0)
    head_a = lane < HEAD_DIM
    strict = (lane & (HEAD_DIM - 1)) < row
    incl = (lane & (HEAD_DIM - 1)) <= row
    tri = (lax.broadcasted_iota(jnp.int32, (cs, cs), 1) <= lax.broadcasted_iota(jnp.int32, (cs, cs), 0)).astype(BF16)
    pr = lax.broadcasted_iota(jnp.int32, (PAIR, PAIR), 0) >> 6
    pc = lax.broadcasted_iota(jnp.int32, (PAIR, PAIR), 1) >> 6
    same_head = pr == pc

    def stack(x):
        return jnp.concatenate([jnp.where(head_a, x, 0.0), jnp.where(head_a, 0.0, x)], axis=0)

    def each(fn, *lists):
        return [fn(*xs) for xs in zip(*lists)]

    def cumulative(lw):
        l1 = lw.astype(BF16)
        rem = lw - l1.astype(F32)
        l2 = rem.astype(BF16)
        l3 = (rem - l2.astype(F32)).astype(BF16)
        parts = _dot(tri, jnp.concatenate([l1, l2, l3], axis=1))
        return parts[:, :PAIR] + parts[:, PAIR:2 * PAIR] + parts[:, 2 * PAIR:]

    def chunk(ci, carry):
        sl = pl.ds(pl.multiple_of(ci * cs, cs), cs)
        lanes = [slice(p * PAIR, (p + 1) * PAIR) for p in range(n_pairs)]
        r, lw, k, v, kk, b = ([ref[0, sl, ln] for ln in lanes] for ref in (r_ref, lw_ref, k_ref, v_ref, kk_ref, b_ref))
        h = [h_scr[p] for p in range(n_pairs)]
        cum = each(cumulative, lw)
        total = [c[cs - 1:cs, :] for c in cum]
        r_hat = each(lambda x, c: x * jnp.exp(c), r, cum)
        kk_hat = each(lambda x, c, l: x * jnp.exp(c - l), kk, cum, lw)
        k_hat = each(lambda x, c: x * jnp.exp(-c), k, cum)
        b_hat = each(lambda x, c: x * jnp.exp(-c), b, cum)
        gram = each(lambda kh, rh, bh, k2: _dot2_nt(jnp.concatenate([kh, rh], axis=0),
                                                    jnp.concatenate([stack(bh), stack(k2)], axis=0)),
                    kk_hat, r_hat, b_hat, k_hat)
        l_b = [jnp.where(strict, g[:cs, :PAIR], 0.0) for g in gram]
        l_k = [jnp.where(strict, g[:cs, PAIR:], 0.0) for g in gram]
        a_rb = [jnp.where(incl, g[cs:, :PAIR], 0.0) for g in gram]
        a_rk = [jnp.where(incl, g[cs:, PAIR:], 0.0) for g in gram]
        v_st = each(stack, v)
        w = each(lambda kh, lk, hh, vs: _dot2(jnp.concatenate([kh, lk], axis=1), jnp.concatenate([hh, vs], axis=0)),
                 kk_hat, l_k, h, v_st)
        inv = _unit_lower_inverses(each(stack, l_b), cs)
        u_st = each(lambda t_, w_: _dot2(t_, stack(w_)), inv, w)
        o = each(lambda rh, ak, ab, hh, vs, us: _dot2(jnp.concatenate([rh, ak, -ab], axis=1),
                                                      jnp.concatenate([hh, vs, us], axis=0)),
                 r_hat, a_rk, a_rb, h, v_st, u_st)
        for ln, o_p in zip(lanes, o):
            o_ref[0, sl, ln] = o_p
        u = [us[:cs] + us[cs:] for us in u_st]
        to_end = each(lambda t_, c: jnp.exp(t_ - c), total, cum)
        upd = each(lambda k_, b_, e_, v_, u_: _dot2_tn(jnp.concatenate([k_ * e_, -(b_ * e_)], axis=0),
                                                       jnp.concatenate([v_, u_], axis=0)),
                   k, b, to_end, v, u)
        for p in range(n_pairs):
            decay_col = jnp.transpose(jnp.broadcast_to(jnp.exp(total[p]), (PAIR, PAIR)))
            h_scr[p] = decay_col * h[p] + jnp.where(same_head, upd[p], 0.0)
        return carry

    lax.fori_loop(0, n_chunks, chunk, 0)

    @pl.when(tb == pl.num_programs(2) - 1)
    def _():
        hT_ref[0] = h_scr[...]


def _rwkv_scan(r, lw, k, v, kk, b, h0, tb):
    bsz, t, d = r.shape
    npair = d // PAIR
    group = math.gcd(npair, SCAN_PAIRS_PER_STEP)
    seq = pl.BlockSpec((1, tb, group * PAIR), lambda bi, pi, ti: (bi, ti, pi))
    st = pl.BlockSpec((1, group, PAIR, PAIR), lambda bi, pi, ti: (bi, pi, 0, 0))
    return pl.pallas_call(
        functools.partial(_rwkv_scan_kernel, n_chunks=tb // SCAN_CHUNK, n_pairs=group),
        grid=(bsz, npair // group, t // tb),
        in_specs=[seq] * 6 + [st],
        out_specs=[seq, st],
        out_shape=[jax.ShapeDtypeStruct((bsz, t, d), F32), jax.ShapeDtypeStruct((bsz, npair, PAIR, PAIR), F32)],
        scratch_shapes=[pltpu.VMEM((group, PAIR, PAIR), F32)],
        compiler_params=_params("parallel", "parallel", "arbitrary"),
        name="rwkv_scan",
    )(r, lw, k, v, kk, b, h0)


def _rwkv_out_kernel(o_ref, g_ref, bonus_ref, x_ref, wo_ref, vec_ref, dst_hbm_unused, out_ref):
    o = o_ref[...]
    ones_pair = _head_ones(PAIR)
    mean = _head_sum(o, ones_pair) * (1.0 / HEAD_DIM)
    c = o - mean
    var = _head_sum(c * c, ones_pair) * (1.0 / HEAD_DIM)
    on = c * lax.rsqrt(var + GN_EPS) * vec_ref[0:1, :] + vec_ref[1:2, :]
    y = _dot(((on + bonus_ref[...]) * g_ref[...]).astype(BF16), wo_ref[...])
    out_ref[...] = _layer_norm(DEEPNORM_ALPHA * x_ref[...] + y, vec_ref[2:3, :], vec_ref[3:4, :])


def _rwkv_out(o, g, bonus, x, w_o_bf16, vecs, dst, n_rows, row_offset, tm):
    t, d = x.shape
    row = pl.BlockSpec((tm, d), lambda i: (i, 0))
    off = row_offset // tm
    return pl.pallas_call(
        _rwkv_out_kernel,
        grid=(t // tm,),
        in_specs=[row, row, row, row, pl.BlockSpec((d, d), lambda i: (0, 0)),
                  pl.BlockSpec(vecs.shape, lambda i: (0, 0)), pl.BlockSpec(memory_space=pl.ANY)],
        out_specs=pl.BlockSpec((tm, d), lambda i: (off + i, 0)),
        out_shape=jax.ShapeDtypeStruct((n_rows, d), F32),
        input_output_aliases={6: 0},
        compiler_params=_params("parallel"),
        name="rwkv_out_ln",
    )(o, g, bonus, x, w_o_bf16, vecs, dst)


def _router_kernel(x_ref, w_ref, o_ref, *, n_experts):
    logits = _dot3(x_ref[...], w_ref[...])
    lane = lax.broadcasted_iota(jnp.int32, logits.shape, 1)
    lane_f = lane.astype(F32)
    lowest = jnp.float32(-3.0e38)
    logits = jnp.where(lane < n_experts, logits, lowest)
    m1 = jnp.max(logits, axis=1, keepdims=True)
    i1 = jnp.min(jnp.where(logits == m1, lane_f, float(LANES)), axis=1, keepdims=True)
    rest = jnp.where(lane_f == i1, lowest, logits)
    m2 = jnp.max(rest, axis=1, keepdims=True)
    i2 = jnp.min(jnp.where(rest == m2, lane_f, float(LANES)), axis=1, keepdims=True)
    e = jnp.exp(m2 - m1)
    g1 = 1.0 / (1.0 + e)
    g2 = e * g1
    out = jnp.where(lane == 0, i1, 0.0)
    out = jnp.where(lane == 1, i2, out)
    out = jnp.where(lane == 2, g1, out)
    o_ref[...] = jnp.where(lane == 3, g2, out)


def _router(x, w_router, tm):
    t, d = x.shape
    e = w_router.shape[1]
    w_pad = jnp.zeros((d, LANES), F32).at[:, :e].set(w_router)
    return pl.pallas_call(
        functools.partial(_router_kernel, n_experts=e),
        grid=(t // tm,),
        in_specs=[pl.BlockSpec((tm, d), lambda i: (i, 0)), pl.BlockSpec((d, LANES), lambda i: (0, 0))],
        out_specs=pl.BlockSpec((tm, LANES), lambda i: (i, 0)),
        out_shape=jax.ShapeDtypeStruct((t, LANES), F32),
        compiler_params=_params("parallel"),
        name="router_top2",
    )(x, w_pad)


def _ffn_chunk(f, cap=1536):
    best = LANES
    for tn in range(LANES, cap + 1, LANES):
        if f % tn == 0:
            best = tn
    return best


def _row_tile(t, cap):
    tm = cap
    while t % tm:
        tm //= 2
    return tm


def _pair_states(s):
    b, h, n, _ = s.shape
    st = jnp.swapaxes(s, 2, 3).reshape(b, h // 2, 2, n, n)
    z = jnp.zeros_like(st[:, :, 0])
    top = jnp.concatenate([st[:, :, 0], z], axis=-1)
    bot = jnp.concatenate([z, st[:, :, 1]], axis=-1)
    return jnp.concatenate([top, bot], axis=-2)


def _unpair_states(hp):
    b, p, _, _ = hp.shape
    n = HEAD_DIM
    st = jnp.stack([hp[:, :, :n, :n], hp[:, :, n:, n:]], axis=2).reshape(b, 2 * p, n, n)
    return jnp.swapaxes(st, 2, 3)


def _attention_layer(x, w_qkv, w_o, ln_g, ln_b, cache_k=None, cache_v=None):
    b, t, d = x.shape
    q, kb, vb, kf, vf = _qkv_proj(x, w_qkv, _row_tile(t, 512))
    if cache_k is None:
        ao = _sb_attention(q, kb, vb, _row_tile(t, ATTN_QUERY_BLOCK), LANES, 0)
    else:
        past = cache_k.shape[2]
        merge = lambda c: jnp.swapaxes(c.astype(BF16), 1, 2).reshape(b, past, d)
        keys = jnp.concatenate([merge(cache_k), kb], axis=1)
        vals = jnp.concatenate([merge(cache_v), vb], axis=1)
        ao = _sb_attention(q, keys, vals, t, t, past)
    x1 = _mm_res_ln(ao.reshape(b * t, d), w_o, x.reshape(b * t, d), ln_g, ln_b, _row_tile(b * t, 512))
    return x1, kf, vf


def _rwkv_layer(x, x_last, s0, p, dst, n_rows, row_offset):
    b, t, d = x.shape
    seg = SCAN_CHUNK
    xs = x.reshape(b, t // seg, seg, d)
    prev = jnp.concatenate([x_last[:, None, :], xs[:, :-1, seg - 1, :]], axis=1).reshape(b * t // seg, 1, d)
    xf = x.reshape(b * t, d)
    tm = _row_tile(b * t, 256)
    r, lw, k, v, kk, bb, g, bonus = _rwkv_proj(xf, prev, p["mu"], p["w_r"], p["w_k"], p["w_v"], p["w1"], p["w2"],
                                               p["a1"], p["a2"], p["g1"], p["g2"], p["proj_vecs"], tm, seg)
    shp = (b, t, d)
    o, h_t = _rwkv_scan(*(z.reshape(shp) for z in (r, lw, k, v, kk, bb)), _pair_states(s0), _row_tile(t, 512))
    dst = _rwkv_out(o.reshape(b * t, d), g, bonus, xf, p["w_o"], p["out_vecs"], dst, n_rows, row_offset, tm)
    return dst, _unpair_states(h_t)


def _moe_layer(x, n_prompt, w_router, w_gate_up, w_down, ln_g, ln_b, tm):
    t, d = x.shape
    e = w_router.shape[1]
    route = _router(x, w_router, tm)
    experts = jnp.concatenate([route[:, 0], route[:, 1]]).astype(jnp.int32)
    onehot = (experts[:, None] == jnp.arange(e, dtype=jnp.int32)[None, :]).astype(jnp.int32)
    counts = jnp.sum(onehot, axis=0)
    padded = ((counts + tm - 1) // tm) * tm
    ends = jnp.cumsum(padded)
    starts = ends - padded
    rank = jnp.sum(jnp.cumsum(onehot, axis=0) * onehot, axis=1) - 1
    pos = (jnp.sum(starts[None, :] * onehot, axis=1) + rank).astype(jnp.int32)
    n_rows = ((2 * t + e * (tm - 1)) // tm) * tm
    tile_start = jnp.arange(n_rows // tm, dtype=jnp.int32) * tm
    tile_expert = jnp.minimum(jnp.sum((tile_start[:, None] >= ends[None, :]).astype(jnp.int32), axis=1), e - 1)
    n_tiles_used = (ends[-1] // tm).astype(jnp.int32).reshape(1)
    xg = _scatter_rows(x, pos[:t], pos[t:], jnp.zeros((n_rows, d), F32), tm)
    y = _moe_ffn(tile_expert, n_tiles_used, xg, w_gate_up, w_down, tm, _ffn_chunk(w_down.shape[1], cap=2048))
    return _combine_ln(x, route, y, pos[:t], pos[t:], ln_g, ln_b, tm, n_prompt)


def kernel(x_prompt, x_sample, cache_k, cache_v, state_wkv, state_shift, att_w_qkv, att_w_o, ffn_w_gate_up, ffn_w_down, rwkv_mu, rwkv_w_rkv, rwkv_w0, rwkv_w1, rwkv_w2, rwkv_a0, rwkv_a1, rwkv_a2, rwkv_g1, rwkv_g2, rwkv_k_k, rwkv_k_a, rwkv_r_k, rwkv_gn_g, rwkv_gn_b, rwkv_w_o, moe_w_router, moe_w_gate_up, moe_w_down, ln_mix_g, ln_mix_b, ln_ffn_g, ln_ffn_b):
    bp, tp, d = x_prompt.shape
    bs, ts, _ = x_sample.shape
    n_prompt, n_sample = bp * tp, bs * ts
    bf = lambda a: a.astype(BF16)

    w_qkv, w_o = bf(att_w_qkv[0]), bf(att_w_o[0])
    xp1, k_p, v_p = _attention_layer(x_prompt, w_qkv, w_o, ln_mix_g[0], ln_mix_b[0])
    xs1, k_s, v_s = _attention_layer(x_sample, w_qkv, w_o, ln_mix_g[0], ln_mix_b[0], cache_k[0], cache_v[0])
    w_gu, w_dn = bf(ffn_w_gate_up[0]), bf(ffn_w_down[0])
    tn = _ffn_chunk(w_dn.shape[0])
    xp2 = _ffn_ln(xp1, w_gu, w_dn, ln_ffn_g[0], ln_ffn_b[0], _row_tile(n_prompt, 512), tn).reshape(bp, tp, d)
    xs2 = _ffn_ln(xs1, w_gu, w_dn, ln_ffn_g[0], ln_ffn_b[0], _row_tile(n_sample, 512), tn).reshape(bs, ts, d)

    p = {
        "mu": rwkv_mu[0], "w_r": bf(rwkv_w_rkv[0, 0]), "w_k": bf(rwkv_w_rkv[0, 1]), "w_v": bf(rwkv_w_rkv[0, 2]),
        "w1": bf(rwkv_w1[0]), "w2": bf(rwkv_w2[0]), "a1": bf(rwkv_a1[0]), "a2": bf(rwkv_a2[0]),
        "g1": bf(rwkv_g1[0]), "g2": bf(rwkv_g2[0]), "w_o": bf(rwkv_w_o[0]),
        "proj_vecs": jnp.stack([rwkv_w0[0], rwkv_a0[0], rwkv_k_k[0], rwkv_k_a[0], rwkv_r_k[0].reshape(d)]),
        "out_vecs": jnp.stack([rwkv_gn_g[0], rwkv_gn_b[0], ln_mix_g[1], ln_mix_b[1]]),
    }
    n_all = n_prompt + n_sample
    x3, wkv_p = _rwkv_layer(xp2, jnp.zeros((bp, d), F32), jnp.zeros((bp, d // HEAD_DIM, HEAD_DIM, HEAD_DIM), F32),
                            p, jnp.zeros((n_all, d), F32), n_all, 0)
    x3, wkv_s = _rwkv_layer(xs2, state_shift[0], state_wkv[0], p, x3, n_all, n_prompt)
    y_p, y_s = _moe_layer(x3, n_prompt, moe_w_router[0], bf(moe_w_gate_up[0]), bf(moe_w_down[0]),
                          ln_ffn_g[1], ln_ffn_b[1], _row_tile(n_sample, 512))
    return (y_p.reshape(bp, tp, d), y_s.reshape(bs, ts, d),
            k_p[None], v_p[None], wkv_p[None], xp2[:, -1][None],
            k_s[None], v_s[None], wkv_s[None], xs2[:, -1][None])
```

```python
import functools
import math

import jax
import jax.numpy as jnp
import numpy as np
from jax import lax
from jax.experimental import pallas as pl
from jax.experimental.pallas import tpu as pltpu

F32 = jnp.float32
BF16 = jnp.bfloat16

HEAD_DIM = 64
LANES = 128
PAIR = 2 * HEAD_DIM
LN_EPS = 1e-5
GN_EPS = 64e-5
DEEPNORM_ALPHA = 4.0 ** 0.25
Q_SCALE = HEAD_DIM ** -0.5 * 1.4426950408889634
SCAN_PAIRS_PER_STEP = 8
DMA_ISSUE_UNROLL = 8
SCAN_CHUNK = 64
V7X_VMEM_LIMIT = 56 * 1024 * 1024


def _params(*sem, vmem=V7X_VMEM_LIMIT):
    return pltpu.CompilerParams(dimension_semantics=sem, vmem_limit_bytes=vmem)


def _layer_norm(y, g, b):
    mu = jnp.mean(y, axis=-1, keepdims=True)
    c = y - mu
    var = jnp.mean(c * c, axis=-1, keepdims=True)
    return c * lax.rsqrt(var + LN_EPS) * g + b


def _split2(x):
    hi = x.astype(BF16)
    lo = (x - hi.astype(F32)).astype(BF16)
    return hi, lo


def _dot(a, b):
    return jnp.dot(a, b, preferred_element_type=F32)


def _dot_nt(a, b):
    return lax.dot_general(a, b, (((1,), (1,)), ((), ())), preferred_element_type=F32)


def _dot_tn(a, b):
    return lax.dot_general(a, b, (((0,), (0,)), ((), ())), preferred_element_type=F32)


def _dot3(a, b):
    ah, al = _split2(a)
    bh, bl = _split2(b)
    return _dot(jnp.concatenate([ah, ah, al], axis=1), jnp.concatenate([bh, bl, bh], axis=0))


def _dot2(a, b):
    ah, al = _split2(a)
    bb = b.astype(BF16)
    return _dot(jnp.concatenate([ah, al], axis=1), jnp.concatenate([bb, bb], axis=0))


def _dot2_nt(a, b):
    ah, al = _split2(a)
    bb = b.astype(BF16)
    return _dot_nt(jnp.concatenate([ah, al], axis=1), jnp.concatenate([bb, bb], axis=1))


def _dot2_tn(a, b):
    ah, al = _split2(a)
    bb = b.astype(BF16)
    return _dot_tn(jnp.concatenate([ah, al], axis=0), jnp.concatenate([bb, bb], axis=0))


def _dot2_exact_rhs(a, b_bf16):
    ah, al = _split2(a)
    return _dot(jnp.concatenate([ah, al], axis=1), jnp.concatenate([b_bf16, b_bf16], axis=0))


def _head_ones(n):
    r = lax.broadcasted_iota(jnp.int32, (n, n), 0) >> 6
    c = lax.broadcasted_iota(jnp.int32, (n, n), 1) >> 6
    return (r == c).astype(BF16)


def _head_sum(x, ones_pair):
    d = x.shape[1]
    cols = [_dot2_exact_rhs(x[:, c:c + PAIR], ones_pair) for c in range(0, d, PAIR)]
    return jnp.concatenate(cols, axis=1)


def _qkv_kernel(x_ref, wq_ref, wk_ref, wv_ref, qb_ref, kb_ref, vb_ref, kf_ref, vf_ref, *, n_heads):
    xb = x_ref[0].astype(BF16)
    qb_ref[0] = (_dot(xb, wq_ref[...]) * Q_SCALE).astype(BF16)
    for w_ref, b_ref, f_ref in ((wk_ref, kb_ref, kf_ref), (wv_ref, vb_ref, vf_ref)):
        y = _dot(xb, w_ref[...])
        b_ref[0] = y.astype(BF16)
        for h in range(n_heads):
            f_ref[0, h] = y[:, h * HEAD_DIM:(h + 1) * HEAD_DIM]


def _qkv_proj(x, w_bf16, tm):
    b, t, d = x.shape
    h = d // HEAD_DIM
    row = pl.BlockSpec((1, tm, d), lambda bi, ti: (bi, ti, 0))
    hspec = pl.BlockSpec((1, h, tm, HEAD_DIM), lambda bi, ti: (bi, 0, ti, 0))
    shp = (b, h, t, HEAD_DIM)
    return pl.pallas_call(
        functools.partial(_qkv_kernel, n_heads=h),
        grid=(b, t // tm),
        in_specs=[row] + [pl.BlockSpec((d, d), lambda bi, ti, n=n: (0, n)) for n in range(3)],
        out_specs=[row] * 3 + [hspec] * 2,
        out_shape=[jax.ShapeDtypeStruct((b, t, d), BF16)] * 3 + [jax.ShapeDtypeStruct(shp, F32)] * 2,
        compiler_params=_params("parallel", "parallel"),
        name="qkv_proj",
    )(x, w_bf16, w_bf16, w_bf16)


def _aligned(x, m):
    return x if isinstance(x, int) else pl.multiple_of(x, m)


ATTN_STAGES = 5
ATTN_QUERY_BLOCK = 128
MASKED_SCORE = -1.0e30


DEAD_LOG2 = -160.0
F_Q0, F_K0, F_BIAS, F_FIRST, F_REAL, F_QB, F_NEXT, F_LANE = range(8)
ATTN_MAX_LANES = 4


def _attn_schedule(nq, per_q, past_blocks, bq, bk):
    n_lanes = 1
    while n_lanes * 2 <= min(ATTN_MAX_LANES, nq):
        n_lanes *= 2
    items = [[0, 0, 0, 1, 0, -1, 0, lane] for lane in range(n_lanes)]
    lane_start, lane_end = [], []
    for lane in range(n_lanes):
        lane_start.append(len(items))
        mine = [qb for qb in range(nq)
                if (qb % (2 * n_lanes) if qb % (2 * n_lanes) < n_lanes else 2 * n_lanes - 1 - qb % (2 * n_lanes)) == lane]
        for qb in mine:
            n_full = past_blocks + qb * per_q
            blocks = [(n_full + j, j + 1) for j in reversed(range(per_q))] + [(kb, 0) for kb in reversed(range(n_full))]
            nxt = len(items) + len(blocks)
            for i, (kb, bias_id) in enumerate(blocks):
                items.append([qb * bq, kb * bk, bias_id, int(i == 0), 1, qb, nxt, lane])
        lane_end.append(len(items))
    return items, n_lanes, lane_start, lane_end


def _sb_attn_kernel(tab_ref, q_ref, k_ref, v_ref, o_ref, z_scr, lb_scr, x2_scr, sums_scr, a_scr, later_scr,
                    acc_scr, bias_scr, suffix_scr, cur_ref, dead_ref, qprev_ref, valid_ref,
                    *, bq, bk, tab_len, n_lanes, lane_start, lane_end):
    per_q = bq // bk
    for ln in range(n_lanes):
        cur_ref[ln] = lane_start[ln]
        dead_ref[ln] = -1
        qprev_ref[ln] = 0
        valid_ref[ln] = 0
    r = lax.broadcasted_iota(jnp.int32, (2 * bk, 2 * bk), 0)
    c = lax.broadcasted_iota(jnp.int32, (2 * bk, 2 * bk), 1)
    r = jnp.where(r >= bk, r - bk, r)
    suffix_scr[...] = ((c >= bk) | (r > c)).astype(BF16)
    qi = lax.broadcasted_iota(jnp.int32, (bq, bk), 0)
    kj = lax.broadcasted_iota(jnp.int32, (bq, bk), 1)
    bias_scr[0] = jnp.zeros((bq, bk), F32)
    for j in range(per_q):
        bias_scr[j + 1] = jnp.where((kj + j * bk) < qi, 0.0, MASKED_SCORE)
    for scr in (z_scr, lb_scr, x2_scr, sums_scr, a_scr, later_scr, acc_scr):
        scr[...] = jnp.zeros_like(scr)
    lane_a = lax.broadcasted_iota(jnp.int32, (bk, PAIR), 1) < HEAD_DIM

    def field(f, i):
        return tab_ref[f * tab_len + i]

    def stack_heads(blk):
        zero = jnp.zeros_like(blk)
        return jnp.concatenate([jnp.where(lane_a, blk, zero), jnp.where(lane_a, zero, blk)], axis=0)

    def keep_going(c):
        return c[5] < ATTN_STAGES + n_lanes

    def body(c):
        trip, i1, i2, i3, i4, idle = c
        lane = trip & (n_lanes - 1)
        end = jnp.int32(lane_end[0])
        for ln in range(1, n_lanes):
            end = jnp.where(lane == ln, lane_end[ln], end)
        nxt = cur_ref[lane]
        peek = jnp.minimum(nxt, tab_len - 1)
        nxt = jnp.where((nxt < end) & (field(F_QB, peek) == dead_ref[lane]), field(F_NEXT, peek), nxt)
        more = nxt < end
        i0 = jnp.where(more, nxt, lane)
        cur_ref[lane] = jnp.where(more, nxt + 1, nxt)
        idle = jnp.where(more, 0, idle + 1)
        first5 = field(F_FIRST, i4) == 1
        lane5 = field(F_LANE, i4)

        @pl.when(first5 & (valid_ref[lane5] == 1))
        def _():
            o_ref[0, pl.ds(pl.multiple_of(qprev_ref[lane5], bq), bq), :] = acc_scr[lane5].astype(BF16)

        qprev_ref[lane5] = field(F_Q0, i4)
        valid_ref[lane5] = field(F_REAL, i4)
        k5 = pl.multiple_of(field(F_K0, i4), bk)
        contrib = _dot(a_scr[...], stack_heads(v_ref[0, pl.ds(k5, bk), :]))
        acc_scr[lane5] = jnp.where(first5, contrib, acc_scr[lane5] + contrib)
        slot = trip & 1
        restart = field(F_FIRST, i3) == 1
        lane4 = field(F_LANE, i3)
        lowest = None
        for h in range(2):
            cols = slice(h * bk, (h + 1) * bk)
            later = jnp.where(restart, 0.0, later_scr[lane4, :, cols])
            between = sums_scr[h, :, :bk] + later
            later = later + sums_scr[h, :, bk:]
            later_scr[lane4, :, cols] = later
            lowest = later if lowest is None else jnp.maximum(lowest, later)
            a_scr[:, cols] = jnp.exp2(lb_scr[slot, :, cols] + between).astype(BF16)
        dead_ref[lane4] = jnp.where(jnp.max(lowest) < DEAD_LOG2, field(F_QB, i3), -1)
        for h in range(2):
            sums_scr[h] = _dot(x2_scr[h], suffix_scr[...])
        bias = bias_scr[field(F_BIAS, i1)]
        for h in range(2):
            cols = slice(h * bk, (h + 1) * bk)
            z = z_scr[:, cols] + bias
            soft = jnp.log2(1.0 + jnp.exp2(-jnp.abs(z)))
            log_beta = jnp.minimum(z, 0.0) - soft
            hi, lo = _split2(log_beta - z)
            lb_scr[slot, :, cols] = log_beta
            x2_scr[h] = jnp.concatenate([hi, lo], axis=1)
        q1 = pl.multiple_of(field(F_Q0, i0), bq)
        k1 = pl.multiple_of(field(F_K0, i0), bk)
        z_scr[...] = _dot_nt(q_ref[0, pl.ds(q1, bq), :], stack_heads(k_ref[0, pl.ds(k1, bk), :]))
        return (trip + 1, i0, i1, i2, i3, idle)

    zero = jnp.int32(0)
    lax.while_loop(keep_going, body, (zero, zero, zero, zero, zero, zero))


def _sb_attention(q, k, v, bq, bk, q_start):
    b, tq, d = q.shape
    tk = k.shape[1]
    assert tq % bq == 0 and bq % bk == 0 and q_start % bk == 0 and tk == q_start + tq
    per_q = bq // bk
    items, n_lanes, lane_start, lane_end = _attn_schedule(tq // bq, per_q, q_start // bk, bq, bk)
    lane_smem = pltpu.SMEM((n_lanes,), jnp.int32)
    table = jnp.asarray(np.asarray(items, np.int32).T.reshape(-1))
    qspec = pl.BlockSpec((1, tq, PAIR), lambda bi, pi, tab: (bi, 0, pi))
    kspec = pl.BlockSpec((1, tk, PAIR), lambda bi, pi, tab: (bi, 0, pi))
    grid_spec = pltpu.PrefetchScalarGridSpec(
        num_scalar_prefetch=1,
        grid=(b, d // PAIR),
        in_specs=[qspec, kspec, kspec],
        out_specs=qspec,
        scratch_shapes=[pltpu.VMEM((bq, 2 * bk), F32), pltpu.VMEM((2, bq, 2 * bk), F32),
                        pltpu.VMEM((2, bq, 2 * bk), BF16), pltpu.VMEM((2, bq, 2 * bk), F32),
                        pltpu.VMEM((bq, 2 * bk), BF16), pltpu.VMEM((n_lanes, bq, 2 * bk), F32),
                        pltpu.VMEM((n_lanes, bq, PAIR), F32), pltpu.VMEM((per_q + 1, bq, bk), F32),
                        pltpu.VMEM((2 * bk, 2 * bk), BF16), lane_smem, lane_smem, lane_smem, lane_smem],
    )
    return pl.pallas_call(
        functools.partial(_sb_attn_kernel, bq=bq, bk=bk, tab_len=len(items), n_lanes=n_lanes,
                          lane_start=tuple(lane_start), lane_end=tuple(lane_end)),
        grid_spec=grid_spec,
        out_shape=jax.ShapeDtypeStruct((b, tq, d), BF16),
        compiler_params=_params("parallel", "parallel"),
        name="sb_attention",
    )(table, q, k, v)


def _mm_res_ln_kernel(a_ref, w_ref, x_ref, g_ref, b_ref, o_ref):
    y = _dot(a_ref[...], w_ref[...])
    o_ref[...] = _layer_norm(DEEPNORM_ALPHA * x_ref[...] + y, g_ref[...], b_ref[...])


def _mm_res_ln(a, w_bf16, x, g, b, tm):
    t, d = x.shape
    kdim = a.shape[1]
    vec = pl.BlockSpec((1, d), lambda i: (0, 0))
    return pl.pallas_call(
        _mm_res_ln_kernel,
        grid=(t // tm,),
        in_specs=[pl.BlockSpec((tm, kdim), lambda i: (i, 0)),
                  pl.BlockSpec((kdim, d), lambda i: (0, 0)),
                  pl.BlockSpec((tm, d), lambda i: (i, 0)), vec, vec],
        out_specs=pl.BlockSpec((tm, d), lambda i: (i, 0)),
        out_shape=jax.ShapeDtypeStruct((t, d), F32),
        compiler_params=_params("parallel"),
        name="attn_out_ln",
    )(a, w_bf16, x, g.reshape(1, d), b.reshape(1, d))


def _swiglu_step(xb, wg_ref, wu_ref, wd_ref):
    gate = _dot(xb, wg_ref[...])
    up = _dot(xb, wu_ref[...])
    hidden = gate * jax.nn.sigmoid(gate) * up
    return _dot(hidden.astype(BF16), wd_ref[...])


def _ffn_ln_kernel(x_ref, wg_ref, wu_ref, wd_ref, g_ref, b_ref, o_ref, xb_scr, acc_scr):
    j = pl.program_id(1)

    @pl.when(j == 0)
    def _():
        xb_scr[...] = x_ref[...].astype(BF16)
        acc_scr[...] = jnp.zeros_like(acc_scr)

    acc_scr[...] += _swiglu_step(xb_scr[...], wg_ref, wu_ref, wd_ref)

    @pl.when(j == pl.num_programs(1) - 1)
    def _():
        o_ref[...] = _layer_norm(DEEPNORM_ALPHA * x_ref[...] + acc_scr[...], g_ref[...], b_ref[...])


def _ffn_ln(x, w_gate_up_bf16, w_down_bf16, g, b, tm, tn):
    t, d = x.shape
    f = w_down_bf16.shape[0]
    nj = f // tn
    vec = pl.BlockSpec((1, d), lambda i, j: (0, 0))
    return pl.pallas_call(
        _ffn_ln_kernel,
        grid=(t // tm, nj),
        in_specs=[pl.BlockSpec((tm, d), lambda i, j: (i, 0)),
                  pl.BlockSpec((d, tn), lambda i, j: (0, j)),
                  pl.BlockSpec((d, tn), lambda i, j: (0, nj + j)),
                  pl.BlockSpec((tn, d), lambda i, j: (j, 0)), vec, vec],
        out_specs=pl.BlockSpec((tm, d), lambda i, j: (i, 0)),
        out_shape=jax.ShapeDtypeStruct((t, d), F32),
        scratch_shapes=[pltpu.VMEM((tm, d), BF16), pltpu.VMEM((tm, d), F32)],
        compiler_params=_params("parallel", "arbitrary"),
        name="dense_swiglu_ln",
    )(x, w_gate_up_bf16, w_gate_up_bf16, w_down_bf16, g.reshape(1, d), b.reshape(1, d))


def _moe_ffn_kernel(te_ref, nt_ref, x_ref, wg_ref, wu_ref, wd_ref, o_ref, xb_scr, acc_scr):
    i = pl.program_id(0)
    j = pl.program_id(1)
    used = i < nt_ref[0]

    @pl.when(used & (j == 0))
    def _():
        xb_scr[...] = x_ref[...].astype(BF16)
        acc_scr[...] = jnp.zeros_like(acc_scr)

    @pl.when(used)
    def _():
        acc_scr[...] += _swiglu_step(xb_scr[...], wg_ref.at[0], wu_ref.at[0], wd_ref.at[0])

    @pl.when(j == pl.num_programs(1) - 1)
    def _():
        o_ref[...] = jnp.where(used, acc_scr[...], 0.0)


def _moe_ffn(tile_expert, n_tiles_used, xg, w_gate_up_bf16, w_down_bf16, tm, tn):
    n, d = xg.shape
    f = w_down_bf16.shape[1]
    nj = f // tn
    grid_spec = pltpu.PrefetchScalarGridSpec(
        num_scalar_prefetch=2,
        grid=(n // tm, nj),
        in_specs=[pl.BlockSpec((tm, d), lambda i, j, te, nt: (i, 0)),
                  pl.BlockSpec((1, d, tn), lambda i, j, te, nt: (te[i], 0, j)),
                  pl.BlockSpec((1, d, tn), lambda i, j, te, nt: (te[i], 0, nj + j)),
                  pl.BlockSpec((1, tn, d), lambda i, j, te, nt: (te[i], j, 0))],
        out_specs=pl.BlockSpec((tm, d), lambda i, j, te, nt: (i, 0)),
        scratch_shapes=[pltpu.VMEM((tm, d), BF16), pltpu.VMEM((tm, d), F32)],
    )
    return pl.pallas_call(
        _moe_ffn_kernel,
        grid_spec=grid_spec,
        out_shape=jax.ShapeDtypeStruct((n, d), F32),
        compiler_params=_params("parallel", "arbitrary"),
        name="expert_swiglu",
    )(tile_expert, n_tiles_used, xg, w_gate_up_bf16, w_gate_up_bf16, w_down_bf16)


def _row_copy(src_hbm, dst_ref, sem, src_row, dst_row):
    return pltpu.make_async_copy(src_hbm.at[pl.ds(src_row, 1)], dst_ref.at[pl.ds(dst_row, 1)], sem)


def _gather_rows_into(idx_ref, src_hbm, dst_ref, sem, n):
    def start(r, c):
        _row_copy(src_hbm, dst_ref, sem, idx_ref[0, 0, r], r).start()
        return c

    lax.fori_loop(0, n, start, 0, unroll=DMA_ISSUE_UNROLL)
    pltpu.make_async_copy(src_hbm.at[pl.ds(0, n)], dst_ref, sem).wait()


def _scatter_copy(x_ref, dst_hbm, sem, src_row, dst_row):
    return pltpu.make_async_copy(x_ref.at[pl.ds(src_row, 1)], dst_hbm.at[pl.ds(dst_row, 1)], sem)


def _scatter_kernel(p0_ref, p1_ref, x_ref, dst_in_hbm_unused, dst_hbm, sem, *, tm):
    def start(r, c):
        _scatter_copy(x_ref, dst_hbm, sem, r, p0_ref[0, 0, r]).start()
        _scatter_copy(x_ref, dst_hbm, sem, r, p1_ref[0, 0, r]).start()
        return c

    lax.fori_loop(0, tm, start, 0, unroll=DMA_ISSUE_UNROLL)
    for _ in range(2):
        pltpu.make_async_copy(x_ref, dst_hbm.at[pl.ds(0, tm)], sem).wait()


def _scatter_rows(x, pos0, pos1, dst, tm):
    t, d = x.shape
    idx = pl.BlockSpec((1, 1, tm), lambda i: (i, 0, 0), memory_space=pltpu.SMEM)
    return pl.pallas_call(
        functools.partial(_scatter_kernel, tm=tm),
        grid=(t // tm,),
        in_specs=[idx, idx, pl.BlockSpec((tm, d), lambda i: (i, 0)), pl.BlockSpec(memory_space=pl.ANY)],
        out_specs=pl.BlockSpec(memory_space=pl.ANY),
        out_shape=jax.ShapeDtypeStruct(dst.shape, dst.dtype),
        input_output_aliases={3: 0},
        scratch_shapes=[pltpu.SemaphoreType.DMA(())],
        compiler_params=_params("arbitrary"),
        name="scatter_rows",
    )(pos0.reshape(-1, 1, tm), pos1.reshape(-1, 1, tm), x, dst)


def _combine_ln_kernel(p0_ref, p1_ref, x_ref, route_ref, y_hbm, g_ref, b_ref, op_ref, os_ref, a_scr, b_scr,
                       sem0, sem1, *, tm, n_prompt_tiles):
    _gather_rows_into(p0_ref, y_hbm, a_scr, sem0, tm)
    _gather_rows_into(p1_ref, y_hbm, b_scr, sem1, tm)
    moe = route_ref[:, 2:3] * a_scr[...] + route_ref[:, 3:4] * b_scr[...]
    out = _layer_norm(DEEPNORM_ALPHA * x_ref[...] + moe, g_ref[...], b_ref[...])
    i = pl.program_id(0)

    @pl.when(i < n_prompt_tiles)
    def _():
        op_ref[...] = out

    @pl.when(i >= n_prompt_tiles)
    def _():
        os_ref[...] = out


def _combine_ln(x, route, y_sorted, pos0, pos1, g, b, tm, n_prompt):
    t, d = x.shape
    npt = n_prompt // tm
    nst = (t - n_prompt) // tm
    idx = pl.BlockSpec((1, 1, tm), lambda i: (i, 0, 0), memory_space=pltpu.SMEM)
    vec = pl.BlockSpec((1, d), lambda i: (0, 0))
    return pl.pallas_call(
        functools.partial(_combine_ln_kernel, tm=tm, n_prompt_tiles=npt),
        grid=(npt + nst,),
        in_specs=[idx, idx, pl.BlockSpec((tm, d), lambda i: (i, 0)), pl.BlockSpec((tm, LANES), lambda i: (i, 0)),
                  pl.BlockSpec(memory_space=pl.ANY), vec, vec],
        out_specs=[pl.BlockSpec((tm, d), lambda i: (jnp.minimum(i, npt - 1), 0)),
                   pl.BlockSpec((tm, d), lambda i: (jnp.maximum(i - npt, 0), 0))],
        out_shape=[jax.ShapeDtypeStruct((n_prompt, d), F32), jax.ShapeDtypeStruct((t - n_prompt, d), F32)],
        scratch_shapes=[pltpu.VMEM((tm, d), F32), pltpu.VMEM((tm, d), F32),
                        pltpu.SemaphoreType.DMA(()), pltpu.SemaphoreType.DMA(())],
        compiler_params=_params("arbitrary"),
        name="moe_combine_ln",
    )(pos0.reshape(-1, 1, tm), pos1.reshape(-1, 1, tm), x, route, y_sorted, g.reshape(1, d), b.reshape(1, d))


def _rwkv_proj_kernel(x_ref, prev_ref, mu_ref, wr_ref, wk_ref, wv_ref, w1_ref, w2_ref, a1_ref, a2_ref,
                      g1_ref, g2_ref, vec_ref,
                      r_ref, lw_ref, k_ref, v_ref, kk_ref, b_ref, g_ref, bonus_ref, *, seg):
    x = x_ref[...]
    tm, d = x.shape
    rolled = pltpu.roll(x, 1, 0).reshape(tm // seg, seg, d)
    first = lax.broadcasted_iota(jnp.int32, (tm // seg, seg, d), 1) == 0
    x_prev = jnp.where(first, prev_ref[...], rolled).reshape(tm, d)
    xx = x_prev - x

    def mixed(i):
        return (x + xx * mu_ref[i:i + 1, :]).astype(BF16)

    r = _dot(mixed(0), wr_ref[...])
    k = _dot(mixed(2), wk_ref[...])
    v = _dot(mixed(3), wv_ref[...])
    w0, a0, k_k, k_a, r_k = (vec_ref[i:i + 1, :] for i in range(5))
    wl = w0 + _dot(jnp.tanh(_dot(mixed(1), w1_ref[...])).astype(BF16), w2_ref[...])
    softplus_neg = jnp.maximum(-wl, 0.0) + jnp.log(1.0 + jnp.exp(-jnp.abs(wl)))
    lw_ref[...] = -jnp.exp(-softplus_neg - 0.5)
    a = jax.nn.sigmoid(a0 + _dot(_dot(mixed(4), a1_ref[...]).astype(BF16), a2_ref[...]))
    g_ref[...] = _dot(jax.nn.sigmoid(_dot(mixed(5), g1_ref[...])).astype(BF16), g2_ref[...])
    ones_pair = _head_ones(PAIR)
    kk = k * k_k
    kk = kk / jnp.maximum(jnp.sqrt(_head_sum(kk * kk, ones_pair)), 1e-12)
    k = k * (1.0 + (a - 1.0) * k_a)
    r_ref[...] = r
    k_ref[...] = k
    v_ref[...] = v
    kk_ref[...] = kk
    b_ref[...] = kk * a
    bonus_ref[...] = _head_sum(r * k * r_k, ones_pair) * v


def _rwkv_proj(x, prev_rows, mu, w_r, w_k, w_v, w1, w2, a1, a2, g1, g2, vecs, tm, seg):
    t, d = x.shape
    row = pl.BlockSpec((tm, d), lambda i: (i, 0))

    def full(a):
        return pl.BlockSpec(a.shape, lambda i: (0,) * a.ndim)

    consts = (mu, w_r, w_k, w_v, w1, w2, a1, a2, g1, g2, vecs)
    return pl.pallas_call(
        functools.partial(_rwkv_proj_kernel, seg=seg),
        grid=(t // tm,),
        in_specs=[row, pl.BlockSpec((tm // seg, 1, d), lambda i: (i, 0, 0))] + [full(c) for c in consts],
        out_specs=[row] * 8,
        out_shape=[jax.ShapeDtypeStruct((t, d), F32)] * 8,
        compiler_params=_params("parallel"),
        name="rwkv_proj",
    )(x, prev_rows, *consts)


def _unit_lower_inverses(lows, limit):
    n = lows[0].shape[0]
    r = lax.broadcasted_iota(jnp.int32, (n, n), 0)
    c = lax.broadcasted_iota(jnp.int32, (n, n), 1)
    eye = (r == c).astype(F32)
    base = (r >> 3) == (c >> 3)
    ps = [jnp.where(base, -low, 0.0) for low in lows]
    invs = [eye + p for p in ps]
    for _ in range(2):
        ps = [_dot2(p, p) for p in ps]
        invs = [inv + _dot2(inv, p) for inv, p in zip(invs, ps)]
    shift = 3
    while (1 << shift) < limit:
        sel = ((r >> (shift + 1)) == (c >> (shift + 1))) & ((r >> shift) != (c >> shift))
        mids = [_dot2(inv, jnp.where(sel, low, 0.0)) for inv, low in zip(invs, lows)]
        invs = [inv - _dot2(mid, inv) for inv, mid in zip(invs, mids)]
        shift += 1
    return invs


def _rwkv_scan_kernel(r_ref, lw_ref, k_ref, v_ref, kk_ref, b_ref, h0_ref, o_ref, hT_ref, h_scr, *, n_chunks, n_pairs):
    cs = SCAN_CHUNK
    tb = pl.program_id(2)

    @pl.when(tb == 0)
    def _():
        h_scr[...] = h0_ref[0]

    lane = lax.broadcasted_iota(jnp.int32, (cs, PAIR), 1)
    row = lax.broadcasted_iota(jnp.int32, (cs, PAIR), 0)
    head_a = lane < HEAD_DIM
    strict = (lane & (HEAD_DIM - 1)) < row
    incl = (lane & (HEAD_DIM - 1)) <= row
    tri = (lax.broadcasted_iota(jnp.int32, (cs, cs), 1) <= lax.broadcasted_iota(jnp.int32, (cs, cs), 0)).astype(BF16)
    pr = lax.broadcasted_iota(jnp.int32, (PAIR, PAIR), 0) >> 6
    pc = lax.broadcasted_iota(jnp.int32, (PAIR, PAIR), 1) >> 6
    same_head = pr == pc

    def stack(x):
        return jnp.concatenate([jnp.where(head_a, x, 0.0), jnp.where(head_a, 0.0, x)], axis=0)

    def each(fn, *lists):
        return [fn(*xs) for xs in zip(*lists)]

    def cumulative(lw):
        l1 = lw.astype(BF16)
        rem = lw - l1.astype(F32)
        l2 = rem.astype(BF16)
        l3 = (rem - l2.astype(F32)).astype(BF16)
        parts = _dot(tri, jnp.concatenate([l1, l2, l3], axis=1))
        return parts[:, :PAIR] + parts[:, PAIR:2 * PAIR] + parts[:, 2 * PAIR:]

    def chunk(ci, carry):
        sl = pl.ds(pl.multiple_of(ci * cs, cs), cs)
        lanes = [slice(p * PAIR, (p + 1) * PAIR) for p in range(n_pairs)]
        r, lw, k, v, kk, b = ([ref[0, sl, ln] for ln in lanes] for ref in (r_ref, lw_ref, k_ref, v_ref, kk_ref, b_ref))
        h = [h_scr[p] for p in range(n_pairs)]
        cum = each(cumulative, lw)
        total = [c[cs - 1:cs, :] for c in cum]
        r_hat = each(lambda x, c: x * jnp.exp(c), r, cum)
        kk_hat = each(lambda x, c, l: x * jnp.exp(c - l), kk, cum, lw)
        k_hat = each(lambda x, c: x * jnp.exp(-c), k, cum)
        b_hat = each(lambda x, c: x * jnp.exp(-c), b, cum)
        gram = each(lambda kh, rh, bh, k2: _dot2_nt(jnp.concatenate([kh, rh], axis=0),
                                                    jnp.concatenate([stack(bh), stack(k2)], axis=0)),
                    kk_hat, r_hat, b_hat, k_hat)
        l_b = [jnp.where(strict, g[:cs, :PAIR], 0.0) for g in gram]
        l_k = [jnp.where(strict, g[:cs, PAIR:], 0.0) for g in gram]
        a_rb = [jnp.where(incl, g[cs:, :PAIR], 0.0) for g in gram]
        a_rk = [jnp.where(incl, g[cs:, PAIR:], 0.0) for g in gram]
        v_st = each(stack, v)
        w = each(lambda kh, lk, hh, vs: _dot2(jnp.concatenate([kh, lk], axis=1), jnp.concatenate([hh, vs], axis=0)),
                 kk_hat, l_k, h, v_st)
        inv = _unit_lower_inverses(each(stack, l_b), cs)
        u_st = each(lambda t_, w_: _dot2(t_, stack(w_)), inv, w)
        o = each(lambda rh, ak, ab, hh, vs, us: _dot2(jnp.concatenate([rh, ak, -ab], axis=1),
                                                      jnp.concatenate([hh, vs, us], axis=0)),
                 r_hat, a_rk, a_rb, h, v_st, u_st)
        for ln, o_p in zip(lanes, o):
            o_ref[0, sl, ln] = o_p
        u = [us[:cs] + us[cs:] for us in u_st]
        to_end = each(lambda t_, c: jnp.exp(t_ - c), total, cum)
        upd = each(lambda k_, b_, e_, v_, u_: _dot2_tn(jnp.concatenate([k_ * e_, -(b_ * e_)], axis=0),
                                                       jnp.concatenate([v_, u_], axis=0)),
                   k, b, to_end, v, u)
        for p in range(n_pairs):
            decay_col = jnp.transpose(jnp.broadcast_to(jnp.exp(total[p]), (PAIR, PAIR)))
            h_scr[p] = decay_col * h[p] + jnp.where(same_head, upd[p], 0.0)
        return carry

    lax.fori_loop(0, n_chunks, chunk, 0)

    @pl.when(tb == pl.num_programs(2) - 1)
    def _():
        hT_ref[0] = h_scr[...]


def _rwkv_scan(r, lw, k, v, kk, b, h0, tb):
    bsz, t, d = r.shape
    npair = d // PAIR
    group = math.gcd(npair, SCAN_PAIRS_PER_STEP)
    seq = pl.BlockSpec((1, tb, group * PAIR), lambda bi, pi, ti: (bi, ti, pi))
    st = pl.BlockSpec((1, group, PAIR, PAIR), lambda bi, pi, ti: (bi, pi, 0, 0))
    return pl.pallas_call(
        functools.partial(_rwkv_scan_kernel, n_chunks=tb // SCAN_CHUNK, n_pairs=group),
        grid=(bsz, npair // group, t // tb),
        in_specs=[seq] * 6 + [st],
        out_specs=[seq, st],
        out_shape=[jax.ShapeDtypeStruct((bsz, t, d), F32), jax.ShapeDtypeStruct((bsz, npair, PAIR, PAIR), F32)],
        scratch_shapes=[pltpu.VMEM((group, PAIR, PAIR), F32)],
        compiler_params=_params("parallel", "parallel", "arbitrary"),
        name="rwkv_scan",
    )(r, lw, k, v, kk, b, h0)


def _rwkv_out_kernel(o_ref, g_ref, bonus_ref, x_ref, wo_ref, vec_ref, dst_hbm_unused, out_ref):
    o = o_ref[...]
    ones_pair = _head_ones(PAIR)
    mean = _head_sum(o, ones_pair) * (1.0 / HEAD_DIM)
    c = o - mean
    var = _head_sum(c * c, ones_pair) * (1.0 / HEAD_DIM)
    on = c * lax.rsqrt(var + GN_EPS) * vec_ref[0:1, :] + vec_ref[1:2, :]
    y = _dot(((on + bonus_ref[...]) * g_ref[...]).astype(BF16), wo_ref[...])
    out_ref[...] = _layer_norm(DEEPNORM_ALPHA * x_ref[...] + y, vec_ref[2:3, :], vec_ref[3:4, :])


def _rwkv_out(o, g, bonus, x, w_o_bf16, vecs, dst, n_rows, row_offset, tm):
    t, d = x.shape
    row = pl.BlockSpec((tm, d), lambda i: (i, 0))
    off = row_offset // tm
    return pl.pallas_call(
        _rwkv_out_kernel,
        grid=(t // tm,),
        in_specs=[row, row, row, row, pl.BlockSpec((d, d), lambda i: (0, 0)),
                  pl.BlockSpec(vecs.shape, lambda i: (0, 0)), pl.BlockSpec(memory_space=pl.ANY)],
        out_specs=pl.BlockSpec((tm, d), lambda i: (off + i, 0)),
        out_shape=jax.ShapeDtypeStruct((n_rows, d), F32),
        input_output_aliases={6: 0},
        compiler_params=_params("parallel"),
        name="rwkv_out_ln",
    )(o, g, bonus, x, w_o_bf16, vecs, dst)


def _router_kernel(x_ref, w_ref, o_ref, *, n_experts):
    logits = _dot3(x_ref[...], w_ref[...])
    lane = lax.broadcasted_iota(jnp.int32, logits.shape, 1)
    lane_f = lane.astype(F32)
    lowest = jnp.float32(-3.0e38)
    logits = jnp.where(lane < n_experts, logits, lowest)
    m1 = jnp.max(logits, axis=1, keepdims=True)
    i1 = jnp.min(jnp.where(logits == m1, lane_f, float(LANES)), axis=1, keepdims=True)
    rest = jnp.where(lane_f == i1, lowest, logits)
    m2 = jnp.max(rest, axis=1, keepdims=True)
    i2 = jnp.min(jnp.where(rest == m2, lane_f, float(LANES)), axis=1, keepdims=True)
    e = jnp.exp(m2 - m1)
    g1 = 1.0 / (1.0 + e)
    g2 = e * g1
    out = jnp.where(lane == 0, i1, 0.0)
    out = jnp.where(lane == 1, i2, out)
    out = jnp.where(lane == 2, g1, out)
    o_ref[...] = jnp.where(lane == 3, g2, out)


def _router(x, w_router, tm):
    t, d = x.shape
    e = w_router.shape[1]
    w_pad = jnp.zeros((d, LANES), F32).at[:, :e].set(w_router)
    return pl.pallas_call(
        functools.partial(_router_kernel, n_experts=e),
        grid=(t // tm,),
        in_specs=[pl.BlockSpec((tm, d), lambda i: (i, 0)), pl.BlockSpec((d, LANES), lambda i: (0, 0))],
        out_specs=pl.BlockSpec((tm, LANES), lambda i: (i, 0)),
        out_shape=jax.ShapeDtypeStruct((t, LANES), F32),
        compiler_params=_params("parallel"),
        name="router_top2",
    )(x, w_pad)


def _ffn_chunk(f, cap=1536):
    best = LANES
    for tn in range(LANES, cap + 1, LANES):
        if f % tn == 0:
            best = tn
    return best


def _row_tile(t, cap):
    tm = cap
    while t % tm:
        tm //= 2
    return tm


def _pair_states(s):
    b, h, n, _ = s.shape
    st = jnp.swapaxes(s, 2, 3).reshape(b, h // 2, 2, n, n)
    z = jnp.zeros_like(st[:, :, 0])
    top = jnp.concatenate([st[:, :, 0], z], axis=-1)
    bot = jnp.concatenate([z, st[:, :, 1]], axis=-1)
    return jnp.concatenate([top, bot], axis=-2)


def _unpair_states(hp):
    b, p, _, _ = hp.shape
    n = HEAD_DIM
    st = jnp.stack([hp[:, :, :n, :n], hp[:, :, n:, n:]], axis=2).reshape(b, 2 * p, n, n)
    return jnp.swapaxes(st, 2, 3)


def _attention_layer(x, w_qkv, w_o, ln_g, ln_b, cache_k=None, cache_v=None):
    b, t, d = x.shape
    q, kb, vb, kf, vf = _qkv_proj(x, w_qkv, _row_tile(t, 512))
    if cache_k is None:
        ao = _sb_attention(q, kb, vb, _row_tile(t, ATTN_QUERY_BLOCK), LANES, 0)
    else:
        past = cache_k.shape[2]
        merge = lambda c: jnp.swapaxes(c.astype(BF16), 1, 2).reshape(b, past, d)
        keys = jnp.concatenate([merge(cache_k), kb], axis=1)
        vals = jnp.concatenate([merge(cache_v), vb], axis=1)
        ao = _sb_attention(q, keys, vals, t, t, past)
    x1 = _mm_res_ln(ao.reshape(b * t, d), w_o, x.reshape(b * t, d), ln_g, ln_b, _row_tile(b * t, 512))
    return x1, kf, vf


def _rwkv_layer(x, x_last, s0, p, dst, n_rows, row_offset):
    b, t, d = x.shape
    seg = SCAN_CHUNK
    xs = x.reshape(b, t // seg, seg, d)
    prev = jnp.concatenate([x_last[:, None, :], xs[:, :-1, seg - 1, :]], axis=1).reshape(b * t // seg, 1, d)
    xf = x.reshape(b * t, d)
    tm = _row_tile(b * t, 256)
    r, lw, k, v, kk, bb, g, bonus = _rwkv_proj(xf, prev, p["mu"], p["w_r"], p["w_k"], p["w_v"], p["w1"], p["w2"],
                                               p["a1"], p["a2"], p["g1"], p["g2"], p["proj_vecs"], tm, seg)
    shp = (b, t, d)
    o, h_t = _rwkv_scan(*(z.reshape(shp) for z in (r, lw, k, v, kk, bb)), _pair_states(s0), _row_tile(t, 512))
    dst = _rwkv_out(o.reshape(b * t, d), g, bonus, xf, p["w_o"], p["out_vecs"], dst, n_rows, row_offset, tm)
    return dst, _unpair_states(h_t)


def _moe_layer(x, n_prompt, w_router, w_gate_up, w_down, ln_g, ln_b, tm):
    t, d = x.shape
    e = w_router.shape[1]
    route = _router(x, w_router, tm)
    experts = jnp.concatenate([route[:, 0], route[:, 1]]).astype(jnp.int32)
    onehot = (experts[:, None] == jnp.arange(e, dtype=jnp.int32)[None, :]).astype(jnp.int32)
    counts = jnp.sum(onehot, axis=0)
    padded = ((counts + tm - 1) // tm) * tm
    ends = jnp.cumsum(padded)
    starts = ends - padded
    rank = jnp.sum(jnp.cumsum(onehot, axis=0) * onehot, axis=1) - 1
    pos = (jnp.sum(starts[None, :] * onehot, axis=1) + rank).astype(jnp.int32)
    n_rows = ((2 * t + e * (tm - 1)) // tm) * tm
    tile_start = jnp.arange(n_rows // tm, dtype=jnp.int32) * tm
    tile_expert = jnp.minimum(jnp.sum((tile_start[:, None] >= ends[None, :]).astype(jnp.int32), axis=1), e - 1)
    n_tiles_used = (ends[-1] // tm).astype(jnp.int32).reshape(1)
    xg = _scatter_rows(x, pos[:t], pos[t:], jnp.zeros((n_rows, d), F32), tm)
    y = _moe_ffn(tile_expert, n_tiles_used, xg, w_gate_up, w_down, tm, _ffn_chunk(w_down.shape[1], cap=2048))
    return _combine_ln(x, route, y, pos[:t], pos[t:], ln_g, ln_b, tm, n_prompt)


def kernel(x_prompt, x_sample, cache_k, cache_v, state_wkv, state_shift, att_w_qkv, att_w_o, ffn_w_gate_up, ffn_w_down, rwkv_mu, rwkv_w_rkv, rwkv_w0, rwkv_w1, rwkv_w2, rwkv_a0, rwkv_a1, rwkv_a2, rwkv_g1, rwkv_g2, rwkv_k_k, rwkv_k_a, rwkv_r_k, rwkv_gn_g, rwkv_gn_b, rwkv_w_o, moe_w_router, moe_w_gate_up, moe_w_down, ln_mix_g, ln_mix_b, ln_ffn_g, ln_ffn_b):
    bp, tp, d = x_prompt.shape
    bs, ts, _ = x_sample.shape
    n_prompt, n_sample = bp * tp, bs * ts
    bf = lambda a: a.astype(BF16)

    w_qkv, w_o = bf(att_w_qkv[0]), bf(att_w_o[0])
    xp1, k_p, v_p = _attention_layer(x_prompt, w_qkv, w_o, ln_mix_g[0], ln_mix_b[0])
    xs1, k_s, v_s = _attention_layer(x_sample, w_qkv, w_o, ln_mix_g[0], ln_mix_b[0], cache_k[0], cache_v[0])
    w_gu, w_dn = bf(ffn_w_gate_up[0]), bf(ffn_w_down[0])
    tn = _ffn_chunk(w_dn.shape[0])
    xp2 = _ffn_ln(xp1, w_gu, w_dn, ln_ffn_g[0], ln_ffn_b[0], _row_tile(n_prompt, 512), tn).reshape(bp, tp, d)
    xs2 = _ffn_ln(xs1, w_gu, w_dn, ln_ffn_g[0], ln_ffn_b[0], _row_tile(n_sample, 512), tn).reshape(bs, ts, d)

    p = {
        "mu": rwkv_mu[0], "w_r": bf(rwkv_w_rkv[0, 0]), "w_k": bf(rwkv_w_rkv[0, 1]), "w_v": bf(rwkv_w_rkv[0, 2]),
        "w1": bf(rwkv_w1[0]), "w2": bf(rwkv_w2[0]), "a1": bf(rwkv_a1[0]), "a2": bf(rwkv_a2[0]),
        "g1": bf(rwkv_g1[0]), "g2": bf(rwkv_g2[0]), "w_o": bf(rwkv_w_o[0]),
        "proj_vecs": jnp.stack([rwkv_w0[0], rwkv_a0[0], rwkv_k_k[0], rwkv_k_a[0], rwkv_r_k[0].reshape(d)]),
        "out_vecs": jnp.stack([rwkv_gn_g[0], rwkv_gn_b[0], ln_mix_g[1], ln_mix_b[1]]),
    }
    n_all = n_prompt + n_sample
    x3, wkv_p = _rwkv_layer(xp2, jnp.zeros((bp, d), F32), jnp.zeros((bp, d // HEAD_DIM, HEAD_DIM, HEAD_DIM), F32),
                            p, jnp.zeros((n_all, d), F32), n_all, 0)
    x3, wkv_s = _rwkv_layer(xs2, state_shift[0], state_wkv[0], p, x3, n_all, n_prompt)
    y_p, y_s = _moe_layer(x3, n_prompt, moe_w_router[0], bf(moe_w_gate_up[0]), bf(moe_w_down[0]),
                          ln_ffn_g[1], ln_ffn_b[1], _row_tile(n_sample, 512))
    return (y_p.reshape(bp, tp, d), y_s.reshape(bs, ts, d),
            k_p[None], v_p[None], wkv_p[None], xp2[:, -1][None],
            k_s[None], v_s[None], wkv_s[None], xs2[:, -1][None])
```

```python
import functools
import math

import jax
import jax.numpy as jnp
import numpy as np
from jax import lax
from jax.experimental import pallas as pl
from jax.experimental.pallas import tpu as pltpu

F32 = jnp.float32
BF16 = jnp.bfloat16

HEAD_DIM = 64
LANES = 128
PAIR = 2 * HEAD_DIM
LN_EPS = 1e-5
GN_EPS = 64e-5
DEEPNORM_ALPHA = 4.0 ** 0.25
Q_SCALE = HEAD_DIM ** -0.5 * 1.4426950408889634
SCAN_PAIRS_PER_STEP = 8
DMA_ISSUE_UNROLL = 8
SCAN_CHUNK = 64
V7X_VMEM_LIMIT = 56 * 1024 * 1024


def _params(*sem, vmem=V7X_VMEM_LIMIT):
    return pltpu.CompilerParams(dimension_semantics=sem, vmem_limit_bytes=vmem)


def _layer_norm(y, g, b):
    mu = jnp.mean(y, axis=-1, keepdims=True)
    c = y - mu
    var = jnp.mean(c * c, axis=-1, keepdims=True)
    return c * lax.rsqrt(var + LN_EPS) * g + b


def _split2(x):
    hi = x.astype(BF16)
    lo = (x - hi.astype(F32)).astype(BF16)
    return hi, lo


def _dot(a, b):
    return jnp.dot(a, b, preferred_element_type=F32)


def _dot_nt(a, b):
    return lax.dot_general(a, b, (((1,), (1,)), ((), ())), preferred_element_type=F32)


def _dot_tn(a, b):
    return lax.dot_general(a, b, (((0,), (0,)), ((), ())), preferred_element_type=F32)


def _dot3(a, b):
    ah, al = _split2(a)
    bh, bl = _split2(b)
    return _dot(jnp.concatenate([ah, ah, al], axis=1), jnp.concatenate([bh, bl, bh], axis=0))


def _dot2(a, b):
    ah, al = _split2(a)
    bb = b.astype(BF16)
    return _dot(jnp.concatenate([ah, al], axis=1), jnp.concatenate([bb, bb], axis=0))


def _dot2_nt(a, b):
    ah, al = _split2(a)
    bb = b.astype(BF16)
    return _dot_nt(jnp.concatenate([ah, al], axis=1), jnp.concatenate([bb, bb], axis=1))


def _dot2_tn(a, b):
    ah, al = _split2(a)
    bb = b.astype(BF16)
    return _dot_tn(jnp.concatenate([ah, al], axis=0), jnp.concatenate([bb, bb], axis=0))


def _dot2_exact_rhs(a, b_bf16):
    ah, al = _split2(a)
    return _dot(jnp.concatenate([ah, al], axis=1), jnp.concatenate([b_bf16, b_bf16], axis=0))


def _head_ones(n):
    r = lax.broadcasted_iota(jnp.int32, (n, n), 0) >> 6
    c = lax.broadcasted_iota(jnp.int32, (n, n), 1) >> 6
    return (r == c).astype(BF16)


def _head_sum(x, ones_pair):
    d = x.shape[1]
    cols = [_dot2_exact_rhs(x[:, c:c + PAIR], ones_pair) for c in range(0, d, PAIR)]
    return jnp.concatenate(cols, axis=1)


def _qkv_kernel(x_ref, wq_ref, wk_ref, wv_ref, qb_ref, kb_ref, vb_ref, kf_ref, vf_ref, *, n_heads):
    xb = x_ref[0].astype(BF16)
    qb_ref[0] = (_dot(xb, wq_ref[...]) * Q_SCALE).astype(BF16)
    for w_ref, b_ref, f_ref in ((wk_ref, kb_ref, kf_ref), (wv_ref, vb_ref, vf_ref)):
        y = _dot(xb, w_ref[...])
        b_ref[0] = y.astype(BF16)
        for h in range(n_heads):
            f_ref[0, h] = y[:, h * HEAD_DIM:(h + 1) * HEAD_DIM]


def _qkv_proj(x, w_bf16, tm):
    b, t, d = x.shape
    h = d // HEAD_DIM
    row = pl.BlockSpec((1, tm, d), lambda bi, ti: (bi, ti, 0))
    hspec = pl.BlockSpec((1, h, tm, HEAD_DIM), lambda bi, ti: (bi, 0, ti, 0))
    shp = (b, h, t, HEAD_DIM)
    return pl.pallas_call(
        functools.partial(_qkv_kernel, n_heads=h),
        grid=(b, t // tm),
        in_specs=[row] + [pl.BlockSpec((d, d), lambda bi, ti, n=n: (0, n)) for n in range(3)],
        out_specs=[row] * 3 + [hspec] * 2,
        out_shape=[jax.ShapeDtypeStruct((b, t, d), BF16)] * 3 + [jax.ShapeDtypeStruct(shp, F32)] * 2,
        compiler_params=_params("parallel", "parallel"),
        name="qkv_proj",
    )(x, w_bf16, w_bf16, w_bf16)


def _aligned(x, m):
    return x if isinstance(x, int) else pl.multiple_of(x, m)


ATTN_STAGES = 5
ATTN_QUERY_BLOCK = 128
MASKED_SCORE = -1.0e30


DEAD_LOG2 = -160.0
F_Q0, F_K0, F_BIAS, F_FIRST, F_REAL, F_QB, F_NEXT, F_LANE = range(8)
ATTN_MAX_LANES = 4


def _attn_schedule(nq, per_q, past_blocks, bq, bk):
    n_lanes = 1
    while n_lanes * 2 <= min(ATTN_MAX_LANES, nq):
        n_lanes *= 2
    items = [[0, 0, 0, 1, 0, -1, 0, lane] for lane in range(n_lanes)]
    lane_start, lane_end = [], []
    for lane in range(n_lanes):
        lane_start.append(len(items))
        mine = [qb for qb in range(nq)
                if (qb % (2 * n_lanes) if qb % (2 * n_lanes) < n_lanes else 2 * n_lanes - 1 - qb % (2 * n_lanes)) == lane]
        for qb in mine:
            n_full = past_blocks + qb * per_q
            blocks = [(n_full + j, j + 1) for j in reversed(range(per_q))] + [(kb, 0) for kb in reversed(range(n_full))]
            nxt = len(items) + len(blocks)
            for i, (kb, bias_id) in enumerate(blocks):
                items.append([qb * bq, kb * bk, bias_id, int(i == 0), 1, qb, nxt, lane])
        lane_end.append(len(items))
    return items, n_lanes, lane_start, lane_end


def _sb_attn_kernel(tab_ref, q_ref, k_ref, v_ref, o_ref, z_scr, lb_scr, x2_scr, sums_scr, a_scr, later_scr,
                    acc_scr, bias_scr, suffix_scr, cur_ref, dead_ref, qprev_ref, valid_ref,
                    *, bq, bk, tab_len, n_lanes, lane_start, lane_end):
    per_q = bq // bk
    for ln in range(n_lanes):
        cur_ref[ln] = lane_start[ln]
        dead_ref[ln] = -1
        qprev_ref[ln] = 0
        valid_ref[ln] = 0
    r = lax.broadcasted_iota(jnp.int32, (2 * bk, 2 * bk), 0)
    c = lax.broadcasted_iota(jnp.int32, (2 * bk, 2 * bk), 1)
    r = jnp.where(r >= bk, r - bk, r)
    suffix_scr[...] = ((c >= bk) | (r > c)).astype(BF16)
    qi = lax.broadcasted_iota(jnp.int32, (bq, bk), 0)
    kj = lax.broadcasted_iota(jnp.int32, (bq, bk), 1)
    bias_scr[0] = jnp.zeros((bq, bk), F32)
    for j in range(per_q):
        bias_scr[j + 1] = jnp.where((kj + j * bk) < qi, 0.0, MASKED_SCORE)
    for scr in (z_scr, lb_scr, x2_scr, sums_scr, a_scr, later_scr, acc_scr):
        scr[...] = jnp.zeros_like(scr)
    lane_a = lax.broadcasted_iota(jnp.int32, (bk, PAIR), 1) < HEAD_DIM

    def field(f, i):
        return tab_ref[f * tab_len + i]

    def stack_heads(blk):
        zero = jnp.zeros_like(blk)
        return jnp.concatenate([jnp.where(lane_a, blk, zero), jnp.where(lane_a, zero, blk)], axis=0)

    def keep_going(c):
        return c[5] < ATTN_STAGES + n_lanes

    def body(c):
        trip, i1, i2, i3, i4, idle = c
        lane = trip & (n_lanes - 1)
        end = jnp.int32(lane_end[0])
        for ln in range(1, n_lanes):
            end = jnp.where(lane == ln, lane_end[ln], end)
        nxt = cur_ref[lane]
        peek = jnp.minimum(nxt, tab_len - 1)
        nxt = jnp.where((nxt < end) & (field(F_QB, peek) == dead_ref[lane]), field(F_NEXT, peek), nxt)
        more = nxt < end
        i0 = jnp.where(more, nxt, lane)
        cur_ref[lane] = jnp.where(more, nxt + 1, nxt)
        idle = jnp.where(more, 0, idle + 1)
        first5 = field(F_FIRST, i4) == 1
        lane5 = field(F_LANE, i4)

        @pl.when(first5 & (valid_ref[lane5] == 1))
        def _():
            o_ref[0, pl.ds(pl.multiple_of(qprev_ref[lane5], bq), bq), :] = acc_scr[lane5].astype(BF16)

        qprev_ref[lane5] = field(F_Q0, i4)
        valid_ref[lane5] = field(F_REAL, i4)
        k5 = pl.multiple_of(field(F_K0, i4), bk)
        contrib = _dot(a_scr[...], stack_heads(v_ref[0, pl.ds(k5, bk), :]))
        acc_scr[lane5] = jnp.where(first5, contrib, acc_scr[lane5] + contrib)
        slot = trip & 1
        restart = field(F_FIRST, i3) == 1
        lane4 = field(F_LANE, i3)
        lowest = None
        for h in range(2):
            cols = slice(h * bk, (h + 1) * bk)
            later = jnp.where(restart, 0.0, later_scr[lane4, :, cols])
            between = sums_scr[h, :, :bk] + later
            later = later + sums_scr[h, :, bk:]
            later_scr[lane4, :, cols] = later
            lowest = later if lowest is None else jnp.maximum(lowest, later)
            a_scr[:, cols] = jnp.exp2(lb_scr[slot, :, cols] + between).astype(BF16)
        dead_ref[lane4] = jnp.where(jnp.max(lowest) < DEAD_LOG2, field(F_QB, i3), -1)
        for h in range(2):
            sums_scr[h] = _dot(x2_scr[h], suffix_scr[...])
        bias = bias_scr[field(F_BIAS, i1)]
        for h in range(2):
            cols = slice(h * bk, (h + 1) * bk)
            z = z_scr[:, cols] + bias
            soft = jnp.log2(1.0 + jnp.exp2(-jnp.abs(z)))
            log_beta = jnp.minimum(z, 0.0) - soft
            hi, lo = _split2(log_beta - z)
            lb_scr[slot, :, cols] = log_beta
            x2_scr[h] = jnp.concatenate([hi, lo], axis=1)
        q1 = pl.multiple_of(field(F_Q0, i0), bq)
        k1 = pl.multiple_of(field(F_K0, i0), bk)
        z_scr[...] = _dot_nt(q_ref[0, pl.ds(q1, bq), :], stack_heads(k_ref[0, pl.ds(k1, bk), :]))
        return (trip + 1, i0, i1, i2, i3, idle)

    zero = jnp.int32(0)
    lax.while_loop(keep_going, body, (zero, zero, zero, zero, zero, zero))


def _sb_attention(q, k, v, bq, bk, q_start):
    b, tq, d = q.shape
    tk = k.shape[1]
    assert tq % bq == 0 and bq % bk == 0 and q_start % bk == 0 and tk == q_start + tq
    per_q = bq // bk
    items, n_lanes, lane_start, lane_end = _attn_schedule(tq // bq, per_q, q_start // bk, bq, bk)
    lane_smem = pltpu.SMEM((n_lanes,), jnp.int32)
    table = jnp.asarray(np.asarray(items, np.int32).T.reshape(-1))
    qspec = pl.BlockSpec((1, tq, PAIR), lambda bi, pi, tab: (bi, 0, pi))
    kspec = pl.BlockSpec((1, tk, PAIR), lambda bi, pi, tab: (bi, 0, pi))
    grid_spec = pltpu.PrefetchScalarGridSpec(
        num_scalar_prefetch=1,
        grid=(b, d // PAIR),
        in_specs=[qspec, kspec, kspec],
        out_specs=qspec,
        scratch_shapes=[pltpu.VMEM((bq, 2 * bk), F32), pltpu.VMEM((2, bq, 2 * bk), F32),
                        pltpu.VMEM((2, bq, 2 * bk), BF16), pltpu.VMEM((2, bq, 2 * bk), F32),
                        pltpu.VMEM((bq, 2 * bk), BF16), pltpu.VMEM((n_lanes, bq, 2 * bk), F32),
                        pltpu.VMEM((n_lanes, bq, PAIR), F32), pltpu.VMEM((per_q + 1, bq, bk), F32),
                        pltpu.VMEM((2 * bk, 2 * bk), BF16), lane_smem, lane_smem, lane_smem, lane_smem],
    )
    return pl.pallas_call(
        functools.partial(_sb_attn_kernel, bq=bq, bk=bk, tab_len=len(items), n_lanes=n_lanes,
                          lane_start=tuple(lane_start), lane_end=tuple(lane_end)),
        grid_spec=grid_spec,
        out_shape=jax.ShapeDtypeStruct((b, tq, d), BF16),
        compiler_params=_params("parallel", "parallel"),
        name="sb_attention",
    )(table, q, k, v)


def _mm_res_ln_kernel(a_ref, w_ref, x_ref, g_ref, b_ref, o_ref):
    y = _dot(a_ref[...], w_ref[...])
    o_ref[...] = _layer_norm(DEEPNORM_ALPHA * x_ref[...] + y, g_ref[...], b_ref[...])


def _mm_res_ln(a, w_bf16, x, g, b, tm):
    t, d = x.shape
    kdim = a.shape[1]
    vec = pl.BlockSpec((1, d), lambda i: (0, 0))
    return pl.pallas_call(
        _mm_res_ln_kernel,
        grid=(t // tm,),
        in_specs=[pl.BlockSpec((tm, kdim), lambda i: (i, 0)),
                  pl.BlockSpec((kdim, d), lambda i: (0, 0)),
                  pl.BlockSpec((tm, d), lambda i: (i, 0)), vec, vec],
        out_specs=pl.BlockSpec((tm, d), lambda i: (i, 0)),
        out_shape=jax.ShapeDtypeStruct((t, d), F32),
        compiler_params=_params("parallel"),
        name="attn_out_ln",
    )(a, w_bf16, x, g.reshape(1, d), b.reshape(1, d))


def _swiglu_step(xb, wg_ref, wu_ref, wd_ref):
    gate = _dot(xb, wg_ref[...])
    up = _dot(xb, wu_ref[...])
    hidden = gate * jax.nn.sigmoid(gate) * up
    return _dot(hidden.astype(BF16), wd_ref[...])


def _ffn_ln_kernel(x_ref, wg_ref, wu_ref, wd_ref, g_ref, b_ref, o_ref, xb_scr, acc_scr):
    j = pl.program_id(1)

    @pl.when(j == 0)
    def _():
        xb_scr[...] = x_ref[...].astype(BF16)
        acc_scr[...] = jnp.zeros_like(acc_scr)

    acc_scr[...] += _swiglu_step(xb_scr[...], wg_ref, wu_ref, wd_ref)

    @pl.when(j == pl.num_programs(1) - 1)
    def _():
        o_ref[...] = _layer_norm(DEEPNORM_ALPHA * x_ref[...] + acc_scr[...], g_ref[...], b_ref[...])


def _ffn_ln(x, w_gate_up_bf16, w_down_bf16, g, b, tm, tn):
    t, d = x.shape
    f = w_down_bf16.shape[0]
    nj = f // tn
    vec = pl.BlockSpec((1, d), lambda i, j: (0, 0))
    return pl.pallas_call(
        _ffn_ln_kernel,
        grid=(t // tm, nj),
        in_specs=[pl.BlockSpec((tm, d), lambda i, j: (i, 0)),
                  pl.BlockSpec((d, tn), lambda i, j: (0, j)),
                  pl.BlockSpec((d, tn), lambda i, j: (0, nj + j)),
                  pl.BlockSpec((tn, d), lambda i, j: (j, 0)), vec, vec],
        out_specs=pl.BlockSpec((tm, d), lambda i, j: (i, 0)),
        out_shape=jax.ShapeDtypeStruct((t, d), F32),
        scratch_shapes=[pltpu.VMEM((tm, d), BF16), pltpu.VMEM((tm, d), F32)],
        compiler_params=_params("parallel", "arbitrary"),
        name="dense_swiglu_ln",
    )(x, w_gate_up_bf16, w_gate_up_bf16, w_down_bf16, g.reshape(1, d), b.reshape(1, d))


def _moe_ffn_kernel(te_ref, nt_ref, x_ref, wg_ref, wu_ref, wd_ref, o_ref, xb_scr, acc_scr):
    i = pl.program_id(0)
    j = pl.program_id(1)
    used = i < nt_ref[0]

    @pl.when(used & (j == 0))
    def _():
        xb_scr[...] = x_ref[...].astype(BF16)
        acc_scr[...] = jnp.zeros_like(acc_scr)

    @pl.when(used)
    def _():
        acc_scr[...] += _swiglu_step(xb_scr[...], wg_ref.at[0], wu_ref.at[0], wd_ref.at[0])

    @pl.when(j == pl.num_programs(1) - 1)
    def _():
        o_ref[...] = jnp.where(used, acc_scr[...], 0.0)


def _moe_ffn(tile_expert, n_tiles_used, xg, w_gate_up_bf16, w_down_bf16, tm, tn):
    n, d = xg.shape
    f = w_down_bf16.shape[1]
    nj = f // tn
    grid_spec = pltpu.PrefetchScalarGridSpec(
        num_scalar_prefetch=2,
        grid=(n // tm, nj),
        in_specs=[pl.BlockSpec((tm, d), lambda i, j, te, nt: (i, 0)),
                  pl.BlockSpec((1, d, tn), lambda i, j, te, nt: (te[i], 0, j)),
                  pl.BlockSpec((1, d, tn), lambda i, j, te, nt: (te[i], 0, nj + j)),
                  pl.BlockSpec((1, tn, d), lambda i, j, te, nt: (te[i], j, 0))],
        out_specs=pl.BlockSpec((tm, d), lambda i, j, te, nt: (i, 0)),
        scratch_shapes=[pltpu.VMEM((tm, d), BF16), pltpu.VMEM((tm, d), F32)],
    )
    return pl.pallas_call(
        _moe_ffn_kernel,
        grid_spec=grid_spec,
        out_shape=jax.ShapeDtypeStruct((n, d), F32),
        compiler_params=_params("parallel", "arbitrary"),
        name="expert_swiglu",
    )(tile_expert, n_tiles_used, xg, w_gate_up_bf16, w_gate_up_bf16, w_down_bf16)


def _row_copy(src_hbm, dst_ref, sem, src_row, dst_row):
    return pltpu.make_async_copy(src_hbm.at[pl.ds(src_row, 1)], dst_ref.at[pl.ds(dst_row, 1)], sem)


def _start_row_gather(idx_ref, src_hbm, dst_ref, sem, n):
    def start(i, c):
        for p in range(2):
            _row_copy(src_hbm, dst_ref, sem, idx_ref[0, 0, 2 * i + p], 2 * i + p).start(priority=p)
        return c

    lax.fori_loop(0, n // 2, start, 0, unroll=DMA_ISSUE_UNROLL // 2)


def _wait_row_gather(src_hbm, dst_ref, sem, n):
    pltpu.make_async_copy(src_hbm.at[pl.ds(0, n)], dst_ref, sem).wait()


def _scatter_copy(x_ref, dst_hbm, sem, src_row, dst_row):
    return pltpu.make_async_copy(x_ref.at[pl.ds(src_row, 1)], dst_hbm.at[pl.ds(dst_row, 1)], sem)


def _scatter_kernel(p0_ref, p1_ref, x_ref, dst_in_hbm_unused, dst_hbm, sem, *, tm):
    def start(r, c):
        _scatter_copy(x_ref, dst_hbm, sem, r, p0_ref[0, 0, r]).start(priority=0)
        _scatter_copy(x_ref, dst_hbm, sem, r, p1_ref[0, 0, r]).start(priority=1)
        return c

    lax.fori_loop(0, tm, start, 0, unroll=DMA_ISSUE_UNROLL)
    for _ in range(2):
        pltpu.make_async_copy(x_ref, dst_hbm.at[pl.ds(0, tm)], sem).wait()


def _scatter_rows(x, pos0, pos1, dst, tm):
    t, d = x.shape
    idx = pl.BlockSpec((1, 1, tm), lambda i: (i, 0, 0), memory_space=pltpu.SMEM)
    return pl.pallas_call(
        functools.partial(_scatter_kernel, tm=tm),
        grid=(t // tm,),
        in_specs=[idx, idx, pl.BlockSpec((tm, d), lambda i: (i, 0)), pl.BlockSpec(memory_space=pl.ANY)],
        out_specs=pl.BlockSpec(memory_space=pl.ANY),
        out_shape=jax.ShapeDtypeStruct(dst.shape, dst.dtype),
        input_output_aliases={3: 0},
        scratch_shapes=[pltpu.SemaphoreType.DMA(())],
        compiler_params=_params("arbitrary"),
        name="scatter_rows",
    )(pos0.reshape(-1, 1, tm), pos1.reshape(-1, 1, tm), x, dst)


def _combine_ln_kernel(p0_ref, p1_ref, p0_next_ref, p1_next_ref, x_ref, route_ref, y_hbm, g_ref, b_ref,
                       op_ref, os_ref, a_scr, b_scr, sems, *, tm, n_prompt_tiles):
    i = pl.program_id(0)
    slot = i % 2

    def start(p0, p1, s):
        _start_row_gather(p0, y_hbm, a_scr.at[s], sems.at[0, s], tm)
        _start_row_gather(p1, y_hbm, b_scr.at[s], sems.at[1, s], tm)

    @pl.when(i == 0)
    def _():
        start(p0_ref, p1_ref, 0)

    @pl.when(i + 1 < pl.num_programs(0))
    def _():
        start(p0_next_ref, p1_next_ref, 1 - slot)

    _wait_row_gather(y_hbm, a_scr.at[slot], sems.at[0, slot], tm)
    _wait_row_gather(y_hbm, b_scr.at[slot], sems.at[1, slot], tm)
    moe = route_ref[:, 2:3] * a_scr[slot] + route_ref[:, 3:4] * b_scr[slot]
    out = _layer_norm(DEEPNORM_ALPHA * x_ref[...] + moe, g_ref[...], b_ref[...])

    @pl.when(i < n_prompt_tiles)
    def _():
        op_ref[...] = out

    @pl.when(i >= n_prompt_tiles)
    def _():
        os_ref[...] = out


def _combine_ln(x, route, y_sorted, pos0, pos1, g, b, tm, n_prompt):
    t, d = x.shape
    npt = n_prompt // tm
    nst = (t - n_prompt) // tm
    n_tiles = npt + nst
    idx = pl.BlockSpec((1, 1, tm), lambda i: (i, 0, 0), memory_space=pltpu.SMEM)
    idx_next = pl.BlockSpec((1, 1, tm), lambda i: (jnp.minimum(i + 1, n_tiles - 1), 0, 0), memory_space=pltpu.SMEM)
    vec = pl.BlockSpec((1, d), lambda i: (0, 0))
    p0, p1 = pos0.reshape(-1, 1, tm), pos1.reshape(-1, 1, tm)
    return pl.pallas_call(
        functools.partial(_combine_ln_kernel, tm=tm, n_prompt_tiles=npt),
        grid=(n_tiles,),
        in_specs=[idx, idx, idx_next, idx_next, pl.BlockSpec((tm, d), lambda i: (i, 0)),
                  pl.BlockSpec((tm, LANES), lambda i: (i, 0)), pl.BlockSpec(memory_space=pl.ANY), vec, vec],
        out_specs=[pl.BlockSpec((tm, d), lambda i: (jnp.minimum(i, npt - 1), 0)),
                   pl.BlockSpec((tm, d), lambda i: (jnp.maximum(i - npt, 0), 0))],
        out_shape=[jax.ShapeDtypeStruct((n_prompt, d), F32), jax.ShapeDtypeStruct((t - n_prompt, d), F32)],
        scratch_shapes=[pltpu.VMEM((2, tm, d), F32), pltpu.VMEM((2, tm, d), F32), pltpu.SemaphoreType.DMA((2, 2))],
        compiler_params=_params("arbitrary"),
        name="moe_combine_ln",
    )(p0, p1, p0, p1, x, route, y_sorted, g.reshape(1, d), b.reshape(1, d))


def _rwkv_proj_kernel(x_ref, prev_ref, mu_ref, wr_ref, wk_ref, wv_ref, w1_ref, w2_ref, a1_ref, a2_ref,
                      g1_ref, g2_ref, vec_ref,
                      r_ref, lw_ref, k_ref, v_ref, kk_ref, b_ref, g_ref, bonus_ref, *, seg):
    x = x_ref[...]
    tm, d = x.shape
    rolled = pltpu.roll(x, 1, 0).reshape(tm // seg, seg, d)
    first = lax.broadcasted_iota(jnp.int32, (tm // seg, seg, d), 1) == 0
    x_prev = jnp.where(first, prev_ref[...], rolled).reshape(tm, d)
    xx = x_prev - x

    def mixed(i):
        return (x + xx * mu_ref[i:i + 1, :]).astype(BF16)

    r = _dot(mixed(0), wr_ref[...])
    k = _dot(mixed(2), wk_ref[...])
    v = _dot(mixed(3), wv_ref[...])
    w0, a0, k_k, k_a, r_k = (vec_ref[i:i + 1, :] for i in range(5))
    wl = w0 + _dot(jnp.tanh(_dot(mixed(1), w1_ref[...])).astype(BF16), w2_ref[...])
    softplus_neg = jnp.maximum(-wl, 0.0) + jnp.log(1.0 + jnp.exp(-jnp.abs(wl)))
    lw_ref[...] = -jnp.exp(-softplus_neg - 0.5)
    a = jax.nn.sigmoid(a0 + _dot(_dot(mixed(4), a1_ref[...]).astype(BF16), a2_ref[...]))
    g_ref[...] = _dot(jax.nn.sigmoid(_dot(mixed(5), g1_ref[...])).astype(BF16), g2_ref[...])
    ones_pair = _head_ones(PAIR)
    kk = k * k_k
    kk = kk / jnp.maximum(jnp.sqrt(_head_sum(kk * kk, ones_pair)), 1e-12)
    k = k * (1.0 + (a - 1.0) * k_a)
    r_ref[...] = r
    k_ref[...] = k
    v_ref[...] = v
    kk_ref[...] = kk
    b_ref[...] = kk * a
    bonus_ref[...] = _head_sum(r * k * r_k, ones_pair) * v


def _rwkv_proj(x, prev_rows, mu, w_r, w_k, w_v, w1, w2, a1, a2, g1, g2, vecs, tm, seg):
    t, d = x.shape
    row = pl.BlockSpec((tm, d), lambda i: (i, 0))

    def full(a):
        return pl.BlockSpec(a.shape, lambda i: (0,) * a.ndim)

    consts = (mu, w_r, w_k, w_v, w1, w2, a1, a2, g1, g2, vecs)
    return pl.pallas_call(
        functools.partial(_rwkv_proj_kernel, seg=seg),
        grid=(t // tm,),
        in_specs=[row, pl.BlockSpec((tm // seg, 1, d), lambda i: (i, 0, 0))] + [full(c) for c in consts],
        out_specs=[row] * 8,
        out_shape=[jax.ShapeDtypeStruct((t, d), F32)] * 8,
        compiler_params=_params("parallel"),
        name="rwkv_proj",
    )(x, prev_rows, *consts)


def _unit_lower_inverses(lows, limit):
    n = lows[0].shape[0]
    r = lax.broadcasted_iota(jnp.int32, (n, n), 0)
    c = lax.broadcasted_iota(jnp.int32, (n, n), 1)
    eye = (r == c).astype(F32)
    base = (r >> 3) == (c >> 3)
    ps = [jnp.where(base, -low, 0.0) for low in lows]
    invs = [eye + p for p in ps]
    for _ in range(2):
        ps = [_dot2(p, p) for p in ps]
        invs = [inv + _dot2(inv, p) for inv, p in zip(invs, ps)]
    shift = 3
    while (1 << shift) < limit:
        sel = ((r >> (shift + 1)) == (c >> (shift + 1))) & ((r >> shift) != (c >> shift))
        mids = [_dot2(inv, jnp.where(sel, low, 0.0)) for inv, low in zip(invs, lows)]
        invs = [inv - _dot2(mid, inv) for inv, mid in zip(invs, mids)]
        shift += 1
    return invs


def _rwkv_scan_kernel(r_ref, lw_ref, k_ref, v_ref, kk_ref, b_ref, h0_ref, o_ref, hT_ref, h_scr, *, n_chunks, n_pairs):
    cs = SCAN_CHUNK
    tb = pl.program_id(2)

    @pl.when(tb == 0)
    def _():
        h_scr[...] = h0_ref[0]

    lane = lax.broadcasted_iota(jnp.int32, (cs, PAIR), 1)
    row = lax.broadcasted_iota(jnp.int32, (cs, PAIR), 0)
    head_a = lane < HEAD_DIM
    strict = (lane & (HEAD_DIM - 1)) < row
    incl = (lane & (HEAD_DIM - 1)) <= row
    tri = (lax.broadcasted_iota(jnp.int32, (cs, cs), 1) <= lax.broadcasted_iota(jnp.int32, (cs, cs), 0)).astype(BF16)
    pr = lax.broadcasted_iota(jnp.int32, (PAIR, PAIR), 0) >> 6
    pc = lax.broadcasted_iota(jnp.int32, (PAIR, PAIR), 1) >> 6
    same_head = pr == pc

    def stack(x):
        return jnp.concatenate([jnp.where(head_a, x, 0.0), jnp.where(head_a, 0.0, x)], axis=0)

    def each(fn, *lists):
        return [fn(*xs) for xs in zip(*lists)]

    def cumulative(lw):
        l1 = lw.astype(BF16)
        rem = lw - l1.astype(F32)
        l2 = rem.astype(BF16)
        l3 = (rem - l2.astype(F32)).astype(BF16)
        parts = _dot(tri, jnp.concatenate([l1, l2, l3], axis=1))
        return parts[:, :PAIR] + parts[:, PAIR:2 * PAIR] + parts[:, 2 * PAIR:]

    def chunk(ci, carry):
        sl = pl.ds(pl.multiple_of(ci * cs, cs), cs)
        lanes = [slice(p * PAIR, (p + 1) * PAIR) for p in range(n_pairs)]
        r, lw, k, v, kk, b = ([ref[0, sl, ln] for ln in lanes] for ref in (r_ref, lw_ref, k_ref, v_ref, kk_ref, b_ref))
        h = [h_scr[p] for p in range(n_pairs)]
        cum = each(cumulative, lw)
        total = [c[cs - 1:cs, :] for c in cum]
        r_hat = each(lambda x, c: x * jnp.exp(c), r, cum)
        kk_hat = each(lambda x, c, l: x * jnp.exp(c - l), kk, cum, lw)
        k_hat = each(lambda x, c: x * jnp.exp(-c), k, cum)
        b_hat = each(lambda x, c: x * jnp.exp(-c), b, cum)
        gram = each(lambda kh, rh, bh, k2: _dot2_nt(jnp.concatenate([kh, rh], axis=0),
                                                    jnp.concatenate([stack(bh), stack(k2)], axis=0)),
                    kk_hat, r_hat, b_hat, k_hat)
        l_b = [jnp.where(strict, g[:cs, :PAIR], 0.0) for g in gram]
        l_k = [jnp.where(strict, g[:cs, PAIR:], 0.0) for g in gram]
        a_rb = [jnp.where(incl, g[cs:, :PAIR], 0.0) for g in gram]
        a_rk = [jnp.where(incl, g[cs:, PAIR:], 0.0) for g in gram]
        v_st = each(stack, v)
        w = each(lambda kh, lk, hh, vs: _dot2(jnp.concatenate([kh, lk], axis=1), jnp.concatenate([hh, vs], axis=0)),
                 kk_hat, l_k, h, v_st)
        inv = _unit_lower_inverses(each(stack, l_b), cs)
        u_st = each(lambda t_, w_: _dot2(t_, stack(w_)), inv, w)
        o = each(lambda rh, ak, ab, hh, vs, us: _dot2(jnp.concatenate([rh, ak, -ab], axis=1),
                                                      jnp.concatenate([hh, vs, us], axis=0)),
                 r_hat, a_rk, a_rb, h, v_st, u_st)
        for ln, o_p in zip(lanes, o):
            o_ref[0, sl, ln] = o_p
        u = [us[:cs] + us[cs:] for us in u_st]
        to_end = each(lambda t_, c: jnp.exp(t_ - c), total, cum)
        upd = each(lambda k_, b_, e_, v_, u_: _dot2_tn(jnp.concatenate([k_ * e_, -(b_ * e_)], axis=0),
                                                       jnp.concatenate([v_, u_], axis=0)),
                   k, b, to_end, v, u)
        for p in range(n_pairs):
            decay_col = jnp.transpose(jnp.broadcast_to(jnp.exp(total[p]), (PAIR, PAIR)))
            h_scr[p] = decay_col * h[p] + jnp.where(same_head, upd[p], 0.0)
        return carry

    lax.fori_loop(0, n_chunks, chunk, 0)

    @pl.when(tb == pl.num_programs(2) - 1)
    def _():
        hT_ref[0] = h_scr[...]


def _rwkv_scan(r, lw, k, v, kk, b, h0, tb):
    bsz, t, d = r.shape
    npair = d // PAIR
    group = math.gcd(npair, SCAN_PAIRS_PER_STEP)
    seq = pl.BlockSpec((1, tb, group * PAIR), lambda bi, pi, ti: (bi, ti, pi))
    st = pl.BlockSpec((1, group, PAIR, PAIR), lambda bi, pi, ti: (bi, pi, 0, 0))
    return pl.pallas_call(
        functools.partial(_rwkv_scan_kernel, n_chunks=tb // SCAN_CHUNK, n_pairs=group),
        grid=(bsz, npair // group, t // tb),
        in_specs=[seq] * 6 + [st],
        out_specs=[seq, st],
        out_shape=[jax.ShapeDtypeStruct((bsz, t, d), F32), jax.ShapeDtypeStruct((bsz, npair, PAIR, PAIR), F32)],
        scratch_shapes=[pltpu.VMEM((group, PAIR, PAIR), F32)],
        compiler_params=_params("parallel", "parallel", "arbitrary"),
        name="rwkv_scan",
    )(r, lw, k, v, kk, b, h0)


def _rwkv_out_kernel(o_ref, g_ref, bonus_ref, x_ref, wo_ref, vec_ref, dst_hbm_unused, out_ref):
    o = o_ref[...]
    ones_pair = _head_ones(PAIR)
    mean = _head_sum(o, ones_pair) * (1.0 / HEAD_DIM)
    c = o - mean
    var = _head_sum(c * c, ones_pair) * (1.0 / HEAD_DIM)
    on = c * lax.rsqrt(var + GN_EPS) * vec_ref[0:1, :] + vec_ref[1:2, :]
    y = _dot(((on + bonus_ref[...]) * g_ref[...]).astype(BF16), wo_ref[...])
    out_ref[...] = _layer_norm(DEEPNORM_ALPHA * x_ref[...] + y, vec_ref[2:3, :], vec_ref[3:4, :])


def _rwkv_out(o, g, bonus, x, w_o_bf16, vecs, dst, n_rows, row_offset, tm):
    t, d = x.shape
    row = pl.BlockSpec((tm, d), lambda i: (i, 0))
    off = row_offset // tm
    return pl.pallas_call(
        _rwkv_out_kernel,
        grid=(t // tm,),
        in_specs=[row, row, row, row, pl.BlockSpec((d, d), lambda i: (0, 0)),
                  pl.BlockSpec(vecs.shape, lambda i: (0, 0)), pl.BlockSpec(memory_space=pl.ANY)],
        out_specs=pl.BlockSpec((tm, d), lambda i: (off + i, 0)),
        out_shape=jax.ShapeDtypeStruct((n_rows, d), F32),
        input_output_aliases={6: 0},
        compiler_params=_params("parallel"),
        name="rwkv_out_ln",
    )(o, g, bonus, x, w_o_bf16, vecs, dst)


def _router_kernel(x_ref, w_ref, o_ref, *, n_experts):
    logits = _dot3(x_ref[...], w_ref[...])
    lane = lax.broadcasted_iota(jnp.int32, logits.shape, 1)
    lane_f = lane.astype(F32)
    lowest = jnp.float32(-3.0e38)
    logits = jnp.where(lane < n_experts, logits, lowest)
    m1 = jnp.max(logits, axis=1, keepdims=True)
    i1 = jnp.min(jnp.where(logits == m1, lane_f, float(LANES)), axis=1, keepdims=True)
    rest = jnp.where(lane_f == i1, lowest, logits)
    m2 = jnp.max(rest, axis=1, keepdims=True)
    i2 = jnp.min(jnp.where(rest == m2, lane_f, float(LANES)), axis=1, keepdims=True)
    e = jnp.exp(m2 - m1)
    g1 = 1.0 / (1.0 + e)
    g2 = e * g1
    out = jnp.where(lane == 0, i1, 0.0)
    out = jnp.where(lane == 1, i2, out)
    out = jnp.where(lane == 2, g1, out)
    o_ref[...] = jnp.where(lane == 3, g2, out)


def _router(x, w_router, tm):
    t, d = x.shape
    e = w_router.shape[1]
    w_pad = jnp.zeros((d, LANES), F32).at[:, :e].set(w_router)
    return pl.pallas_call(
        functools.partial(_router_kernel, n_experts=e),
        grid=(t // tm,),
        in_specs=[pl.BlockSpec((tm, d), lambda i: (i, 0)), pl.BlockSpec((d, LANES), lambda i: (0, 0))],
        out_specs=pl.BlockSpec((tm, LANES), lambda i: (i, 0)),
        out_shape=jax.ShapeDtypeStruct((t, LANES), F32),
        compiler_params=_params("parallel"),
        name="router_top2",
    )(x, w_pad)


def _ffn_chunk(f, cap=1536):
    best = LANES
    for tn in range(LANES, cap + 1, LANES):
        if f % tn == 0:
            best = tn
    return best


def _row_tile(t, cap):
    tm = cap
    while t % tm:
        tm //= 2
    return tm


def _pair_states(s):
    b, h, n, _ = s.shape
    st = jnp.swapaxes(s, 2, 3).reshape(b, h // 2, 2, n, n)
    z = jnp.zeros_like(st[:, :, 0])
    top = jnp.concatenate([st[:, :, 0], z], axis=-1)
    bot = jnp.concatenate([z, st[:, :, 1]], axis=-1)
    return jnp.concatenate([top, bot], axis=-2)


def _unpair_states(hp):
    b, p, _, _ = hp.shape
    n = HEAD_DIM
    st = jnp.stack([hp[:, :, :n, :n], hp[:, :, n:, n:]], axis=2).reshape(b, 2 * p, n, n)
    return jnp.swapaxes(st, 2, 3)


def _attention_layer(x, w_qkv, w_o, ln_g, ln_b, cache_k=None, cache_v=None):
    b, t, d = x.shape
    q, kb, vb, kf, vf = _qkv_proj(x, w_qkv, _row_tile(t, 512))
    if cache_k is None:
        ao = _sb_attention(q, kb, vb, _row_tile(t, ATTN_QUERY_BLOCK), LANES, 0)
    else:
        past = cache_k.shape[2]
        merge = lambda c: jnp.swapaxes(c.astype(BF16), 1, 2).reshape(b, past, d)
        keys = jnp.concatenate([merge(cache_k), kb], axis=1)
        vals = jnp.concatenate([merge(cache_v), vb], axis=1)
        ao = _sb_attention(q, keys, vals, t, t, past)
    x1 = _mm_res_ln(ao.reshape(b * t, d), w_o, x.reshape(b * t, d), ln_g, ln_b, _row_tile(b * t, 512))
    return x1, kf, vf


def _rwkv_layer(x, x_last, s0, p, dst, n_rows, row_offset):
    b, t, d = x.shape
    seg = SCAN_CHUNK
    xs = x.reshape(b, t // seg, seg, d)
    prev = jnp.concatenate([x_last[:, None, :], xs[:, :-1, seg - 1, :]], axis=1).reshape(b * t // seg, 1, d)
    xf = x.reshape(b * t, d)
    tm = _row_tile(b * t, 256)
    r, lw, k, v, kk, bb, g, bonus = _rwkv_proj(xf, prev, p["mu"], p["w_r"], p["w_k"], p["w_v"], p["w1"], p["w2"],
                                               p["a1"], p["a2"], p["g1"], p["g2"], p["proj_vecs"], tm, seg)
    shp = (b, t, d)
    o, h_t = _rwkv_scan(*(z.reshape(shp) for z in (r, lw, k, v, kk, bb)), _pair_states(s0), _row_tile(t, 512))
    dst = _rwkv_out(o.reshape(b * t, d), g, bonus, xf, p["w_o"], p["out_vecs"], dst, n_rows, row_offset, tm)
    return dst, _unpair_states(h_t)


def _moe_layer(x, n_prompt, w_router, w_gate_up, w_down, ln_g, ln_b, tm):
    t, d = x.shape
    e = w_router.shape[1]
    route = _router(x, w_router, tm)
    experts = jnp.concatenate([route[:, 0], route[:, 1]]).astype(jnp.int32)
    onehot = (experts[:, None] == jnp.arange(e, dtype=jnp.int32)[None, :]).astype(jnp.int32)
    counts = jnp.sum(onehot, axis=0)
    padded = ((counts + tm - 1) // tm) * tm
    ends = jnp.cumsum(padded)
    starts = ends - padded
    rank = jnp.sum(jnp.cumsum(onehot, axis=0) * onehot, axis=1) - 1
    pos = (jnp.sum(starts[None, :] * onehot, axis=1) + rank).astype(jnp.int32)
    n_rows = ((2 * t + e * (tm - 1)) // tm) * tm
    tile_start = jnp.arange(n_rows // tm, dtype=jnp.int32) * tm
    tile_expert = jnp.minimum(jnp.sum((tile_start[:, None] >= ends[None, :]).astype(jnp.int32), axis=1), e - 1)
    n_tiles_used = (ends[-1] // tm).astype(jnp.int32).reshape(1)
    xg = _scatter_rows(x, pos[:t], pos[t:], jnp.zeros((n_rows, d), F32), tm)
    y = _moe_ffn(tile_expert, n_tiles_used, xg, w_gate_up, w_down, tm, _ffn_chunk(w_down.shape[1], cap=2048))
    return _combine_ln(x, route, y, pos[:t], pos[t:], ln_g, ln_b, tm, n_prompt)


def kernel(x_prompt, x_sample, cache_k, cache_v, state_wkv, state_shift, att_w_qkv, att_w_o, ffn_w_gate_up, ffn_w_down, rwkv_mu, rwkv_w_rkv, rwkv_w0, rwkv_w1, rwkv_w2, rwkv_a0, rwkv_a1, rwkv_a2, rwkv_g1, rwkv_g2, rwkv_k_k, rwkv_k_a, rwkv_r_k, rwkv_gn_g, rwkv_gn_b, rwkv_w_o, moe_w_router, moe_w_gate_up, moe_w_down, ln_mix_g, ln_mix_b, ln_ffn_g, ln_ffn_b):
    bp, tp, d = x_prompt.shape
    bs, ts, _ = x_sample.shape
    n_prompt, n_sample = bp * tp, bs * ts
    bf = lambda a: a.astype(BF16)

    w_qkv, w_o = bf(att_w_qkv[0]), bf(att_w_o[0])
    xp1, k_p, v_p = _attention_layer(x_prompt, w_qkv, w_o, ln_mix_g[0], ln_mix_b[0])
    xs1, k_s, v_s = _attention_layer(x_sample, w_qkv, w_o, ln_mix_g[0], ln_mix_b[0], cache_k[0], cache_v[0])
    w_gu, w_dn = bf(ffn_w_gate_up[0]), bf(ffn_w_down[0])
    tn = _ffn_chunk(w_dn.shape[0])
    xp2 = _ffn_ln(xp1, w_gu, w_dn, ln_ffn_g[0], ln_ffn_b[0], _row_tile(n_prompt, 512), tn).reshape(bp, tp, d)
    xs2 = _ffn_ln(xs1, w_gu, w_dn, ln_ffn_g[0], ln_ffn_b[0], _row_tile(n_sample, 512), tn).reshape(bs, ts, d)

    p = {
        "mu": rwkv_mu[0], "w_r": bf(rwkv_w_rkv[0, 0]), "w_k": bf(rwkv_w_rkv[0, 1]), "w_v": bf(rwkv_w_rkv[0, 2]),
        "w1": bf(rwkv_w1[0]), "w2": bf(rwkv_w2[0]), "a1": bf(rwkv_a1[0]), "a2": bf(rwkv_a2[0]),
        "g1": bf(rwkv_g1[0]), "g2": bf(rwkv_g2[0]), "w_o": bf(rwkv_w_o[0]),
        "proj_vecs": jnp.stack([rwkv_w0[0], rwkv_a0[0], rwkv_k_k[0], rwkv_k_a[0], rwkv_r_k[0].reshape(d)]),
        "out_vecs": jnp.stack([rwkv_gn_g[0], rwkv_gn_b[0], ln_mix_g[1], ln_mix_b[1]]),
    }
    n_all = n_prompt + n_sample
    x3, wkv_p = _rwkv_layer(xp2, jnp.zeros((bp, d), F32), jnp.zeros((bp, d // HEAD_DIM, HEAD_DIM, HEAD_DIM), F32),
                            p, jnp.zeros((n_all, d), F32), n_all, 0)
    x3, wkv_s = _rwkv_layer(xs2, state_shift[0], state_wkv[0], p, x3, n_all, n_prompt)
    y_p, y_s = _moe_layer(x3, n_prompt, moe_w_router[0], bf(moe_w_gate_up[0]), bf(moe_w_down[0]),
                          ln_ffn_g[1], ln_ffn_b[1], _row_tile(n_sample, 512))
    return (y_p.reshape(bp, tp, d), y_s.reshape(bs, ts, d),
            k_p[None], v_p[None], wkv_p[None], xp2[:, -1][None],
            k_s[None], v_s[None], wkv_s[None], xs2[:, -1][None])
```

```python
import functools
import math

import jax
import jax.numpy as jnp
import numpy as np
from jax import lax
from jax.experimental import pallas as pl
from jax.experimental.pallas import tpu as pltpu

F32 = jnp.float32
BF16 = jnp.bfloat16

HEAD_DIM = 64
LANES = 128
PAIR = 2 * HEAD_DIM
LN_EPS = 1e-5
GN_EPS = 64e-5
DEEPNORM_ALPHA = 4.0 ** 0.25
Q_SCALE = HEAD_DIM ** -0.5 * 1.4426950408889634
SCAN_PAIRS_PER_STEP = 8
DMA_ISSUE_UNROLL = 8
SCAN_CHUNK = 64
V7X_VMEM_LIMIT = 56 * 1024 * 1024


def _params(*sem, vmem=V7X_VMEM_LIMIT):
    return pltpu.CompilerParams(dimension_semantics=sem, vmem_limit_bytes=vmem)


def _layer_norm(y, g, b):
    mu = jnp.mean(y, axis=-1, keepdims=True)
    c = y - mu
    var = jnp.mean(c * c, axis=-1, keepdims=True)
    return c * lax.rsqrt(var + LN_EPS) * g + b


def _split2(x):
    hi = x.astype(BF16)
    lo = (x - hi.astype(F32)).astype(BF16)
    return hi, lo


def _dot(a, b):
    return jnp.dot(a, b, preferred_element_type=F32)


def _dot_nt(a, b):
    return lax.dot_general(a, b, (((1,), (1,)), ((), ())), preferred_element_type=F32)


def _dot_tn(a, b):
    return lax.dot_general(a, b, (((0,), (0,)), ((), ())), preferred_element_type=F32)


def _dot3(a, b):
    ah, al = _split2(a)
    bh, bl = _split2(b)
    return _dot(jnp.concatenate([ah, ah, al], axis=1), jnp.concatenate([bh, bl, bh], axis=0))


def _dot2(a, b):
    ah, al = _split2(a)
    bb = b.astype(BF16)
    return _dot(jnp.concatenate([ah, al], axis=1), jnp.concatenate([bb, bb], axis=0))


def _dot2_nt(a, b):
    ah, al = _split2(a)
    bb = b.astype(BF16)
    return _dot_nt(jnp.concatenate([ah, al], axis=1), jnp.concatenate([bb, bb], axis=1))


def _dot2_tn(a, b):
    ah, al = _split2(a)
    bb = b.astype(BF16)
    return _dot_tn(jnp.concatenate([ah, al], axis=0), jnp.concatenate([bb, bb], axis=0))


def _dot2_exact_rhs(a, b_bf16):
    ah, al = _split2(a)
    return _dot(jnp.concatenate([ah, al], axis=1), jnp.concatenate([b_bf16, b_bf16], axis=0))


def _head_ones(n):
    r = lax.broadcasted_iota(jnp.int32, (n, n), 0) >> 6
    c = lax.broadcasted_iota(jnp.int32, (n, n), 1) >> 6
    return (r == c).astype(BF16)


def _head_sum(x, ones_pair):
    d = x.shape[1]
    cols = [_dot2_exact_rhs(x[:, c:c + PAIR], ones_pair) for c in range(0, d, PAIR)]
    return jnp.concatenate(cols, axis=1)


def _qkv_kernel(x_ref, wq_ref, wk_ref, wv_ref, qb_ref, kb_ref, vb_ref, kf_ref, vf_ref, *, n_heads):
    xb = x_ref[0].astype(BF16)
    qb_ref[0] = (_dot(xb, wq_ref[...]) * Q_SCALE).astype(BF16)
    for w_ref, b_ref, f_ref in ((wk_ref, kb_ref, kf_ref), (wv_ref, vb_ref, vf_ref)):
        y = _dot(xb, w_ref[...])
        b_ref[0] = y.astype(BF16)
        for h in range(n_heads):
            f_ref[0, h] = y[:, h * HEAD_DIM:(h + 1) * HEAD_DIM]


def _qkv_proj(x, w_bf16, tm):
    b, t, d = x.shape
    h = d // HEAD_DIM
    row = pl.BlockSpec((1, tm, d), lambda bi, ti: (bi, ti, 0))
    hspec = pl.BlockSpec((1, h, tm, HEAD_DIM), lambda bi, ti: (bi, 0, ti, 0))
    shp = (b, h, t, HEAD_DIM)
    return pl.pallas_call(
        functools.partial(_qkv_kernel, n_heads=h),
        grid=(b, t // tm),
        in_specs=[row] + [pl.BlockSpec((d, d), lambda bi, ti, n=n: (0, n)) for n in range(3)],
        out_specs=[row] * 3 + [hspec] * 2,
        out_shape=[jax.ShapeDtypeStruct((b, t, d), BF16)] * 3 + [jax.ShapeDtypeStruct(shp, F32)] * 2,
        compiler_params=_params("parallel", "parallel"),
        name="qkv_proj",
    )(x, w_bf16, w_bf16, w_bf16)


def _aligned(x, m):
    return x if isinstance(x, int) else pl.multiple_of(x, m)


ATTN_STAGES = 5
ATTN_QUERY_BLOCK = 128
MASKED_SCORE = -1.0e30


DEAD_LOG2 = -160.0
F_Q0, F_K0, F_BIAS, F_FIRST, F_REAL, F_QB, F_NEXT, F_LANE = range(8)
ATTN_MAX_LANES = 16
ATTN_ITEMS_PER_TRIP = 4


def _attn_schedule(nq, per_q, past_blocks, bq, bk):
    n_lanes = 1
    while n_lanes * 2 <= min(ATTN_MAX_LANES, nq):
        n_lanes *= 2
    items = [[0, 0, 0, 1, 0, -1, 0, lane] for lane in range(n_lanes)]
    lane_start, lane_end = [], []
    for lane in range(n_lanes):
        lane_start.append(len(items))
        mine = [qb for qb in range(nq)
                if (qb % (2 * n_lanes) if qb % (2 * n_lanes) < n_lanes else 2 * n_lanes - 1 - qb % (2 * n_lanes)) == lane]
        for qb in mine:
            n_full = past_blocks + qb * per_q
            blocks = [(n_full + j, j + 1) for j in reversed(range(per_q))] + [(kb, 0) for kb in reversed(range(n_full))]
            nxt = len(items) + len(blocks)
            for i, (kb, bias_id) in enumerate(blocks):
                items.append([qb * bq, kb * bk, bias_id, int(i == 0), 1, qb, nxt, lane])
        lane_end.append(len(items))
    return items, n_lanes, lane_start, lane_end


def _sb_attn_kernel(tab_ref, q_ref, k_ref, v_ref, o_ref, z_scr, lb_scr, x2_scr, sums_scr, a_scr, later_scr,
                    acc_scr, bias_scr, suffix_scr, cur_ref, dead_ref, qprev_ref, valid_ref,
                    *, bq, bk, tab_len, n_lanes, width, lane_start, lane_end):
    per_q = bq // bk
    for ln in range(n_lanes):
        cur_ref[ln] = lane_start[ln]
        dead_ref[ln] = -1
        qprev_ref[ln] = 0
        valid_ref[ln] = 0
    r = lax.broadcasted_iota(jnp.int32, (2 * bk, 2 * bk), 0)
    c = lax.broadcasted_iota(jnp.int32, (2 * bk, 2 * bk), 1)
    r = jnp.where(r >= bk, r - bk, r)
    suffix_scr[...] = ((c >= bk) | (r > c)).astype(BF16)
    qi = lax.broadcasted_iota(jnp.int32, (bq, bk), 0)
    kj = lax.broadcasted_iota(jnp.int32, (bq, bk), 1)
    bias_scr[0] = jnp.zeros((bq, bk), F32)
    for j in range(per_q):
        bias_scr[j + 1] = jnp.where((kj + j * bk) < qi, 0.0, MASKED_SCORE)
    for scr in (z_scr, lb_scr, x2_scr, sums_scr, a_scr, later_scr, acc_scr):
        scr[...] = jnp.zeros_like(scr)
    lane_a = lax.broadcasted_iota(jnp.int32, (bk, PAIR), 1) < HEAD_DIM

    def field(f, i):
        return tab_ref[f * tab_len + i]

    def stack_heads(blk):
        zero = jnp.zeros_like(blk)
        return jnp.concatenate([jnp.where(lane_a, blk, zero), jnp.where(lane_a, zero, blk)], axis=0)

    def keep_going(c):
        return c[1] < ATTN_STAGES - 1

    def body(c):
        trip, idle = c[0], c[1]
        in_flight = [c[2 + s * width:2 + (s + 1) * width] for s in range(ATTN_STAGES - 1)]
        i1, i2, i3, i4 = in_flight
        i0, any_more = [], None
        for w in range(width):
            lane = (trip * width + w) & (n_lanes - 1)
            end = jnp.int32(lane_end[0])
            for ln in range(1, n_lanes):
                end = jnp.where(lane == ln, lane_end[ln], end)
            nxt = cur_ref[lane]
            peek = jnp.minimum(nxt, tab_len - 1)
            nxt = jnp.where((nxt < end) & (field(F_QB, peek) == dead_ref[lane]), field(F_NEXT, peek), nxt)
            more = nxt < end
            i0.append(jnp.where(more, nxt, lane))
            cur_ref[lane] = jnp.where(more, nxt + 1, nxt)
            any_more = more if any_more is None else any_more | more
        idle = jnp.where(any_more, 0, idle + 1)
        first5 = [field(F_FIRST, i) == 1 for i in i4]
        lane5 = [field(F_LANE, i) for i in i4]
        for w in range(width):
            @pl.when(first5[w] & (valid_ref[lane5[w]] == 1))
            def _():
                o_ref[0, pl.ds(pl.multiple_of(qprev_ref[lane5[w]], bq), bq), :] = acc_scr[lane5[w]].astype(BF16)

            qprev_ref[lane5[w]] = field(F_Q0, i4[w])
            valid_ref[lane5[w]] = field(F_REAL, i4[w])
        for w in range(width):
            k5 = pl.multiple_of(field(F_K0, i4[w]), bk)
            contrib = _dot(a_scr[w], stack_heads(v_ref[0, pl.ds(k5, bk), :]))
            acc_scr[lane5[w]] = jnp.where(first5[w], contrib, acc_scr[lane5[w]] + contrib)
        slot = trip & 1
        for w in range(width):
            restart = field(F_FIRST, i3[w]) == 1
            lane4 = field(F_LANE, i3[w])
            lowest = None
            for h in range(2):
                cols = slice(h * bk, (h + 1) * bk)
                later = jnp.where(restart, 0.0, later_scr[lane4, :, cols])
                between = sums_scr[w, h, :, :bk] + later
                later = later + sums_scr[w, h, :, bk:]
                later_scr[lane4, :, cols] = later
                lowest = later if lowest is None else jnp.maximum(lowest, later)
                a_scr[w, :, cols] = jnp.exp2(lb_scr[w, slot, :, cols] + between).astype(BF16)
            dead_ref[lane4] = jnp.where(jnp.max(lowest) < DEAD_LOG2, field(F_QB, i3[w]), -1)
        for w in range(width):
            for h in range(2):
                sums_scr[w, h] = _dot(x2_scr[w, h], suffix_scr[...])
        for w in range(width):
            bias = bias_scr[field(F_BIAS, i1[w])]
            for h in range(2):
                cols = slice(h * bk, (h + 1) * bk)
                z = z_scr[w, :, cols] + bias
                soft = jnp.log2(1.0 + jnp.exp2(-jnp.abs(z)))
                log_beta = jnp.minimum(z, 0.0) - soft
                hi, lo = _split2(log_beta - z)
                lb_scr[w, slot, :, cols] = log_beta
                x2_scr[w, h] = jnp.concatenate([hi, lo], axis=1)
        for w in range(width):
            q1 = pl.multiple_of(field(F_Q0, i0[w]), bq)
            k1 = pl.multiple_of(field(F_K0, i0[w]), bk)
            z_scr[w] = _dot_nt(q_ref[0, pl.ds(q1, bq), :], stack_heads(k_ref[0, pl.ds(k1, bk), :]))
        return (trip + 1, idle, *i0, *i1, *i2, *i3)

    zero = jnp.int32(0)
    lax.while_loop(keep_going, body, (zero,) * (2 + (ATTN_STAGES - 1) * width))
    for ln in range(n_lanes):
        @pl.when(valid_ref[ln] == 1)
        def _():
            o_ref[0, pl.ds(pl.multiple_of(qprev_ref[ln], bq), bq), :] = acc_scr[ln].astype(BF16)


def _sb_attention(q, k, v, bq, bk, q_start):
    b, tq, d = q.shape
    tk = k.shape[1]
    assert tq % bq == 0 and bq % bk == 0 and q_start % bk == 0 and tk == q_start + tq
    per_q = bq // bk
    items, n_lanes, lane_start, lane_end = _attn_schedule(tq // bq, per_q, q_start // bk, bq, bk)
    lane_smem = pltpu.SMEM((n_lanes,), jnp.int32)
    width = max(1, min(ATTN_ITEMS_PER_TRIP, n_lanes // (ATTN_STAGES - 1)))
    table = jnp.asarray(np.asarray(items, np.int32).T.reshape(-1))
    qspec = pl.BlockSpec((1, tq, PAIR), lambda bi, pi, tab: (bi, 0, pi))
    kspec = pl.BlockSpec((1, tk, PAIR), lambda bi, pi, tab: (bi, 0, pi))
    grid_spec = pltpu.PrefetchScalarGridSpec(
        num_scalar_prefetch=1,
        grid=(b, d // PAIR),
        in_specs=[qspec, kspec, kspec],
        out_specs=qspec,
        scratch_shapes=[pltpu.VMEM((width, bq, 2 * bk), F32), pltpu.VMEM((width, 2, bq, 2 * bk), F32),
                        pltpu.VMEM((width, 2, bq, 2 * bk), BF16), pltpu.VMEM((width, 2, bq, 2 * bk), F32),
                        pltpu.VMEM((width, bq, 2 * bk), BF16), pltpu.VMEM((n_lanes, bq, 2 * bk), F32),
                        pltpu.VMEM((n_lanes, bq, PAIR), F32), pltpu.VMEM((per_q + 1, bq, bk), F32),
                        pltpu.VMEM((2 * bk, 2 * bk), BF16), lane_smem, lane_smem, lane_smem, lane_smem],
    )
    return pl.pallas_call(
        functools.partial(_sb_attn_kernel, bq=bq, bk=bk, tab_len=len(items), n_lanes=n_lanes, width=width,
                          lane_start=tuple(lane_start), lane_end=tuple(lane_end)),
        grid_spec=grid_spec,
        out_shape=jax.ShapeDtypeStruct((b, tq, d), BF16),
        compiler_params=_params("parallel", "parallel"),
        name="sb_attention",
    )(table, q, k, v)


def _mm_res_ln_kernel(a_ref, w_ref, x_ref, g_ref, b_ref, o_ref):
    y = _dot(a_ref[...], w_ref[...])
    o_ref[...] = _layer_norm(DEEPNORM_ALPHA * x_ref[...] + y, g_ref[...], b_ref[...])


def _mm_res_ln(a, w_bf16, x, g, b, tm):
    t, d = x.shape
    kdim = a.shape[1]
    vec = pl.BlockSpec((1, d), lambda i: (0, 0))
    return pl.pallas_call(
        _mm_res_ln_kernel,
        grid=(t // tm,),
        in_specs=[pl.BlockSpec((tm, kdim), lambda i: (i, 0)),
                  pl.BlockSpec((kdim, d), lambda i: (0, 0)),
                  pl.BlockSpec((tm, d), lambda i: (i, 0)), vec, vec],
        out_specs=pl.BlockSpec((tm, d), lambda i: (i, 0)),
        out_shape=jax.ShapeDtypeStruct((t, d), F32),
        compiler_params=_params("parallel"),
        name="attn_out_ln",
    )(a, w_bf16, x, g.reshape(1, d), b.reshape(1, d))


def _swiglu_step(xb, wg_ref, wu_ref, wd_ref):
    gate = _dot(xb, wg_ref[...])
    up = _dot(xb, wu_ref[...])
    hidden = gate * jax.nn.sigmoid(gate) * up
    return _dot(hidden.astype(BF16), wd_ref[...])


def _ffn_ln_kernel(x_ref, wg_ref, wu_ref, wd_ref, g_ref, b_ref, o_ref, xb_scr, acc_scr):
    j = pl.program_id(1)

    @pl.when(j == 0)
    def _():
        xb_scr[...] = x_ref[...].astype(BF16)
        acc_scr[...] = jnp.zeros_like(acc_scr)

    acc_scr[...] += _swiglu_step(xb_scr[...], wg_ref, wu_ref, wd_ref)

    @pl.when(j == pl.num_programs(1) - 1)
    def _():
        o_ref[...] = _layer_norm(DEEPNORM_ALPHA * x_ref[...] + acc_scr[...], g_ref[...], b_ref[...])


def _ffn_ln(x, w_gate_up_bf16, w_down_bf16, g, b, tm, tn):
    t, d = x.shape
    f = w_down_bf16.shape[0]
    nj = f // tn
    vec = pl.BlockSpec((1, d), lambda i, j: (0, 0))
    return pl.pallas_call(
        _ffn_ln_kernel,
        grid=(t // tm, nj),
        in_specs=[pl.BlockSpec((tm, d), lambda i, j: (i, 0)),
                  pl.BlockSpec((d, tn), lambda i, j: (0, j)),
                  pl.BlockSpec((d, tn), lambda i, j: (0, nj + j)),
                  pl.BlockSpec((tn, d), lambda i, j: (j, 0)), vec, vec],
        out_specs=pl.BlockSpec((tm, d), lambda i, j: (i, 0)),
        out_shape=jax.ShapeDtypeStruct((t, d), F32),
        scratch_shapes=[pltpu.VMEM((tm, d), BF16), pltpu.VMEM((tm, d), F32)],
        compiler_params=_params("parallel", "arbitrary"),
        name="dense_swiglu_ln",
    )(x, w_gate_up_bf16, w_gate_up_bf16, w_down_bf16, g.reshape(1, d), b.reshape(1, d))


def _moe_ffn_kernel(te_ref, nt_ref, x_ref, wg_ref, wu_ref, wd_ref, o_ref, xb_scr, acc_scr):
    i = pl.program_id(0)
    j = pl.program_id(1)
    used = i < nt_ref[0]

    @pl.when(used & (j == 0))
    def _():
        xb_scr[...] = x_ref[...].astype(BF16)
        acc_scr[...] = jnp.zeros_like(acc_scr)

    @pl.when(used)
    def _():
        acc_scr[...] += _swiglu_step(xb_scr[...], wg_ref.at[0], wu_ref.at[0], wd_ref.at[0])

    @pl.when(j == pl.num_programs(1) - 1)
    def _():
        o_ref[...] = jnp.where(used, acc_scr[...], 0.0)


def _moe_ffn(tile_expert, n_tiles_used, xg, w_gate_up_bf16, w_down_bf16, tm, tn):
    n, d = xg.shape
    f = w_down_bf16.shape[1]
    nj = f // tn
    grid_spec = pltpu.PrefetchScalarGridSpec(
        num_scalar_prefetch=2,
        grid=(n // tm, nj),
        in_specs=[pl.BlockSpec((tm, d), lambda i, j, te, nt: (i, 0)),
                  pl.BlockSpec((1, d, tn), lambda i, j, te, nt: (te[i], 0, j)),
                  pl.BlockSpec((1, d, tn), lambda i, j, te, nt: (te[i], 0, nj + j)),
                  pl.BlockSpec((1, tn, d), lambda i, j, te, nt: (te[i], j, 0))],
        out_specs=pl.BlockSpec((tm, d), lambda i, j, te, nt: (i, 0)),
        scratch_shapes=[pltpu.VMEM((tm, d), BF16), pltpu.VMEM((tm, d), F32)],
    )
    return pl.pallas_call(
        _moe_ffn_kernel,
        grid_spec=grid_spec,
        out_shape=jax.ShapeDtypeStruct((n, d), F32),
        compiler_params=_params("parallel", "arbitrary"),
        name="expert_swiglu",
    )(tile_expert, n_tiles_used, xg, w_gate_up_bf16, w_gate_up_bf16, w_down_bf16)


def _row_copy(src_hbm, dst_ref, sem, src_row, dst_row):
    return pltpu.make_async_copy(src_hbm.at[pl.ds(src_row, 1)], dst_ref.at[pl.ds(dst_row, 1)], sem)


def _start_row_gather(idx_ref, src_hbm, dst_ref, sem, n):
    def start(i, c):
        for p in range(2):
            _row_copy(src_hbm, dst_ref, sem, idx_ref[0, 0, 2 * i + p], 2 * i + p).start(priority=p)
        return c

    lax.fori_loop(0, n // 2, start, 0, unroll=DMA_ISSUE_UNROLL // 2)


def _wait_row_gather(src_hbm, dst_ref, sem, n):
    pltpu.make_async_copy(src_hbm.at[pl.ds(0, n)], dst_ref, sem).wait()


def _scatter_copy(x_ref, dst_hbm, sem, src_row, dst_row):
    return pltpu.make_async_copy(x_ref.at[pl.ds(src_row, 1)], dst_hbm.at[pl.ds(dst_row, 1)], sem)


def _scatter_kernel(p0_ref, p1_ref, x_ref, dst_in_hbm_unused, dst_hbm, sem, *, tm):
    def start(r, c):
        _scatter_copy(x_ref, dst_hbm, sem, r, p0_ref[0, 0, r]).start(priority=0)
        _scatter_copy(x_ref, dst_hbm, sem, r, p1_ref[0, 0, r]).start(priority=1)
        return c

    lax.fori_loop(0, tm, start, 0, unroll=DMA_ISSUE_UNROLL)
    for _ in range(2):
        pltpu.make_async_copy(x_ref, dst_hbm.at[pl.ds(0, tm)], sem).wait()


def _scatter_rows(x, pos0, pos1, dst, tm):
    t, d = x.shape
    idx = pl.BlockSpec((1, 1, tm), lambda i: (i, 0, 0), memory_space=pltpu.SMEM)
    return pl.pallas_call(
        functools.partial(_scatter_kernel, tm=tm),
        grid=(t // tm,),
        in_specs=[idx, idx, pl.BlockSpec((tm, d), lambda i: (i, 0)), pl.BlockSpec(memory_space=pl.ANY)],
        out_specs=pl.BlockSpec(memory_space=pl.ANY),
        out_shape=jax.ShapeDtypeStruct(dst.shape, dst.dtype),
        input_output_aliases={3: 0},
        scratch_shapes=[pltpu.SemaphoreType.DMA(())],
        compiler_params=_params("arbitrary"),
        name="scatter_rows",
    )(pos0.reshape(-1, 1, tm), pos1.reshape(-1, 1, tm), x, dst)


def _combine_ln_kernel(p0_ref, p1_ref, p0_next_ref, p1_next_ref, x_ref, route_ref, y_hbm, g_ref, b_ref,
                       op_ref, os_ref, a_scr, b_scr, sems, *, tm, n_prompt_tiles):
    i = pl.program_id(0)
    slot = i % 2

    def start(p0, p1, s):
        _start_row_gather(p0, y_hbm, a_scr.at[s], sems.at[0, s], tm)
        _start_row_gather(p1, y_hbm, b_scr.at[s], sems.at[1, s], tm)

    @pl.when(i == 0)
    def _():
        start(p0_ref, p1_ref, 0)

    @pl.when(i + 1 < pl.num_programs(0))
    def _():
        start(p0_next_ref, p1_next_ref, 1 - slot)

    _wait_row_gather(y_hbm, a_scr.at[slot], sems.at[0, slot], tm)
    _wait_row_gather(y_hbm, b_scr.at[slot], sems.at[1, slot], tm)
    moe = route_ref[:, 2:3] * a_scr[slot] + route_ref[:, 3:4] * b_scr[slot]
    out = _layer_norm(DEEPNORM_ALPHA * x_ref[...] + moe, g_ref[...], b_ref[...])

    @pl.when(i < n_prompt_tiles)
    def _():
        op_ref[...] = out

    @pl.when(i >= n_prompt_tiles)
    def _():
        os_ref[...] = out


def _combine_ln(x, route, y_sorted, pos0, pos1, g, b, tm, n_prompt):
    t, d = x.shape
    npt = n_prompt // tm
    nst = (t - n_prompt) // tm
    n_tiles = npt + nst
    idx = pl.BlockSpec((1, 1, tm), lambda i: (i, 0, 0), memory_space=pltpu.SMEM)
    idx_next = pl.BlockSpec((1, 1, tm), lambda i: (jnp.minimum(i + 1, n_tiles - 1), 0, 0), memory_space=pltpu.SMEM)
    vec = pl.BlockSpec((1, d), lambda i: (0, 0))
    p0, p1 = pos0.reshape(-1, 1, tm), pos1.reshape(-1, 1, tm)
    return pl.pallas_call(
        functools.partial(_combine_ln_kernel, tm=tm, n_prompt_tiles=npt),
        grid=(n_tiles,),
        in_specs=[idx, idx, idx_next, idx_next, pl.BlockSpec((tm, d), lambda i: (i, 0)),
                  pl.BlockSpec((tm, LANES), lambda i: (i, 0)), pl.BlockSpec(memory_space=pl.ANY), vec, vec],
        out_specs=[pl.BlockSpec((tm, d), lambda i: (jnp.minimum(i, npt - 1), 0)),
                   pl.BlockSpec((tm, d), lambda i: (jnp.maximum(i - npt, 0), 0))],
        out_shape=[jax.ShapeDtypeStruct((n_prompt, d), F32), jax.ShapeDtypeStruct((t - n_prompt, d), F32)],
        scratch_shapes=[pltpu.VMEM((2, tm, d), F32), pltpu.VMEM((2, tm, d), F32), pltpu.SemaphoreType.DMA((2, 2))],
        compiler_params=_params("arbitrary"),
        name="moe_combine_ln",
    )(p0, p1, p0, p1, x, route, y_sorted, g.reshape(1, d), b.reshape(1, d))


def _rwkv_proj_kernel(x_ref, prev_ref, mu_ref, wr_ref, wk_ref, wv_ref, w1_ref, w2_ref, a1_ref, a2_ref,
                      g1_ref, g2_ref, vec_ref,
                      r_ref, lw_ref, k_ref, v_ref, kk_ref, b_ref, g_ref, bonus_ref, *, seg):
    x = x_ref[...]
    tm, d = x.shape
    rolled = pltpu.roll(x, 1, 0).reshape(tm // seg, seg, d)
    first = lax.broadcasted_iota(jnp.int32, (tm // seg, seg, d), 1) == 0
    x_prev = jnp.where(first, prev_ref[...], rolled).reshape(tm, d)
    xx = x_prev - x

    def mixed(i):
        return (x + xx * mu_ref[i:i + 1, :]).astype(BF16)

    r = _dot(mixed(0), wr_ref[...])
    k = _dot(mixed(2), wk_ref[...])
    v = _dot(mixed(3), wv_ref[...])
    w0, a0, k_k, k_a, r_k = (vec_ref[i:i + 1, :] for i in range(5))
    wl = w0 + _dot(jnp.tanh(_dot(mixed(1), w1_ref[...])).astype(BF16), w2_ref[...])
    softplus_neg = jnp.maximum(-wl, 0.0) + jnp.log(1.0 + jnp.exp(-jnp.abs(wl)))
    lw_ref[...] = -jnp.exp(-softplus_neg - 0.5)
    a = jax.nn.sigmoid(a0 + _dot(_dot(mixed(4), a1_ref[...]).astype(BF16), a2_ref[...]))
    g_ref[...] = _dot(jax.nn.sigmoid(_dot(mixed(5), g1_ref[...])).astype(BF16), g2_ref[...])
    ones_pair = _head_ones(PAIR)
    kk = k * k_k
    kk = kk / jnp.maximum(jnp.sqrt(_head_sum(kk * kk, ones_pair)), 1e-12)
    k = k * (1.0 + (a - 1.0) * k_a)
    r_ref[...] = r
    k_ref[...] = k
    v_ref[...] = v
    kk_ref[...] = kk
    b_ref[...] = kk * a
    bonus_ref[...] = _head_sum(r * k * r_k, ones_pair) * v


def _rwkv_proj(x, prev_rows, mu, w_r, w_k, w_v, w1, w2, a1, a2, g1, g2, vecs, tm, seg):
    t, d = x.shape
    row = pl.BlockSpec((tm, d), lambda i: (i, 0))

    def full(a):
        return pl.BlockSpec(a.shape, lambda i: (0,) * a.ndim)

    consts = (mu, w_r, w_k, w_v, w1, w2, a1, a2, g1, g2, vecs)
    return pl.pallas_call(
        functools.partial(_rwkv_proj_kernel, seg=seg),
        grid=(t // tm,),
        in_specs=[row, pl.BlockSpec((tm // seg, 1, d), lambda i: (i, 0, 0))] + [full(c) for c in consts],
        out_specs=[row] * 8,
        out_shape=[jax.ShapeDtypeStruct((t, d), F32)] * 8,
        compiler_params=_params("parallel"),
        name="rwkv_proj",
    )(x, prev_rows, *consts)


def _unit_lower_inverses(lows, limit):
    n = lows[0].shape[0]
    r = lax.broadcasted_iota(jnp.int32, (n, n), 0)
    c = lax.broadcasted_iota(jnp.int32, (n, n), 1)
    eye = (r == c).astype(F32)
    base = (r >> 3) == (c >> 3)
    ps = [jnp.where(base, -low, 0.0) for low in lows]
    invs = [eye + p for p in ps]
    for _ in range(2):
        ps = [_dot2(p, p) for p in ps]
        invs = [inv + _dot2(inv, p) for inv, p in zip(invs, ps)]
    shift = 3
    while (1 << shift) < limit:
        sel = ((r >> (shift + 1)) == (c >> (shift + 1))) & ((r >> shift) != (c >> shift))
        mids = [_dot2(inv, jnp.where(sel, low, 0.0)) for inv, low in zip(invs, lows)]
        invs = [inv - _dot2(mid, inv) for inv, mid in zip(invs, mids)]
        shift += 1
    return invs


def _rwkv_scan_kernel(r_ref, lw_ref, k_ref, v_ref, kk_ref, b_ref, h0_ref, o_ref, hT_ref, h_scr, *, n_chunks, n_pairs):
    cs = SCAN_CHUNK
    tb = pl.program_id(2)

    @pl.when(tb == 0)
    def _():
        h_scr[...] = h0_ref[0]

    lane = lax.broadcasted_iota(jnp.int32, (cs, PAIR), 1)
    row = lax.broadcasted_iota(jnp.int32, (cs, PAIR), 0)
    head_a = lane < HEAD_DIM
    strict = (lane & (HEAD_DIM - 1)) < row
    incl = (lane & (HEAD_DIM - 1)) <= row
    tri = (lax.broadcasted_iota(jnp.int32, (cs, cs), 1) <= lax.broadcasted_iota(jnp.int32, (cs, cs), 0)).astype(BF16)
    pr = lax.broadcasted_iota(jnp.int32, (PAIR, PAIR), 0) >> 6
    pc = lax.broadcasted_iota(jnp.int32, (PAIR, PAIR), 1) >> 6
    same_head = pr == pc

    def stack(x):
        return jnp.concatenate([jnp.where(head_a, x, 0.0), jnp.where(head_a, 0.0, x)], axis=0)

    def each(fn, *lists):
        return [fn(*xs) for xs in zip(*lists)]

    def cumulative(lw):
        l1 = lw.astype(BF16)
        rem = lw - l1.astype(F32)
        l2 = rem.astype(BF16)
        l3 = (rem - l2.astype(F32)).astype(BF16)
        parts = _dot(tri, jnp.concatenate([l1, l2, l3], axis=1))
        return parts[:, :PAIR] + parts[:, PAIR:2 * PAIR] + parts[:, 2 * PAIR:]

    def chunk(ci, carry):
        sl = pl.ds(pl.multiple_of(ci * cs, cs), cs)
        lanes = [slice(p * PAIR, (p + 1) * PAIR) for p in range(n_pairs)]
        r, lw, k, v, kk, b = ([ref[0, sl, ln] for ln in lanes] for ref in (r_ref, lw_ref, k_ref, v_ref, kk_ref, b_ref))
        h = [h_scr[p] for p in range(n_pairs)]
        cum = each(cumulative, lw)
        total = [c[cs - 1:cs, :] for c in cum]
        r_hat = each(lambda x, c: x * jnp.exp(c), r, cum)
        kk_hat = each(lambda x, c, l: x * jnp.exp(c - l), kk, cum, lw)
        k_hat = each(lambda x, c: x * jnp.exp(-c), k, cum)
        b_hat = each(lambda x, c: x * jnp.exp(-c), b, cum)
        gram = each(lambda kh, rh, bh, k2: _dot2_nt(jnp.concatenate([kh, rh], axis=0),
                                                    jnp.concatenate([stack(bh), stack(k2)], axis=0)),
                    kk_hat, r_hat, b_hat, k_hat)
        l_b = [jnp.where(strict, g[:cs, :PAIR], 0.0) for g in gram]
        l_k = [jnp.where(strict, g[:cs, PAIR:], 0.0) for g in gram]
        a_rb = [jnp.where(incl, g[cs:, :PAIR], 0.0) for g in gram]
        a_rk = [jnp.where(incl, g[cs:, PAIR:], 0.0) for g in gram]
        v_st = each(stack, v)
        w = each(lambda kh, lk, hh, vs: _dot2(jnp.concatenate([kh, lk], axis=1), jnp.concatenate([hh, vs], axis=0)),
                 kk_hat, l_k, h, v_st)
        inv = _unit_lower_inverses(each(stack, l_b), cs)
        u_st = each(lambda t_, w_: _dot2(t_, stack(w_)), inv, w)
        o = each(lambda rh, ak, ab, hh, vs, us: _dot2(jnp.concatenate([rh, ak, -ab], axis=1),
                                                      jnp.concatenate([hh, vs, us], axis=0)),
                 r_hat, a_rk, a_rb, h, v_st, u_st)
        for ln, o_p in zip(lanes, o):
            o_ref[0, sl, ln] = o_p
        u = [us[:cs] + us[cs:] for us in u_st]
        to_end = each(lambda t_, c: jnp.exp(t_ - c), total, cum)
        upd = each(lambda k_, b_, e_, v_, u_: _dot2_tn(jnp.concatenate([k_ * e_, -(b_ * e_)], axis=0),
                                                       jnp.concatenate([v_, u_], axis=0)),
                   k, b, to_end, v, u)
        for p in range(n_pairs):
            decay_col = jnp.transpose(jnp.broadcast_to(jnp.exp(total[p]), (PAIR, PAIR)))
            h_scr[p] = decay_col * h[p] + jnp.where(same_head, upd[p], 0.0)
        return carry

    lax.fori_loop(0, n_chunks, chunk, 0)

    @pl.when(tb == pl.num_programs(2) - 1)
    def _():
        hT_ref[0] = h_scr[...]


def _rwkv_scan(r, lw, k, v, kk, b, h0, tb):
    bsz, t, d = r.shape
    npair = d // PAIR
    group = math.gcd(npair, SCAN_PAIRS_PER_STEP)
    seq = pl.BlockSpec((1, tb, group * PAIR), lambda bi, pi, ti: (bi, ti, pi))
    st = pl.BlockSpec((1, group, PAIR, PAIR), lambda bi, pi, ti: (bi, pi, 0, 0))
    return pl.pallas_call(
        functools.partial(_rwkv_scan_kernel, n_chunks=tb // SCAN_CHUNK, n_pairs=group),
        grid=(bsz, npair // group, t // tb),
        in_specs=[seq] * 6 + [st],
        out_specs=[seq, st],
        out_shape=[jax.ShapeDtypeStruct((bsz, t, d), F32), jax.ShapeDtypeStruct((bsz, npair, PAIR, PAIR), F32)],
        scratch_shapes=[pltpu.VMEM((group, PAIR, PAIR), F32)],
        compiler_params=_params("parallel", "parallel", "arbitrary"),
        name="rwkv_scan",
    )(r, lw, k, v, kk, b, h0)


def _rwkv_out_kernel(o_ref, g_ref, bonus_ref, x_ref, wo_ref, vec_ref, dst_hbm_unused, out_ref):
    o = o_ref[...]
    ones_pair = _head_ones(PAIR)
    mean = _head_sum(o, ones_pair) * (1.0 / HEAD_DIM)
    c = o - mean
    var = _head_sum(c * c, ones_pair) * (1.0 / HEAD_DIM)
    on = c * lax.rsqrt(var + GN_EPS) * vec_ref[0:1, :] + vec_ref[1:2, :]
    y = _dot(((on + bonus_ref[...]) * g_ref[...]).astype(BF16), wo_ref[...])
    out_ref[...] = _layer_norm(DEEPNORM_ALPHA * x_ref[...] + y, vec_ref[2:3, :], vec_ref[3:4, :])


def _rwkv_out(o, g, bonus, x, w_o_bf16, vecs, dst, n_rows, row_offset, tm):
    t, d = x.shape
    row = pl.BlockSpec((tm, d), lambda i: (i, 0))
    off = row_offset // tm
    return pl.pallas_call(
        _rwkv_out_kernel,
        grid=(t // tm,),
        in_specs=[row, row, row, row, pl.BlockSpec((d, d), lambda i: (0, 0)),
                  pl.BlockSpec(vecs.shape, lambda i: (0, 0)), pl.BlockSpec(memory_space=pl.ANY)],
        out_specs=pl.BlockSpec((tm, d), lambda i: (off + i, 0)),
        out_shape=jax.ShapeDtypeStruct((n_rows, d), F32),
        input_output_aliases={6: 0},
        compiler_params=_params("parallel"),
        name="rwkv_out_ln",
    )(o, g, bonus, x, w_o_bf16, vecs, dst)


def _router_kernel(x_ref, w_ref, o_ref, *, n_experts):
    logits = _dot3(x_ref[...], w_ref[...])
    lane = lax.broadcasted_iota(jnp.int32, logits.shape, 1)
    lane_f = lane.astype(F32)
    lowest = jnp.float32(-3.0e38)
    logits = jnp.where(lane < n_experts, logits, lowest)
    m1 = jnp.max(logits, axis=1, keepdims=True)
    i1 = jnp.min(jnp.where(logits == m1, lane_f, float(LANES)), axis=1, keepdims=True)
    rest = jnp.where(lane_f == i1, lowest, logits)
    m2 = jnp.max(rest, axis=1, keepdims=True)
    i2 = jnp.min(jnp.where(rest == m2, lane_f, float(LANES)), axis=1, keepdims=True)
    e = jnp.exp(m2 - m1)
    g1 = 1.0 / (1.0 + e)
    g2 = e * g1
    out = jnp.where(lane == 0, i1, 0.0)
    out = jnp.where(lane == 1, i2, out)
    out = jnp.where(lane == 2, g1, out)
    o_ref[...] = jnp.where(lane == 3, g2, out)


def _router(x, w_router, tm):
    t, d = x.shape
    e = w_router.shape[1]
    w_pad = jnp.zeros((d, LANES), F32).at[:, :e].set(w_router)
    return pl.pallas_call(
        functools.partial(_router_kernel, n_experts=e),
        grid=(t // tm,),
        in_specs=[pl.BlockSpec((tm, d), lambda i: (i, 0)), pl.BlockSpec((d, LANES), lambda i: (0, 0))],
        out_specs=pl.BlockSpec((tm, LANES), lambda i: (i, 0)),
        out_shape=jax.ShapeDtypeStruct((t, LANES), F32),
        compiler_params=_params("parallel"),
        name="router_top2",
    )(x, w_pad)


def _ffn_chunk(f, cap=1536):
    best = LANES
    for tn in range(LANES, cap + 1, LANES):
        if f % tn == 0:
            best = tn
    return best


def _row_tile(t, cap):
    tm = cap
    while t % tm:
        tm //= 2
    return tm


def _pair_states(s):
    b, h, n, _ = s.shape
    st = jnp.swapaxes(s, 2, 3).reshape(b, h // 2, 2, n, n)
    z = jnp.zeros_like(st[:, :, 0])
    top = jnp.concatenate([st[:, :, 0], z], axis=-1)
    bot = jnp.concatenate([z, st[:, :, 1]], axis=-1)
    return jnp.concatenate([top, bot], axis=-2)


def _unpair_states(hp):
    b, p, _, _ = hp.shape
    n = HEAD_DIM
    st = jnp.stack([hp[:, :, :n, :n], hp[:, :, n:, n:]], axis=2).reshape(b, 2 * p, n, n)
    return jnp.swapaxes(st, 2, 3)


def _attention_layer(x, w_qkv, w_o, ln_g, ln_b, cache_k=None, cache_v=None):
    b, t, d = x.shape
    q, kb, vb, kf, vf = _qkv_proj(x, w_qkv, _row_tile(t, 512))
    if cache_k is None:
        ao = _sb_attention(q, kb, vb, _row_tile(t, ATTN_QUERY_BLOCK), LANES, 0)
    else:
        past = cache_k.shape[2]
        merge = lambda c: jnp.swapaxes(c.astype(BF16), 1, 2).reshape(b, past, d)
        keys = jnp.concatenate([merge(cache_k), kb], axis=1)
        vals = jnp.concatenate([merge(cache_v), vb], axis=1)
        ao = _sb_attention(q, keys, vals, t, t, past)
    x1 = _mm_res_ln(ao.reshape(b * t, d), w_o, x.reshape(b * t, d), ln_g, ln_b, _row_tile(b * t, 512))
    return x1, kf, vf


def _rwkv_layer(x, x_last, s0, p, dst, n_rows, row_offset):
    b, t, d = x.shape
    seg = SCAN_CHUNK
    xs = x.reshape(b, t // seg, seg, d)
    prev = jnp.concatenate([x_last[:, None, :], xs[:, :-1, seg - 1, :]], axis=1).reshape(b * t // seg, 1, d)
    xf = x.reshape(b * t, d)
    tm = _row_tile(b * t, 256)
    r, lw, k, v, kk, bb, g, bonus = _rwkv_proj(xf, prev, p["mu"], p["w_r"], p["w_k"], p["w_v"], p["w1"], p["w2"],
                                               p["a1"], p["a2"], p["g1"], p["g2"], p["proj_vecs"], tm, seg)
    shp = (b, t, d)
    o, h_t = _rwkv_scan(*(z.reshape(shp) for z in (r, lw, k, v, kk, bb)), _pair_states(s0), _row_tile(t, 512))
    dst = _rwkv_out(o.reshape(b * t, d), g, bonus, xf, p["w_o"], p["out_vecs"], dst, n_rows, row_offset, tm)
    return dst, _unpair_states(h_t)


def _moe_layer(x, n_prompt, w_router, w_gate_up, w_down, ln_g, ln_b, tm):
    t, d = x.shape
    e = w_router.shape[1]
    route = _router(x, w_router, tm)
    experts = jnp.concatenate([route[:, 0], route[:, 1]]).astype(jnp.int32)
    onehot = (experts[:, None] == jnp.arange(e, dtype=jnp.int32)[None, :]).astype(jnp.int32)
    counts = jnp.sum(onehot, axis=0)
    padded = ((counts + tm - 1) // tm) * tm
    ends = jnp.cumsum(padded)
    starts = ends - padded
    rank = jnp.sum(jnp.cumsum(onehot, axis=0) * onehot, axis=1) - 1
    pos = (jnp.sum(starts[None, :] * onehot, axis=1) + rank).astype(jnp.int32)
    n_rows = ((2 * t + e * (tm - 1)) // tm) * tm
    tile_start = jnp.arange(n_rows // tm, dtype=jnp.int32) * tm
    tile_expert = jnp.minimum(jnp.sum((tile_start[:, None] >= ends[None, :]).astype(jnp.int32), axis=1), e - 1)
    n_tiles_used = (ends[-1] // tm).astype(jnp.int32).reshape(1)
    xg = _scatter_rows(x, pos[:t], pos[t:], jnp.zeros((n_rows, d), F32), tm)
    y = _moe_ffn(tile_expert, n_tiles_used, xg, w_gate_up, w_down, tm, _ffn_chunk(w_down.shape[1], cap=2048))
    return _combine_ln(x, route, y, pos[:t], pos[t:], ln_g, ln_b, tm, n_prompt)


def kernel(x_prompt, x_sample, cache_k, cache_v, state_wkv, state_shift, att_w_qkv, att_w_o, ffn_w_gate_up, ffn_w_down, rwkv_mu, rwkv_w_rkv, rwkv_w0, rwkv_w1, rwkv_w2, rwkv_a0, rwkv_a1, rwkv_a2, rwkv_g1, rwkv_g2, rwkv_k_k, rwkv_k_a, rwkv_r_k, rwkv_gn_g, rwkv_gn_b, rwkv_w_o, moe_w_router, moe_w_gate_up, moe_w_down, ln_mix_g, ln_mix_b, ln_ffn_g, ln_ffn_b):
    bp, tp, d = x_prompt.shape
    bs, ts, _ = x_sample.shape
    n_prompt, n_sample = bp * tp, bs * ts
    bf = lambda a: a.astype(BF16)

    w_qkv, w_o = bf(att_w_qkv[0]), bf(att_w_o[0])
    xp1, k_p, v_p = _attention_layer(x_prompt, w_qkv, w_o, ln_mix_g[0], ln_mix_b[0])
    xs1, k_s, v_s = _attention_layer(x_sample, w_qkv, w_o, ln_mix_g[0], ln_mix_b[0], cache_k[0], cache_v[0])
    w_gu, w_dn = bf(ffn_w_gate_up[0]), bf(ffn_w_down[0])
    tn = _ffn_chunk(w_dn.shape[0])
    xp2 = _ffn_ln(xp1, w_gu, w_dn, ln_ffn_g[0], ln_ffn_b[0], _row_tile(n_prompt, 512), tn).reshape(bp, tp, d)
    xs2 = _ffn_ln(xs1, w_gu, w_dn, ln_ffn_g[0], ln_ffn_b[0], _row_tile(n_sample, 512), tn).reshape(bs, ts, d)

    p = {
        "mu": rwkv_mu[0], "w_r": bf(rwkv_w_rkv[0, 0]), "w_k": bf(rwkv_w_rkv[0, 1]), "w_v": bf(rwkv_w_rkv[0, 2]),
        "w1": bf(rwkv_w1[0]), "w2": bf(rwkv_w2[0]), "a1": bf(rwkv_a1[0]), "a2": bf(rwkv_a2[0]),
        "g1": bf(rwkv_g1[0]), "g2": bf(rwkv_g2[0]), "w_o": bf(rwkv_w_o[0]),
        "proj_vecs": jnp.stack([rwkv_w0[0], rwkv_a0[0], rwkv_k_k[0], rwkv_k_a[0], rwkv_r_k[0].reshape(d)]),
        "out_vecs": jnp.stack([rwkv_gn_g[0], rwkv_gn_b[0], ln_mix_g[1], ln_mix_b[1]]),
    }
    n_all = n_prompt + n_sample
    x3, wkv_p = _rwkv_layer(xp2, jnp.zeros((bp, d), F32), jnp.zeros((bp, d // HEAD_DIM, HEAD_DIM, HEAD_DIM), F32),
                            p, jnp.zeros((n_all, d), F32), n_all, 0)
    x3, wkv_s = _rwkv_layer(xs2, state_shift[0], state_wkv[0], p, x3, n_all, n_prompt)
    y_p, y_s = _moe_layer(x3, n_prompt, moe_w_router[0], bf(moe_w_gate_up[0]), bf(moe_w_down[0]),
                          ln_ffn_g[1], ln_ffn_b[1], _row_tile(n_sample, 512))
    return (y_p.reshape(bp, tp, d), y_s.reshape(bs, ts, d),
            k_p[None], v_p[None], wkv_p[None], xp2[:, -1][None],
            k_s[None], v_s[None], wkv_s[None], xs2[:, -1][None])
```

```python
import functools
import math

import jax
import jax.numpy as jnp
import numpy as np
from jax import lax
from jax.experimental import pallas as pl
from jax.experimental.pallas import tpu as pltpu

F32 = jnp.float32
BF16 = jnp.bfloat16

HEAD_DIM = 64
LANES = 128
PAIR = 2 * HEAD_DIM
LN_EPS = 1e-5
GN_EPS = 64e-5
DEEPNORM_ALPHA = 4.0 ** 0.25
Q_SCALE = HEAD_DIM ** -0.5 * 1.4426950408889634
SCAN_PAIRS_PER_STEP = 8
SCAN_CHUNK = 64
DMA_ISSUE_UNROLL = 8
ROW_TILE = 512
RWKV_ROW_TILE = 256
V7X_VMEM_BYTES = 64 * 1024 * 1024
V7X_VMEM_LIMIT = V7X_VMEM_BYTES - 8 * 1024 * 1024


def _params(*sem, vmem=V7X_VMEM_LIMIT):
    return pltpu.CompilerParams(dimension_semantics=sem, vmem_limit_bytes=vmem)


def _layer_norm(y, g, b):
    mu = jnp.mean(y, axis=-1, keepdims=True)
    c = y - mu
    var = jnp.mean(c * c, axis=-1, keepdims=True)
    return c * lax.rsqrt(var + LN_EPS) * g + b


def _split2(x):
    hi = x.astype(BF16)
    lo = (x - hi.astype(F32)).astype(BF16)
    return hi, lo


def _dot(a, b):
    return jnp.dot(a, b, preferred_element_type=F32)


def _dot_nt(a, b):
    return lax.dot_general(a, b, (((1,), (1,)), ((), ())), preferred_element_type=F32)


def _dot_tn(a, b):
    return lax.dot_general(a, b, (((0,), (0,)), ((), ())), preferred_element_type=F32)


def _dot3(a, b):
    ah, al = _split2(a)
    bh, bl = _split2(b)
    return _dot(jnp.concatenate([ah, ah, al], axis=1), jnp.concatenate([bh, bl, bh], axis=0))


def _dot2(a, b):
    ah, al = _split2(a)
    bb = b.astype(BF16)
    return _dot(jnp.concatenate([ah, al], axis=1), jnp.concatenate([bb, bb], axis=0))


def _dot2_nt(a, b):
    ah, al = _split2(a)
    bb = b.astype(BF16)
    return _dot_nt(jnp.concatenate([ah, al], axis=1), jnp.concatenate([bb, bb], axis=1))


def _dot2_tn(a, b):
    ah, al = _split2(a)
    bb = b.astype(BF16)
    return _dot_tn(jnp.concatenate([ah, al], axis=0), jnp.concatenate([bb, bb], axis=0))


def _head_ones(n):
    r = lax.broadcasted_iota(jnp.int32, (n, n), 0) >> 6
    c = lax.broadcasted_iota(jnp.int32, (n, n), 1) >> 6
    return (r == c).astype(BF16)


def _head_sum(x, ones_pair):
    d = x.shape[1]
    cols = [_dot2(x[:, c:c + PAIR], ones_pair) for c in range(0, d, PAIR)]
    return jnp.concatenate(cols, axis=1)


def _qkv_kernel(x_ref, wq_ref, wk_ref, wv_ref, qb_ref, kb_ref, vb_ref, kf_ref, vf_ref, *, n_heads):
    xb = x_ref[0].astype(BF16)
    qb_ref[0] = (_dot(xb, wq_ref[...]) * Q_SCALE).astype(BF16)
    for w_ref, b_ref, f_ref in ((wk_ref, kb_ref, kf_ref), (wv_ref, vb_ref, vf_ref)):
        y = _dot(xb, w_ref[...])
        b_ref[0] = y.astype(BF16)
        for h in range(n_heads):
            f_ref[0, h] = y[:, h * HEAD_DIM:(h + 1) * HEAD_DIM]


def _qkv_proj(x, w_bf16, tm):
    b, t, d = x.shape
    h = d // HEAD_DIM
    row = pl.BlockSpec((1, tm, d), lambda bi, ti: (bi, ti, 0))
    hspec = pl.BlockSpec((1, h, tm, HEAD_DIM), lambda bi, ti: (bi, 0, ti, 0))
    shp = (b, h, t, HEAD_DIM)
    return pl.pallas_call(
        functools.partial(_qkv_kernel, n_heads=h),
        grid=(b, t // tm),
        in_specs=[row] + [pl.BlockSpec((d, d), lambda bi, ti, n=n: (0, n)) for n in range(3)],
        out_specs=[row] * 3 + [hspec] * 2,
        out_shape=[jax.ShapeDtypeStruct((b, t, d), BF16)] * 3 + [jax.ShapeDtypeStruct(shp, F32)] * 2,
        compiler_params=_params("parallel", "parallel"),
        name="qkv_proj",
    )(x, w_bf16, w_bf16, w_bf16)


ATTN_STAGES = 5
ATTN_QUERY_BLOCK = 128
MASKED_SCORE = -1.0e30


DEAD_LOG2 = -160.0
F_Q0, F_K0, F_BIAS, F_FIRST, F_REAL, F_QB, F_NEXT, F_LANE = range(8)
ATTN_MAX_LANES = 16
ATTN_ITEMS_PER_TRIP = 4


def _attn_schedule(nq, per_q, past_blocks, bq, bk):
    n_lanes = 1
    while n_lanes * 2 <= min(ATTN_MAX_LANES, nq):
        n_lanes *= 2
    items = [[0, 0, 0, 1, 0, -1, 0, lane] for lane in range(n_lanes)]
    lane_start, lane_end = [], []
    for lane in range(n_lanes):
        lane_start.append(len(items))
        mine = [qb for qb in range(nq)
                if (qb % (2 * n_lanes) if qb % (2 * n_lanes) < n_lanes else 2 * n_lanes - 1 - qb % (2 * n_lanes)) == lane]
        for qb in mine:
            n_full = past_blocks + qb * per_q
            blocks = [(n_full + j, j + 1) for j in reversed(range(per_q))] + [(kb, 0) for kb in reversed(range(n_full))]
            nxt = len(items) + len(blocks)
            for i, (kb, bias_id) in enumerate(blocks):
                items.append([qb * bq, kb * bk, bias_id, int(i == 0), 1, qb, nxt, lane])
        lane_end.append(len(items))
    return items, n_lanes, lane_start, lane_end


def _sb_attn_kernel(tab_ref, q_ref, k_ref, v_ref, o_ref, z_scr, lb_scr, x2_scr, sums_scr, a_scr, later_scr,
                    acc_scr, bias_scr, suffix_scr, cur_ref, dead_ref, qprev_ref, valid_ref,
                    *, bq, bk, tab_len, n_lanes, width, lane_start, lane_end):
    per_q = bq // bk
    for ln in range(n_lanes):
        cur_ref[ln] = lane_start[ln]
        dead_ref[ln] = -1
        qprev_ref[ln] = 0
        valid_ref[ln] = 0
    r = lax.broadcasted_iota(jnp.int32, (2 * bk, 2 * bk), 0)
    c = lax.broadcasted_iota(jnp.int32, (2 * bk, 2 * bk), 1)
    r = jnp.where(r >= bk, r - bk, r)
    suffix_scr[...] = ((c >= bk) | (r > c)).astype(BF16)
    qi = lax.broadcasted_iota(jnp.int32, (bq, bk), 0)
    kj = lax.broadcasted_iota(jnp.int32, (bq, bk), 1)
    bias_scr[0] = jnp.zeros((bq, bk), F32)
    for j in range(per_q):
        bias_scr[j + 1] = jnp.where((kj + j * bk) < qi, 0.0, MASKED_SCORE)
    for scr in (z_scr, lb_scr, x2_scr, sums_scr, a_scr, later_scr, acc_scr):
        scr[...] = jnp.zeros_like(scr)
    lane_a = lax.broadcasted_iota(jnp.int32, (bk, PAIR), 1) < HEAD_DIM

    def field(f, i):
        return tab_ref[f * tab_len + i]

    def stack_heads(blk):
        zero = jnp.zeros_like(blk)
        return jnp.concatenate([jnp.where(lane_a, blk, zero), jnp.where(lane_a, zero, blk)], axis=0)

    def keep_going(c):
        return c[1] < ATTN_STAGES - 1

    def body(c):
        trip, idle = c[0], c[1]
        in_flight = [c[2 + s * width:2 + (s + 1) * width] for s in range(ATTN_STAGES - 1)]
        i1, i2, i3, i4 = in_flight
        i0, any_more = [], None
        for w in range(width):
            lane = (trip * width + w) & (n_lanes - 1)
            end = jnp.int32(lane_end[0])
            for ln in range(1, n_lanes):
                end = jnp.where(lane == ln, lane_end[ln], end)
            nxt = cur_ref[lane]
            peek = jnp.minimum(nxt, tab_len - 1)
            nxt = jnp.where((nxt < end) & (field(F_QB, peek) == dead_ref[lane]), field(F_NEXT, peek), nxt)
            more = nxt < end
            i0.append(jnp.where(more, nxt, lane))
            cur_ref[lane] = jnp.where(more, nxt + 1, nxt)
            any_more = more if any_more is None else any_more | more
        idle = jnp.where(any_more, 0, idle + 1)
        first5 = [field(F_FIRST, i) == 1 for i in i4]
        lane5 = [field(F_LANE, i) for i in i4]
        for w in range(width):
            @pl.when(first5[w] & (valid_ref[lane5[w]] == 1))
            def _():
                o_ref[0, pl.ds(pl.multiple_of(qprev_ref[lane5[w]], bq), bq), :] = acc_scr[lane5[w]].astype(BF16)

            qprev_ref[lane5[w]] = field(F_Q0, i4[w])
            valid_ref[lane5[w]] = field(F_REAL, i4[w])
        for w in range(width):
            k5 = pl.multiple_of(field(F_K0, i4[w]), bk)
            contrib = _dot(a_scr[w], stack_heads(v_ref[0, pl.ds(k5, bk), :]))
            acc_scr[lane5[w]] = jnp.where(first5[w], contrib, acc_scr[lane5[w]] + contrib)
        slot = trip & 1
        for w in range(width):
            restart = field(F_FIRST, i3[w]) == 1
            lane4 = field(F_LANE, i3[w])
            lowest = None
            for h in range(2):
                cols = slice(h * bk, (h + 1) * bk)
                later = jnp.where(restart, 0.0, later_scr[lane4, :, cols])
                between = sums_scr[w, h, :, :bk] + later
                later = later + sums_scr[w, h, :, bk:]
                later_scr[lane4, :, cols] = later
                lowest = later if lowest is None else jnp.maximum(lowest, later)
                a_scr[w, :, cols] = jnp.exp2(lb_scr[w, slot, :, cols] + between).astype(BF16)
            dead_ref[lane4] = jnp.where(jnp.max(lowest) < DEAD_LOG2, field(F_QB, i3[w]), -1)
        for w in range(width):
            for h in range(2):
                sums_scr[w, h] = _dot(x2_scr[w, h], suffix_scr[...])
        for w in range(width):
            bias = bias_scr[field(F_BIAS, i1[w])]
            for h in range(2):
                cols = slice(h * bk, (h + 1) * bk)
                z = z_scr[w, :, cols] + bias
                soft = jnp.log2(1.0 + jnp.exp2(-jnp.abs(z)))
                log_beta = jnp.minimum(z, 0.0) - soft
                hi, lo = _split2(log_beta - z)
                lb_scr[w, slot, :, cols] = log_beta
                x2_scr[w, h] = jnp.concatenate([hi, lo], axis=1)
        for w in range(width):
            q1 = pl.multiple_of(field(F_Q0, i0[w]), bq)
            k1 = pl.multiple_of(field(F_K0, i0[w]), bk)
            z_scr[w] = _dot_nt(q_ref[0, pl.ds(q1, bq), :], stack_heads(k_ref[0, pl.ds(k1, bk), :]))
        return (trip + 1, idle, *i0, *i1, *i2, *i3)

    zero = jnp.int32(0)
    lax.while_loop(keep_going, body, (zero,) * (2 + (ATTN_STAGES - 1) * width))
    for ln in range(n_lanes):
        @pl.when(valid_ref[ln] == 1)
        def _():
            o_ref[0, pl.ds(pl.multiple_of(qprev_ref[ln], bq), bq), :] = acc_scr[ln].astype(BF16)


def _sb_attention(q, k, v, bq, bk, q_start):
    b, tq, d = q.shape
    tk = k.shape[1]
    assert tq % bq == 0 and bq % bk == 0 and q_start % bk == 0 and tk == q_start + tq
    per_q = bq // bk
    items, n_lanes, lane_start, lane_end = _attn_schedule(tq // bq, per_q, q_start // bk, bq, bk)
    lane_smem = pltpu.SMEM((n_lanes,), jnp.int32)
    width = max(1, min(ATTN_ITEMS_PER_TRIP, n_lanes // (ATTN_STAGES - 1)))
    table = jnp.asarray(np.asarray(items, np.int32).T.reshape(-1))
    qspec = pl.BlockSpec((1, tq, PAIR), lambda bi, pi, tab: (bi, 0, pi))
    kspec = pl.BlockSpec((1, tk, PAIR), lambda bi, pi, tab: (bi, 0, pi))
    grid_spec = pltpu.PrefetchScalarGridSpec(
        num_scalar_prefetch=1,
        grid=(b, d // PAIR),
        in_specs=[qspec, kspec, kspec],
        out_specs=qspec,
        scratch_shapes=[pltpu.VMEM((width, bq, 2 * bk), F32), pltpu.VMEM((width, 2, bq, 2 * bk), F32),
                        pltpu.VMEM((width, 2, bq, 2 * bk), BF16), pltpu.VMEM((width, 2, bq, 2 * bk), F32),
                        pltpu.VMEM((width, bq, 2 * bk), BF16), pltpu.VMEM((n_lanes, bq, 2 * bk), F32),
                        pltpu.VMEM((n_lanes, bq, PAIR), F32), pltpu.VMEM((per_q + 1, bq, bk), F32),
                        pltpu.VMEM((2 * bk, 2 * bk), BF16), lane_smem, lane_smem, lane_smem, lane_smem],
    )
    return pl.pallas_call(
        functools.partial(_sb_attn_kernel, bq=bq, bk=bk, tab_len=len(items), n_lanes=n_lanes, width=width,
                          lane_start=tuple(lane_start), lane_end=tuple(lane_end)),
        grid_spec=grid_spec,
        out_shape=jax.ShapeDtypeStruct((b, tq, d), BF16),
        compiler_params=_params("parallel", "parallel"),
        name="sb_attention",
    )(table, q, k, v)


def _mm_res_ln_kernel(a_ref, w_ref, x_ref, g_ref, b_ref, o_ref):
    y = _dot(a_ref[...], w_ref[...])
    o_ref[...] = _layer_norm(DEEPNORM_ALPHA * x_ref[...] + y, g_ref[...], b_ref[...])


def _mm_res_ln(a, w_bf16, x, g, b, tm):
    t, d = x.shape
    kdim = a.shape[1]
    vec = pl.BlockSpec((1, d), lambda i: (0, 0))
    return pl.pallas_call(
        _mm_res_ln_kernel,
        grid=(t // tm,),
        in_specs=[pl.BlockSpec((tm, kdim), lambda i: (i, 0)),
                  pl.BlockSpec((kdim, d), lambda i: (0, 0)),
                  pl.BlockSpec((tm, d), lambda i: (i, 0)), vec, vec],
        out_specs=pl.BlockSpec((tm, d), lambda i: (i, 0)),
        out_shape=jax.ShapeDtypeStruct((t, d), F32),
        compiler_params=_params("parallel"),
        name="attn_out_ln",
    )(a, w_bf16, x, g.reshape(1, d), b.reshape(1, d))


def _swiglu_step(xb, wg_ref, wu_ref, wd_ref):
    gate = _dot(xb, wg_ref[...])
    up = _dot(xb, wu_ref[...])
    hidden = gate * jax.nn.sigmoid(gate) * up
    return _dot(hidden.astype(BF16), wd_ref[...])


def _ffn_ln_kernel(x_ref, wg_ref, wu_ref, wd_ref, g_ref, b_ref, o_ref, xb_scr, acc_scr):
    j = pl.program_id(1)

    @pl.when(j == 0)
    def _():
        xb_scr[...] = x_ref[...].astype(BF16)
        acc_scr[...] = jnp.zeros_like(acc_scr)

    acc_scr[...] += _swiglu_step(xb_scr[...], wg_ref, wu_ref, wd_ref)

    @pl.when(j == pl.num_programs(1) - 1)
    def _():
        o_ref[...] = _layer_norm(DEEPNORM_ALPHA * x_ref[...] + acc_scr[...], g_ref[...], b_ref[...])


def _ffn_ln(x, w_gate_up_bf16, w_down_bf16, g, b, tm, tn):
    t, d = x.shape
    f = w_down_bf16.shape[0]
    nj = f // tn
    vec = pl.BlockSpec((1, d), lambda i, j: (0, 0))
    return pl.pallas_call(
        _ffn_ln_kernel,
        grid=(t // tm, nj),
        in_specs=[pl.BlockSpec((tm, d), lambda i, j: (i, 0)),
                  pl.BlockSpec((d, tn), lambda i, j: (0, j)),
                  pl.BlockSpec((d, tn), lambda i, j: (0, nj + j)),
                  pl.BlockSpec((tn, d), lambda i, j: (j, 0)), vec, vec],
        out_specs=pl.BlockSpec((tm, d), lambda i, j: (i, 0)),
        out_shape=jax.ShapeDtypeStruct((t, d), F32),
        scratch_shapes=[pltpu.VMEM((tm, d), BF16), pltpu.VMEM((tm, d), F32)],
        compiler_params=_params("parallel", "arbitrary"),
        name="dense_swiglu_ln",
    )(x, w_gate_up_bf16, w_gate_up_bf16, w_down_bf16, g.reshape(1, d), b.reshape(1, d))


def _moe_ffn_kernel(te_ref, nt_ref, x_ref, wg_ref, wu_ref, wd_ref, o_ref, xb_scr, acc_scr):
    i = pl.program_id(0)
    j = pl.program_id(1)
    used = i < nt_ref[0]

    @pl.when(used & (j == 0))
    def _():
        xb_scr[...] = x_ref[...].astype(BF16)
        acc_scr[...] = jnp.zeros_like(acc_scr)

    @pl.when(used)
    def _():
        acc_scr[...] += _swiglu_step(xb_scr[...], wg_ref.at[0], wu_ref.at[0], wd_ref.at[0])

    @pl.when(j == pl.num_programs(1) - 1)
    def _():
        o_ref[...] = jnp.where(used, acc_scr[...], 0.0)


def _moe_ffn(tile_expert, n_tiles_used, xg, w_gate_up_bf16, w_down_bf16, tm, tn):
    n, d = xg.shape
    f = w_down_bf16.shape[1]
    nj = f // tn
    grid_spec = pltpu.PrefetchScalarGridSpec(
        num_scalar_prefetch=2,
        grid=(n // tm, nj),
        in_specs=[pl.BlockSpec((tm, d), lambda i, j, te, nt: (i, 0)),
                  pl.BlockSpec((1, d, tn), lambda i, j, te, nt: (te[i], 0, j)),
                  pl.BlockSpec((1, d, tn), lambda i, j, te, nt: (te[i], 0, nj + j)),
                  pl.BlockSpec((1, tn, d), lambda i, j, te, nt: (te[i], j, 0))],
        out_specs=pl.BlockSpec((tm, d), lambda i, j, te, nt: (i, 0)),
        scratch_shapes=[pltpu.VMEM((tm, d), BF16), pltpu.VMEM((tm, d), F32)],
    )
    return pl.pallas_call(
        _moe_ffn_kernel,
        grid_spec=grid_spec,
        out_shape=jax.ShapeDtypeStruct((n, d), F32),
        compiler_params=_params("parallel", "arbitrary"),
        name="expert_swiglu",
    )(tile_expert, n_tiles_used, xg, w_gate_up_bf16, w_gate_up_bf16, w_down_bf16)


def _row_copy(src_hbm, dst_ref, sem, src_row, dst_row):
    return pltpu.make_async_copy(src_hbm.at[pl.ds(src_row, 1)], dst_ref.at[pl.ds(dst_row, 1)], sem)


def _start_row_gather(idx_ref, src_hbm, dst_ref, sem, n):
    def start(i, c):
        for p in range(2):
            _row_copy(src_hbm, dst_ref, sem, idx_ref[0, 0, 2 * i + p], 2 * i + p).start(priority=p)
        return c

    lax.fori_loop(0, n // 2, start, 0, unroll=DMA_ISSUE_UNROLL // 2)


def _wait_row_gather(src_hbm, dst_ref, sem, n):
    pltpu.make_async_copy(src_hbm.at[pl.ds(0, n)], dst_ref, sem).wait()


def _scatter_copy(x_ref, dst_hbm, sem, src_row, dst_row):
    return pltpu.make_async_copy(x_ref.at[pl.ds(src_row, 1)], dst_hbm.at[pl.ds(dst_row, 1)], sem)


def _scatter_kernel(p0_ref, p1_ref, x_ref, dst_in_hbm_unused, dst_hbm, sem, *, tm):
    def start(r, c):
        _scatter_copy(x_ref, dst_hbm, sem, r, p0_ref[0, 0, r]).start(priority=0)
        _scatter_copy(x_ref, dst_hbm, sem, r, p1_ref[0, 0, r]).start(priority=1)
        return c

    lax.fori_loop(0, tm, start, 0, unroll=DMA_ISSUE_UNROLL)
    for _ in range(2):
        pltpu.make_async_copy(x_ref, dst_hbm.at[pl.ds(0, tm)], sem).wait()


def _scatter_rows(x, pos0, pos1, dst, tm):
    t, d = x.shape
    idx = pl.BlockSpec((1, 1, tm), lambda i: (i, 0, 0), memory_space=pltpu.SMEM)
    return pl.pallas_call(
        functools.partial(_scatter_kernel, tm=tm),
        grid=(t // tm,),
        in_specs=[idx, idx, pl.BlockSpec((tm, d), lambda i: (i, 0)), pl.BlockSpec(memory_space=pl.ANY)],
        out_specs=pl.BlockSpec(memory_space=pl.ANY),
        out_shape=jax.ShapeDtypeStruct(dst.shape, dst.dtype),
        input_output_aliases={3: 0},
        scratch_shapes=[pltpu.SemaphoreType.DMA(())],
        compiler_params=_params("arbitrary"),
        name="scatter_rows",
    )(pos0.reshape(-1, 1, tm), pos1.reshape(-1, 1, tm), x, dst)


def _combine_ln_kernel(p0_ref, p1_ref, p0_next_ref, p1_next_ref, x_ref, route_ref, y_hbm, g_ref, b_ref,
                       op_ref, os_ref, a_scr, b_scr, sems, *, tm, n_prompt_tiles):
    i = pl.program_id(0)
    slot = i % 2

    def start(p0, p1, s):
        _start_row_gather(p0, y_hbm, a_scr.at[s], sems.at[0, s], tm)
        _start_row_gather(p1, y_hbm, b_scr.at[s], sems.at[1, s], tm)

    @pl.when(i == 0)
    def _():
        start(p0_ref, p1_ref, 0)

    @pl.when(i + 1 < pl.num_programs(0))
    def _():
        start(p0_next_ref, p1_next_ref, 1 - slot)

    _wait_row_gather(y_hbm, a_scr.at[slot], sems.at[0, slot], tm)
    _wait_row_gather(y_hbm, b_scr.at[slot], sems.at[1, slot], tm)
    moe = route_ref[:, 2:3] * a_scr[slot] + route_ref[:, 3:4] * b_scr[slot]
    out = _layer_norm(DEEPNORM_ALPHA * x_ref[...] + moe, g_ref[...], b_ref[...])

    @pl.when(i < n_prompt_tiles)
    def _():
        op_ref[...] = out

    @pl.when(i >= n_prompt_tiles)
    def _():
        os_ref[...] = out


def _combine_ln(x, route, y_sorted, pos0, pos1, g, b, tm, n_prompt):
    t, d = x.shape
    npt = n_prompt // tm
    nst = (t - n_prompt) // tm
    n_tiles = npt + nst
    idx = pl.BlockSpec((1, 1, tm), lambda i: (i, 0, 0), memory_space=pltpu.SMEM)
    idx_next = pl.BlockSpec((1, 1, tm), lambda i: (jnp.minimum(i + 1, n_tiles - 1), 0, 0), memory_space=pltpu.SMEM)
    vec = pl.BlockSpec((1, d), lambda i: (0, 0))
    p0, p1 = pos0.reshape(-1, 1, tm), pos1.reshape(-1, 1, tm)
    return pl.pallas_call(
        functools.partial(_combine_ln_kernel, tm=tm, n_prompt_tiles=npt),
        grid=(n_tiles,),
        in_specs=[idx, idx, idx_next, idx_next, pl.BlockSpec((tm, d), lambda i: (i, 0)),
                  pl.BlockSpec((tm, LANES), lambda i: (i, 0)), pl.BlockSpec(memory_space=pl.ANY), vec, vec],
        out_specs=[pl.BlockSpec((tm, d), lambda i: (jnp.minimum(i, npt - 1), 0)),
                   pl.BlockSpec((tm, d), lambda i: (jnp.maximum(i - npt, 0), 0))],
        out_shape=[jax.ShapeDtypeStruct((n_prompt, d), F32), jax.ShapeDtypeStruct((t - n_prompt, d), F32)],
        scratch_shapes=[pltpu.VMEM((2, tm, d), F32), pltpu.VMEM((2, tm, d), F32), pltpu.SemaphoreType.DMA((2, 2))],
        compiler_params=_params("arbitrary"),
        name="moe_combine_ln",
    )(p0, p1, p0, p1, x, route, y_sorted, g.reshape(1, d), b.reshape(1, d))


def _rwkv_proj_kernel(x_ref, prev_ref, mu_ref, wr_ref, wk_ref, wv_ref, w1_ref, w2_ref, a1_ref, a2_ref,
                      g1_ref, g2_ref, vec_ref,
                      r_ref, lw_ref, k_ref, v_ref, kk_ref, b_ref, g_ref, bonus_ref, *, seg):
    x = x_ref[...]
    tm, d = x.shape
    rolled = pltpu.roll(x, 1, 0).reshape(tm // seg, seg, d)
    first = lax.broadcasted_iota(jnp.int32, (tm // seg, seg, d), 1) == 0
    x_prev = jnp.where(first, prev_ref[...], rolled).reshape(tm, d)
    xx = x_prev - x

    def mixed(i):
        return (x + xx * mu_ref[i:i + 1, :]).astype(BF16)

    r = _dot(mixed(0), wr_ref[...])
    k = _dot(mixed(2), wk_ref[...])
    v = _dot(mixed(3), wv_ref[...])
    w0, a0, k_k, k_a, r_k = (vec_ref[i:i + 1, :] for i in range(5))
    wl = w0 + _dot(jnp.tanh(_dot(mixed(1), w1_ref[...])).astype(BF16), w2_ref[...])
    softplus_neg = jnp.maximum(-wl, 0.0) + jnp.log(1.0 + jnp.exp(-jnp.abs(wl)))
    lw_ref[...] = -jnp.exp(-softplus_neg - 0.5)
    a = jax.nn.sigmoid(a0 + _dot(_dot(mixed(4), a1_ref[...]).astype(BF16), a2_ref[...]))
    g_ref[...] = _dot(jax.nn.sigmoid(_dot(mixed(5), g1_ref[...])).astype(BF16), g2_ref[...])
    ones_pair = _head_ones(PAIR)
    kk = k * k_k
    kk = kk / jnp.maximum(jnp.sqrt(_head_sum(kk * kk, ones_pair)), 1e-12)
    k = k * (1.0 + (a - 1.0) * k_a)
    r_ref[...] = r
    k_ref[...] = k
    v_ref[...] = v
    kk_ref[...] = kk
    b_ref[...] = kk * a
    bonus_ref[...] = _head_sum(r * k * r_k, ones_pair) * v


def _rwkv_proj(x, prev_rows, mu, w_r, w_k, w_v, w1, w2, a1, a2, g1, g2, vecs, tm, seg):
    t, d = x.shape
    row = pl.BlockSpec((tm, d), lambda i: (i, 0))

    def full(a):
        return pl.BlockSpec(a.shape, lambda i: (0,) * a.ndim)

    consts = (mu, w_r, w_k, w_v, w1, w2, a1, a2, g1, g2, vecs)
    return pl.pallas_call(
        functools.partial(_rwkv_proj_kernel, seg=seg),
        grid=(t // tm,),
        in_specs=[row, pl.BlockSpec((tm // seg, 1, d), lambda i: (i, 0, 0))] + [full(c) for c in consts],
        out_specs=[row] * 8,
        out_shape=[jax.ShapeDtypeStruct((t, d), F32)] * 8,
        compiler_params=_params("parallel"),
        name="rwkv_proj",
    )(x, prev_rows, *consts)


def _unit_lower_inverses(lows, limit):
    n = lows[0].shape[0]
    r = lax.broadcasted_iota(jnp.int32, (n, n), 0)
    c = lax.broadcasted_iota(jnp.int32, (n, n), 1)
    eye = (r == c).astype(F32)
    base = (r >> 3) == (c >> 3)
    ps = [jnp.where(base, -low, 0.0) for low in lows]
    invs = [eye + p for p in ps]
    for _ in range(2):
        ps = [_dot2(p, p) for p in ps]
        invs = [inv + _dot2(inv, p) for inv, p in zip(invs, ps)]
    shift = 3
    while (1 << shift) < limit:
        sel = ((r >> (shift + 1)) == (c >> (shift + 1))) & ((r >> shift) != (c >> shift))
        mids = [_dot2(inv, jnp.where(sel, low, 0.0)) for inv, low in zip(invs, lows)]
        invs = [inv - _dot2(mid, inv) for inv, mid in zip(invs, mids)]
        shift += 1
    return invs


def _rwkv_scan_kernel(r_ref, lw_ref, k_ref, v_ref, kk_ref, b_ref, h0_ref, o_ref, hT_ref, h_scr, *, n_chunks, n_pairs):
    cs = SCAN_CHUNK
    tb = pl.program_id(2)

    @pl.when(tb == 0)
    def _():
        h_scr[...] = h0_ref[0]

    lane = lax.broadcasted_iota(jnp.int32, (cs, PAIR), 1)
    row = lax.broadcasted_iota(jnp.int32, (cs, PAIR), 0)
    head_a = lane < HEAD_DIM
    strict = (lane & (HEAD_DIM - 1)) < row
    incl = (lane & (HEAD_DIM - 1)) <= row
    tri = (lax.broadcasted_iota(jnp.int32, (cs, cs), 1) <= lax.broadcasted_iota(jnp.int32, (cs, cs), 0)).astype(BF16)
    pr = lax.broadcasted_iota(jnp.int32, (PAIR, PAIR), 0) >> 6
    pc = lax.broadcasted_iota(jnp.int32, (PAIR, PAIR), 1) >> 6
    same_head = pr == pc

    def stack(x):
        return jnp.concatenate([jnp.where(head_a, x, 0.0), jnp.where(head_a, 0.0, x)], axis=0)

    def each(fn, *lists):
        return [fn(*xs) for xs in zip(*lists)]

    def cumulative(lw):
        l1 = lw.astype(BF16)
        rem = lw - l1.astype(F32)
        l2 = rem.astype(BF16)
        l3 = (rem - l2.astype(F32)).astype(BF16)
        parts = _dot(tri, jnp.concatenate([l1, l2, l3], axis=1))
        return parts[:, :PAIR] + parts[:, PAIR:2 * PAIR] + parts[:, 2 * PAIR:]

    def chunk(ci, carry):
        sl = pl.ds(pl.multiple_of(ci * cs, cs), cs)
        lanes = [slice(p * PAIR, (p + 1) * PAIR) for p in range(n_pairs)]
        r, lw, k, v, kk, b = ([ref[0, sl, ln] for ln in lanes] for ref in (r_ref, lw_ref, k_ref, v_ref, kk_ref, b_ref))
        h = [h_scr[p] for p in range(n_pairs)]
        cum = each(cumulative, lw)
        total = [c[cs - 1:cs, :] for c in cum]
        r_hat = each(lambda x, c: x * jnp.exp(c), r, cum)
        kk_hat = each(lambda x, c, l: x * jnp.exp(c - l), kk, cum, lw)
        k_hat = each(lambda x, c: x * jnp.exp(-c), k, cum)
        b_hat = each(lambda x, c: x * jnp.exp(-c), b, cum)
        gram = each(lambda kh, rh, bh, k2: _dot2_nt(jnp.concatenate([kh, rh], axis=0),
                                                    jnp.concatenate([stack(bh), stack(k2)], axis=0)),
                    kk_hat, r_hat, b_hat, k_hat)
        l_b = [jnp.where(strict, g[:cs, :PAIR], 0.0) for g in gram]
        l_k = [jnp.where(strict, g[:cs, PAIR:], 0.0) for g in gram]
        a_rb = [jnp.where(incl, g[cs:, :PAIR], 0.0) for g in gram]
        a_rk = [jnp.where(incl, g[cs:, PAIR:], 0.0) for g in gram]
        v_st = each(stack, v)
        w = each(lambda kh, lk, hh, vs: _dot2(jnp.concatenate([kh, lk], axis=1), jnp.concatenate([hh, vs], axis=0)),
                 kk_hat, l_k, h, v_st)
        inv = _unit_lower_inverses(each(stack, l_b), cs)
        u_st = each(lambda t_, w_: _dot2(t_, stack(w_)), inv, w)
        o = each(lambda rh, ak, ab, hh, vs, us: _dot2(jnp.concatenate([rh, ak, -ab], axis=1),
                                                      jnp.concatenate([hh, vs, us], axis=0)),
                 r_hat, a_rk, a_rb, h, v_st, u_st)
        for ln, o_p in zip(lanes, o):
            o_ref[0, sl, ln] = o_p
        u = [us[:cs] + us[cs:] for us in u_st]
        to_end = each(lambda t_, c: jnp.exp(t_ - c), total, cum)
        upd = each(lambda k_, b_, e_, v_, u_: _dot2_tn(jnp.concatenate([k_ * e_, -(b_ * e_)], axis=0),
                                                       jnp.concatenate([v_, u_], axis=0)),
                   k, b, to_end, v, u)
        for p in range(n_pairs):
            decay_col = jnp.transpose(jnp.broadcast_to(jnp.exp(total[p]), (PAIR, PAIR)))
            h_scr[p] = decay_col * h[p] + jnp.where(same_head, upd[p], 0.0)
        return carry

    lax.fori_loop(0, n_chunks, chunk, 0)

    @pl.when(tb == pl.num_programs(2) - 1)
    def _():
        hT_ref[0] = h_scr[...]


def _rwkv_scan(r, lw, k, v, kk, b, h0, tb):
    bsz, t, d = r.shape
    npair = d // PAIR
    group = math.gcd(npair, SCAN_PAIRS_PER_STEP)
    seq = pl.BlockSpec((1, tb, group * PAIR), lambda bi, pi, ti: (bi, ti, pi))
    st = pl.BlockSpec((1, group, PAIR, PAIR), lambda bi, pi, ti: (bi, pi, 0, 0))
    return pl.pallas_call(
        functools.partial(_rwkv_scan_kernel, n_chunks=tb // SCAN_CHUNK, n_pairs=group),
        grid=(bsz, npair // group, t // tb),
        in_specs=[seq] * 6 + [st],
        out_specs=[seq, st],
        out_shape=[jax.ShapeDtypeStruct((bsz, t, d), F32), jax.ShapeDtypeStruct((bsz, npair, PAIR, PAIR), F32)],
        scratch_shapes=[pltpu.VMEM((group, PAIR, PAIR), F32)],
        compiler_params=_params("parallel", "parallel", "arbitrary"),
        name="rwkv_scan",
    )(r, lw, k, v, kk, b, h0)


def _rwkv_out_kernel(o_ref, g_ref, bonus_ref, x_ref, wo_ref, vec_ref, dst_hbm_unused, out_ref):
    o = o_ref[...]
    ones_pair = _head_ones(PAIR)
    mean = _head_sum(o, ones_pair) * (1.0 / HEAD_DIM)
    c = o - mean
    var = _head_sum(c * c, ones_pair) * (1.0 / HEAD_DIM)
    on = c * lax.rsqrt(var + GN_EPS) * vec_ref[0:1, :] + vec_ref[1:2, :]
    y = _dot(((on + bonus_ref[...]) * g_ref[...]).astype(BF16), wo_ref[...])
    out_ref[...] = _layer_norm(DEEPNORM_ALPHA * x_ref[...] + y, vec_ref[2:3, :], vec_ref[3:4, :])


def _rwkv_out(o, g, bonus, x, w_o_bf16, vecs, dst, n_rows, row_offset, tm):
    t, d = x.shape
    row = pl.BlockSpec((tm, d), lambda i: (i, 0))
    off = row_offset // tm
    return pl.pallas_call(
        _rwkv_out_kernel,
        grid=(t // tm,),
        in_specs=[row, row, row, row, pl.BlockSpec((d, d), lambda i: (0, 0)),
                  pl.BlockSpec(vecs.shape, lambda i: (0, 0)), pl.BlockSpec(memory_space=pl.ANY)],
        out_specs=pl.BlockSpec((tm, d), lambda i: (off + i, 0)),
        out_shape=jax.ShapeDtypeStruct((n_rows, d), F32),
        input_output_aliases={6: 0},
        compiler_params=_params("parallel"),
        name="rwkv_out_ln",
    )(o, g, bonus, x, w_o_bf16, vecs, dst)


def _router_kernel(x_ref, w_ref, o_ref, *, n_experts):
    logits = _dot3(x_ref[...], w_ref[...])
    lane = lax.broadcasted_iota(jnp.int32, logits.shape, 1)
    lane_f = lane.astype(F32)
    lowest = jnp.float32(-3.0e38)
    logits = jnp.where(lane < n_experts, logits, lowest)
    m1 = jnp.max(logits, axis=1, keepdims=True)
    i1 = jnp.min(jnp.where(logits == m1, lane_f, float(LANES)), axis=1, keepdims=True)
    rest = jnp.where(lane_f == i1, lowest, logits)
    m2 = jnp.max(rest, axis=1, keepdims=True)
    i2 = jnp.min(jnp.where(rest == m2, lane_f, float(LANES)), axis=1, keepdims=True)
    e = jnp.exp(m2 - m1)
    g1 = 1.0 / (1.0 + e)
    g2 = e * g1
    out = jnp.where(lane == 0, i1, 0.0)
    out = jnp.where(lane == 1, i2, out)
    out = jnp.where(lane == 2, g1, out)
    o_ref[...] = jnp.where(lane == 3, g2, out)


def _router(x, w_router, tm):
    t, d = x.shape
    e = w_router.shape[1]
    w_pad = jnp.zeros((d, LANES), F32).at[:, :e].set(w_router)
    return pl.pallas_call(
        functools.partial(_router_kernel, n_experts=e),
        grid=(t // tm,),
        in_specs=[pl.BlockSpec((tm, d), lambda i: (i, 0)), pl.BlockSpec((d, LANES), lambda i: (0, 0))],
        out_specs=pl.BlockSpec((tm, LANES), lambda i: (i, 0)),
        out_shape=jax.ShapeDtypeStruct((t, LANES), F32),
        compiler_params=_params("parallel"),
        name="router_top2",
    )(x, w_pad)


def _ffn_chunk(f, cap=1536):
    best = LANES
    for tn in range(LANES, cap + 1, LANES):
        if f % tn == 0:
            best = tn
    return best


def _row_tile(t, cap):
    tm = cap
    while t % tm:
        tm //= 2
    return tm


def _pair_states(s):
    b, h, n, _ = s.shape
    st = jnp.swapaxes(s, 2, 3).reshape(b, h // 2, 2, n, n)
    z = jnp.zeros_like(st[:, :, 0])
    top = jnp.concatenate([st[:, :, 0], z], axis=-1)
    bot = jnp.concatenate([z, st[:, :, 1]], axis=-1)
    return jnp.concatenate([top, bot], axis=-2)


def _unpair_states(hp):
    b, p, _, _ = hp.shape
    n = HEAD_DIM
    st = jnp.stack([hp[:, :, :n, :n], hp[:, :, n:, n:]], axis=2).reshape(b, 2 * p, n, n)
    return jnp.swapaxes(st, 2, 3)


def _attention_layer(x, w_qkv, w_o, ln_g, ln_b, cache_k=None, cache_v=None):
    b, t, d = x.shape
    q, kb, vb, kf, vf = _qkv_proj(x, w_qkv, _row_tile(t, ROW_TILE))
    if cache_k is None:
        ao = _sb_attention(q, kb, vb, _row_tile(t, ATTN_QUERY_BLOCK), LANES, 0)
    else:
        past = cache_k.shape[2]
        merge = lambda c: jnp.swapaxes(c.astype(BF16), 1, 2).reshape(b, past, d)
        keys = jnp.concatenate([merge(cache_k), kb], axis=1)
        vals = jnp.concatenate([merge(cache_v), vb], axis=1)
        ao = _sb_attention(q, keys, vals, t, t, past)
    x1 = _mm_res_ln(ao.reshape(b * t, d), w_o, x.reshape(b * t, d), ln_g, ln_b, _row_tile(b * t, ROW_TILE))
    return x1, kf, vf


def _rwkv_layer(x, x_last, s0, p, dst, n_rows, row_offset):
    b, t, d = x.shape
    seg = SCAN_CHUNK
    xs = x.reshape(b, t // seg, seg, d)
    prev = jnp.concatenate([x_last[:, None, :], xs[:, :-1, seg - 1, :]], axis=1).reshape(b * t // seg, 1, d)
    xf = x.reshape(b * t, d)
    tm = _row_tile(b * t, RWKV_ROW_TILE)
    r, lw, k, v, kk, bb, g, bonus = _rwkv_proj(xf, prev, p["mu"], p["w_r"], p["w_k"], p["w_v"], p["w1"], p["w2"],
                                               p["a1"], p["a2"], p["g1"], p["g2"], p["proj_vecs"], tm, seg)
    shp = (b, t, d)
    o, h_t = _rwkv_scan(*(z.reshape(shp) for z in (r, lw, k, v, kk, bb)), _pair_states(s0), _row_tile(t, ROW_TILE))
    dst = _rwkv_out(o.reshape(b * t, d), g, bonus, xf, p["w_o"], p["out_vecs"], dst, n_rows, row_offset, tm)
    return dst, _unpair_states(h_t)


def _moe_layer(x, n_prompt, w_router, w_gate_up, w_down, ln_g, ln_b, tm):
    t, d = x.shape
    e = w_router.shape[1]
    route = _router(x, w_router, tm)
    experts = jnp.concatenate([route[:, 0], route[:, 1]]).astype(jnp.int32)
    onehot = (experts[:, None] == jnp.arange(e, dtype=jnp.int32)[None, :]).astype(jnp.int32)
    counts = jnp.sum(onehot, axis=0)
    padded = ((counts + tm - 1) // tm) * tm
    ends = jnp.cumsum(padded)
    starts = ends - padded
    rank = jnp.sum(jnp.cumsum(onehot, axis=0) * onehot, axis=1) - 1
    pos = (jnp.sum(starts[None, :] * onehot, axis=1) + rank).astype(jnp.int32)
    n_rows = ((2 * t + e * (tm - 1)) // tm) * tm
    tile_start = jnp.arange(n_rows // tm, dtype=jnp.int32) * tm
    tile_expert = jnp.minimum(jnp.sum((tile_start[:, None] >= ends[None, :]).astype(jnp.int32), axis=1), e - 1)
    n_tiles_used = (ends[-1] // tm).astype(jnp.int32).reshape(1)
    xg = _scatter_rows(x, pos[:t], pos[t:], jnp.zeros((n_rows, d), F32), tm)
    y = _moe_ffn(tile_expert, n_tiles_used, xg, w_gate_up, w_down, tm, _ffn_chunk(w_down.shape[1], cap=2048))
    return _combine_ln(x, route, y, pos[:t], pos[t:], ln_g, ln_b, tm, n_prompt)


def kernel(x_prompt, x_sample, cache_k, cache_v, state_wkv, state_shift, att_w_qkv, att_w_o, ffn_w_gate_up, ffn_w_down, rwkv_mu, rwkv_w_rkv, rwkv_w0, rwkv_w1, rwkv_w2, rwkv_a0, rwkv_a1, rwkv_a2, rwkv_g1, rwkv_g2, rwkv_k_k, rwkv_k_a, rwkv_r_k, rwkv_gn_g, rwkv_gn_b, rwkv_w_o, moe_w_router, moe_w_gate_up, moe_w_down, ln_mix_g, ln_mix_b, ln_ffn_g, ln_ffn_b):
    bp, tp, d = x_prompt.shape
    bs, ts, _ = x_sample.shape
    n_prompt, n_sample = bp * tp, bs * ts
    bf = lambda a: a.astype(BF16)

    w_qkv, w_o = bf(att_w_qkv[0]), bf(att_w_o[0])
    xp1, k_p, v_p = _attention_layer(x_prompt, w_qkv, w_o, ln_mix_g[0], ln_mix_b[0])
    xs1, k_s, v_s = _attention_layer(x_sample, w_qkv, w_o, ln_mix_g[0], ln_mix_b[0], cache_k[0], cache_v[0])
    w_gu, w_dn = bf(ffn_w_gate_up[0]), bf(ffn_w_down[0])
    tn = _ffn_chunk(w_dn.shape[0])
    xp2 = _ffn_ln(xp1, w_gu, w_dn, ln_ffn_g[0], ln_ffn_b[0], _row_tile(n_prompt, ROW_TILE), tn).reshape(bp, tp, d)
    xs2 = _ffn_ln(xs1, w_gu, w_dn, ln_ffn_g[0], ln_ffn_b[0], _row_tile(n_sample, ROW_TILE), tn).reshape(bs, ts, d)

    p = {
        "mu": rwkv_mu[0], "w_r": bf(rwkv_w_rkv[0, 0]), "w_k": bf(rwkv_w_rkv[0, 1]), "w_v": bf(rwkv_w_rkv[0, 2]),
        "w1": bf(rwkv_w1[0]), "w2": bf(rwkv_w2[0]), "a1": bf(rwkv_a1[0]), "a2": bf(rwkv_a2[0]),
        "g1": bf(rwkv_g1[0]), "g2": bf(rwkv_g2[0]), "w_o": bf(rwkv_w_o[0]),
        "proj_vecs": jnp.stack([rwkv_w0[0], rwkv_a0[0], rwkv_k_k[0], rwkv_k_a[0], rwkv_r_k[0].reshape(d)]),
        "out_vecs": jnp.stack([rwkv_gn_g[0], rwkv_gn_b[0], ln_mix_g[1], ln_mix_b[1]]),
    }
    n_all = n_prompt + n_sample
    x3, wkv_p = _rwkv_layer(xp2, jnp.zeros((bp, d), F32), jnp.zeros((bp, d // HEAD_DIM, HEAD_DIM, HEAD_DIM), F32),
                            p, jnp.zeros((n_all, d), F32), n_all, 0)
    x3, wkv_s = _rwkv_layer(xs2, state_shift[0], state_wkv[0], p, x3, n_all, n_prompt)
    y_p, y_s = _moe_layer(x3, n_prompt, moe_w_router[0], bf(moe_w_gate_up[0]), bf(moe_w_down[0]),
                          ln_ffn_g[1], ln_ffn_b[1], _row_tile(n_sample, ROW_TILE))
    return (y_p.reshape(bp, tp, d), y_s.reshape(bs, ts, d),
            k_p[None], v_p[None], wkv_p[None], xp2[:, -1][None],
            k_s[None], v_s[None], wkv_s[None], xs2[:, -1][None])
```

```python
import functools
import math

import jax
import jax.numpy as jnp
import numpy as np
from jax import lax
from jax.experimental import pallas as pl
from jax.experimental.pallas import tpu as pltpu

F32 = jnp.float32
BF16 = jnp.bfloat16

HEAD_DIM = 64
LANES = 128
PAIR = 2 * HEAD_DIM
LN_EPS = 1e-5
GN_EPS = 64e-5
DEEPNORM_ALPHA = 4.0 ** 0.25
Q_SCALE = HEAD_DIM ** -0.5 * 1.4426950408889634
SCAN_PAIRS_PER_STEP = 8
SCAN_CHUNK = 64
DMA_ISSUE_UNROLL = 8
ROW_TILE = 512
RWKV_ROW_TILE = 256
V7X_VMEM_BYTES = 64 * 1024 * 1024
V7X_VMEM_LIMIT = V7X_VMEM_BYTES - 8 * 1024 * 1024


def _params(*sem, vmem=V7X_VMEM_LIMIT):
    return pltpu.CompilerParams(dimension_semantics=sem, vmem_limit_bytes=vmem)


def _layer_norm(y, g, b):
    mu = jnp.mean(y, axis=-1, keepdims=True)
    c = y - mu
    var = jnp.mean(c * c, axis=-1, keepdims=True)
    return c * lax.rsqrt(var + LN_EPS) * g + b


def _split2(x):
    hi = x.astype(BF16)
    lo = (x - hi.astype(F32)).astype(BF16)
    return hi, lo


def _dot(a, b):
    return jnp.dot(a, b, preferred_element_type=F32)


def _dot_nt(a, b):
    return lax.dot_general(a, b, (((1,), (1,)), ((), ())), preferred_element_type=F32)


def _dot_tn(a, b):
    return lax.dot_general(a, b, (((0,), (0,)), ((), ())), preferred_element_type=F32)


def _dot3(a, b):
    ah, al = _split2(a)
    bh, bl = _split2(b)
    return _dot(jnp.concatenate([ah, ah, al], axis=1), jnp.concatenate([bh, bl, bh], axis=0))


def _dot2(a, b):
    ah, al = _split2(a)
    bb = b.astype(BF16)
    return _dot(jnp.concatenate([ah, al], axis=1), jnp.concatenate([bb, bb], axis=0))


def _dot2_nt(a, b):
    ah, al = _split2(a)
    bb = b.astype(BF16)
    return _dot_nt(jnp.concatenate([ah, al], axis=1), jnp.concatenate([bb, bb], axis=1))


def _dot2_tn(a, b):
    ah, al = _split2(a)
    bb = b.astype(BF16)
    return _dot_tn(jnp.concatenate([ah, al], axis=0), jnp.concatenate([bb, bb], axis=0))


def _head_ones(n):
    r = lax.broadcasted_iota(jnp.int32, (n, n), 0) >> 6
    c = lax.broadcasted_iota(jnp.int32, (n, n), 1) >> 6
    return (r == c).astype(BF16)


def _head_sum(x, ones_pair):
    d = x.shape[1]
    cols = [_dot2(x[:, c:c + PAIR], ones_pair) for c in range(0, d, PAIR)]
    return jnp.concatenate(cols, axis=1)


def _qkv_kernel(x_ref, wq_ref, wk_ref, wv_ref, qb_ref, kb_ref, vb_ref, kf_ref, vf_ref, *, n_heads):
    xb = x_ref[0].astype(BF16)
    qb_ref[0] = (_dot(xb, wq_ref[...]) * Q_SCALE).astype(BF16)
    for w_ref, b_ref, f_ref in ((wk_ref, kb_ref, kf_ref), (wv_ref, vb_ref, vf_ref)):
        y = _dot(xb, w_ref[...])
        b_ref[0] = y.astype(BF16)
        for h in range(n_heads):
            f_ref[0, h] = y[:, h * HEAD_DIM:(h + 1) * HEAD_DIM]


def _qkv_proj(x, w_bf16, tm):
    b, t, d = x.shape
    h = d // HEAD_DIM
    row = pl.BlockSpec((1, tm, d), lambda bi, ti: (bi, ti, 0))
    hspec = pl.BlockSpec((1, h, tm, HEAD_DIM), lambda bi, ti: (bi, 0, ti, 0))
    shp = (b, h, t, HEAD_DIM)
    return pl.pallas_call(
        functools.partial(_qkv_kernel, n_heads=h),
        grid=(b, t // tm),
        in_specs=[row] + [pl.BlockSpec((d, d), lambda bi, ti, n=n: (0, n)) for n in range(3)],
        out_specs=[row] * 3 + [hspec] * 2,
        out_shape=[jax.ShapeDtypeStruct((b, t, d), BF16)] * 3 + [jax.ShapeDtypeStruct(shp, F32)] * 2,
        compiler_params=_params("parallel", "parallel"),
        name="qkv_proj",
    )(x, w_bf16, w_bf16, w_bf16)


ATTN_STAGES = 5
ATTN_QUERY_BLOCK = 128
MASKED_SCORE = -1.0e30


DEAD_LOG2 = -160.0
F_Q0, F_K0, F_BIAS, F_FIRST, F_REAL, F_QB, F_NEXT, F_LANE = range(8)
ATTN_MAX_LANES = 16
ATTN_ITEMS_PER_TRIP = 4


def _attn_schedule(nq, per_q, past_blocks, bq, bk):
    n_lanes = 1
    while n_lanes * 2 <= min(ATTN_MAX_LANES, nq):
        n_lanes *= 2
    items = [[0, 0, 0, 1, 0, -1, 0, lane] for lane in range(n_lanes)]
    lane_start, lane_end = [], []
    for lane in range(n_lanes):
        lane_start.append(len(items))
        mine = [qb for qb in range(nq)
                if (qb % (2 * n_lanes) if qb % (2 * n_lanes) < n_lanes else 2 * n_lanes - 1 - qb % (2 * n_lanes)) == lane]
        for qb in mine:
            n_full = past_blocks + qb * per_q
            blocks = [(n_full + j, j + 1) for j in reversed(range(per_q))] + [(kb, 0) for kb in reversed(range(n_full))]
            nxt = len(items) + len(blocks)
            for i, (kb, bias_id) in enumerate(blocks):
                items.append([qb * bq, kb * bk, bias_id, int(i == 0), 1, qb, nxt, lane])
        lane_end.append(len(items))
    return items, n_lanes, lane_start, lane_end


def _sb_attn_kernel(tab_ref, q_ref, k_ref, v_ref, o_ref, z_scr, lb_scr, x2_scr, sums_scr, a_scr, later_scr,
                    acc_scr, bias_scr, suffix_scr, cur_ref, dead_ref, qprev_ref, valid_ref,
                    *, bq, bk, tab_len, n_lanes, width, lane_start, lane_end):
    per_q = bq // bk
    for ln in range(n_lanes):
        cur_ref[ln] = lane_start[ln]
        dead_ref[ln] = -1
        qprev_ref[ln] = 0
        valid_ref[ln] = 0
    r = lax.broadcasted_iota(jnp.int32, (2 * bk, 2 * bk), 0)
    c = lax.broadcasted_iota(jnp.int32, (2 * bk, 2 * bk), 1)
    r = jnp.where(r >= bk, r - bk, r)
    suffix_scr[...] = ((c >= bk) | (r > c)).astype(BF16)
    qi = lax.broadcasted_iota(jnp.int32, (bq, bk), 0)
    kj = lax.broadcasted_iota(jnp.int32, (bq, bk), 1)
    bias_scr[0] = jnp.zeros((bq, bk), F32)
    for j in range(per_q):
        bias_scr[j + 1] = jnp.where((kj + j * bk) < qi, 0.0, MASKED_SCORE)
    for scr in (z_scr, lb_scr, x2_scr, sums_scr, a_scr, later_scr, acc_scr):
        scr[...] = jnp.zeros_like(scr)
    lane_a = lax.broadcasted_iota(jnp.int32, (bk, PAIR), 1) < HEAD_DIM

    def field(f, i):
        return tab_ref[f * tab_len + i]

    def stack_heads(blk):
        zero = jnp.zeros_like(blk)
        return jnp.concatenate([jnp.where(lane_a, blk, zero), jnp.where(lane_a, zero, blk)], axis=0)

    def keep_going(c):
        return c[1] < ATTN_STAGES - 1

    def body(c):
        trip, idle = c[0], c[1]
        in_flight = [c[2 + s * width:2 + (s + 1) * width] for s in range(ATTN_STAGES - 1)]
        i1, i2, i3, i4 = in_flight
        i0, any_more = [], None
        for w in range(width):
            lane = (trip * width + w) & (n_lanes - 1)
            end = jnp.int32(lane_end[0])
            for ln in range(1, n_lanes):
                end = jnp.where(lane == ln, lane_end[ln], end)
            nxt = cur_ref[lane]
            peek = jnp.minimum(nxt, tab_len - 1)
            nxt = jnp.where((nxt < end) & (field(F_QB, peek) == dead_ref[lane]), field(F_NEXT, peek), nxt)
            more = nxt < end
            i0.append(jnp.where(more, nxt, lane))
            cur_ref[lane] = jnp.where(more, nxt + 1, nxt)
            any_more = more if any_more is None else any_more | more
        idle = jnp.where(any_more, 0, idle + 1)
        first5 = [field(F_FIRST, i) == 1 for i in i4]
        lane5 = [field(F_LANE, i) for i in i4]
        for w in range(width):
            @pl.when(first5[w] & (valid_ref[lane5[w]] == 1))
            def _():
                o_ref[0, pl.ds(pl.multiple_of(qprev_ref[lane5[w]], bq), bq), :] = acc_scr[lane5[w]].astype(BF16)

            qprev_ref[lane5[w]] = field(F_Q0, i4[w])
            valid_ref[lane5[w]] = field(F_REAL, i4[w])
        for w in range(width):
            k5 = pl.multiple_of(field(F_K0, i4[w]), bk)
            contrib = _dot(a_scr[w], stack_heads(v_ref[0, pl.ds(k5, bk), :]))
            acc_scr[lane5[w]] = jnp.where(first5[w], contrib, acc_scr[lane5[w]] + contrib)
        slot = trip & 1
        for w in range(width):
            restart = field(F_FIRST, i3[w]) == 1
            lane4 = field(F_LANE, i3[w])
            lowest = None
            for h in range(2):
                cols = slice(h * bk, (h + 1) * bk)
                later = jnp.where(restart, 0.0, later_scr[lane4, :, cols])
                between = sums_scr[w, h, :, :bk] + later
                later = later + sums_scr[w, h, :, bk:]
                later_scr[lane4, :, cols] = later
                lowest = later if lowest is None else jnp.maximum(lowest, later)
                a_scr[w, :, cols] = jnp.exp2(lb_scr[w, slot, :, cols] + between).astype(BF16)
            dead_ref[lane4] = jnp.where(jnp.max(lowest) < DEAD_LOG2, field(F_QB, i3[w]), -1)
        for w in range(width):
            for h in range(2):
                sums_scr[w, h] = _dot(x2_scr[w, h], suffix_scr[...])
        for w in range(width):
            bias = bias_scr[field(F_BIAS, i1[w])]
            for h in range(2):
                cols = slice(h * bk, (h + 1) * bk)
                z = z_scr[w, :, cols] + bias
                soft = jnp.log2(1.0 + jnp.exp2(-jnp.abs(z)))
                log_beta = jnp.minimum(z, 0.0) - soft
                hi, lo = _split2(log_beta - z)
                lb_scr[w, slot, :, cols] = log_beta
                x2_scr[w, h] = jnp.concatenate([hi, lo], axis=1)
        for w in range(width):
            q1 = pl.multiple_of(field(F_Q0, i0[w]), bq)
            k1 = pl.multiple_of(field(F_K0, i0[w]), bk)
            z_scr[w] = _dot_nt(q_ref[0, pl.ds(q1, bq), :], stack_heads(k_ref[0, pl.ds(k1, bk), :]))
        return (trip + 1, idle, *i0, *i1, *i2, *i3)

    zero = jnp.int32(0)
    lax.while_loop(keep_going, body, (zero,) * (2 + (ATTN_STAGES - 1) * width))
    for ln in range(n_lanes):
        @pl.when(valid_ref[ln] == 1)
        def _():
            o_ref[0, pl.ds(pl.multiple_of(qprev_ref[ln], bq), bq), :] = acc_scr[ln].astype(BF16)


def _sb_attention(q, k, v, bq, bk, q_start):
    b, tq, d = q.shape
    tk = k.shape[1]
    assert tq % bq == 0 and bq % bk == 0 and q_start % bk == 0 and tk == q_start + tq
    per_q = bq // bk
    items, n_lanes, lane_start, lane_end = _attn_schedule(tq // bq, per_q, q_start // bk, bq, bk)
    lane_smem = pltpu.SMEM((n_lanes,), jnp.int32)
    width = max(1, min(ATTN_ITEMS_PER_TRIP, n_lanes // (ATTN_STAGES - 1)))
    table = jnp.asarray(np.asarray(items, np.int32).T.reshape(-1))
    qspec = pl.BlockSpec((1, tq, PAIR), lambda bi, pi, tab: (bi, 0, pi))
    kspec = pl.BlockSpec((1, tk, PAIR), lambda bi, pi, tab: (bi, 0, pi))
    grid_spec = pltpu.PrefetchScalarGridSpec(
        num_scalar_prefetch=1,
        grid=(b, d // PAIR),
        in_specs=[qspec, kspec, kspec],
        out_specs=qspec,
        scratch_shapes=[pltpu.VMEM((width, bq, 2 * bk), F32), pltpu.VMEM((width, 2, bq, 2 * bk), F32),
                        pltpu.VMEM((width, 2, bq, 2 * bk), BF16), pltpu.VMEM((width, 2, bq, 2 * bk), F32),
                        pltpu.VMEM((width, bq, 2 * bk), BF16), pltpu.VMEM((n_lanes, bq, 2 * bk), F32),
                        pltpu.VMEM((n_lanes, bq, PAIR), F32), pltpu.VMEM((per_q + 1, bq, bk), F32),
                        pltpu.VMEM((2 * bk, 2 * bk), BF16), lane_smem, lane_smem, lane_smem, lane_smem],
    )
    return pl.pallas_call(
        functools.partial(_sb_attn_kernel, bq=bq, bk=bk, tab_len=len(items), n_lanes=n_lanes, width=width,
                          lane_start=tuple(lane_start), lane_end=tuple(lane_end)),
        grid_spec=grid_spec,
        out_shape=jax.ShapeDtypeStruct((b, tq, d), BF16),
        compiler_params=_params("parallel", "parallel"),
        name="sb_attention",
    )(table, q, k, v)


def _mm_res_ln_kernel(a_ref, w_ref, x_ref, g_ref, b_ref, o_ref):
    y = _dot(a_ref[...], w_ref[...])
    o_ref[...] = _layer_norm(DEEPNORM_ALPHA * x_ref[...] + y, g_ref[...], b_ref[...])


def _mm_res_ln(a, w_bf16, x, g, b, tm):
    t, d = x.shape
    kdim = a.shape[1]
    vec = pl.BlockSpec((1, d), lambda i: (0, 0))
    return pl.pallas_call(
        _mm_res_ln_kernel,
        grid=(t // tm,),
        in_specs=[pl.BlockSpec((tm, kdim), lambda i: (i, 0)),
                  pl.BlockSpec((kdim, d), lambda i: (0, 0)),
                  pl.BlockSpec((tm, d), lambda i: (i, 0)), vec, vec],
        out_specs=pl.BlockSpec((tm, d), lambda i: (i, 0)),
        out_shape=jax.ShapeDtypeStruct((t, d), F32),
        compiler_params=_params("parallel"),
        name="attn_out_ln",
    )(a, w_bf16, x, g.reshape(1, d), b.reshape(1, d))


def _swiglu_step(xb, wg_ref, wu_ref, wd_ref):
    gate = _dot(xb, wg_ref[...])
    up = _dot(xb, wu_ref[...])
    hidden = gate * jax.nn.sigmoid(gate) * up
    return _dot(hidden.astype(BF16), wd_ref[...])


def _ffn_ln_kernel(x_ref, wg_ref, wu_ref, wd_ref, g_ref, b_ref, o_ref, xb_scr, acc_scr):
    j = pl.program_id(1)

    @pl.when(j == 0)
    def _():
        xb_scr[...] = x_ref[...].astype(BF16)
        acc_scr[...] = jnp.zeros_like(acc_scr)

    acc_scr[...] += _swiglu_step(xb_scr[...], wg_ref, wu_ref, wd_ref)

    @pl.when(j == pl.num_programs(1) - 1)
    def _():
        o_ref[...] = _layer_norm(DEEPNORM_ALPHA * x_ref[...] + acc_scr[...], g_ref[...], b_ref[...])


def _ffn_ln(x, w_gate_up_bf16, w_down_bf16, g, b, tm, tn):
    t, d = x.shape
    f = w_down_bf16.shape[0]
    nj = f // tn
    vec = pl.BlockSpec((1, d), lambda i, j: (0, 0))
    once = dict(pipeline_mode=pl.Buffered(1)) if nj == 1 else {}
    return pl.pallas_call(
        _ffn_ln_kernel,
        grid=(t // tm, nj),
        in_specs=[pl.BlockSpec((tm, d), lambda i, j: (i, 0)),
                  pl.BlockSpec((d, tn), lambda i, j: (0, j), **once),
                  pl.BlockSpec((d, tn), lambda i, j: (0, nj + j), **once),
                  pl.BlockSpec((tn, d), lambda i, j: (j, 0), **once), vec, vec],
        out_specs=pl.BlockSpec((tm, d), lambda i, j: (i, 0)),
        out_shape=jax.ShapeDtypeStruct((t, d), F32),
        scratch_shapes=[pltpu.VMEM((tm, d), BF16), pltpu.VMEM((tm, d), F32)],
        compiler_params=_params("parallel", "arbitrary"),
        name="dense_swiglu_ln",
    )(x, w_gate_up_bf16, w_gate_up_bf16, w_down_bf16, g.reshape(1, d), b.reshape(1, d))


def _moe_ffn_kernel(te_ref, nt_ref, x_ref, wg_ref, wu_ref, wd_ref, o_ref, xb_scr, acc_scr):
    i = pl.program_id(0)
    j = pl.program_id(1)
    used = i < nt_ref[0]

    @pl.when(used & (j == 0))
    def _():
        xb_scr[...] = x_ref[...].astype(BF16)
        acc_scr[...] = jnp.zeros_like(acc_scr)

    @pl.when(used)
    def _():
        acc_scr[...] += _swiglu_step(xb_scr[...], wg_ref.at[0], wu_ref.at[0], wd_ref.at[0])

    @pl.when(j == pl.num_programs(1) - 1)
    def _():
        o_ref[...] = jnp.where(used, acc_scr[...], 0.0)


def _moe_ffn(tile_expert, n_tiles_used, xg, w_gate_up_bf16, w_down_bf16, tm, tn):
    n, d = xg.shape
    f = w_down_bf16.shape[1]
    nj = f // tn
    grid_spec = pltpu.PrefetchScalarGridSpec(
        num_scalar_prefetch=2,
        grid=(n // tm, nj),
        in_specs=[pl.BlockSpec((tm, d), lambda i, j, te, nt: (i, 0)),
                  pl.BlockSpec((1, d, tn), lambda i, j, te, nt: (te[i], 0, j)),
                  pl.BlockSpec((1, d, tn), lambda i, j, te, nt: (te[i], 0, nj + j)),
                  pl.BlockSpec((1, tn, d), lambda i, j, te, nt: (te[i], j, 0))],
        out_specs=pl.BlockSpec((tm, d), lambda i, j, te, nt: (i, 0)),
        scratch_shapes=[pltpu.VMEM((tm, d), BF16), pltpu.VMEM((tm, d), F32)],
    )
    return pl.pallas_call(
        _moe_ffn_kernel,
        grid_spec=grid_spec,
        out_shape=jax.ShapeDtypeStruct((n, d), F32),
        compiler_params=_params("parallel", "arbitrary"),
        name="expert_swiglu",
    )(tile_expert, n_tiles_used, xg, w_gate_up_bf16, w_gate_up_bf16, w_down_bf16)


def _row_copy(src_hbm, dst_ref, sem, src_row, dst_row):
    return pltpu.make_async_copy(src_hbm.at[pl.ds(src_row, 1)], dst_ref.at[pl.ds(dst_row, 1)], sem)


def _start_row_gather(idx_ref, src_hbm, dst_ref, sem, n):
    def start(i, c):
        for p in range(2):
            _row_copy(src_hbm, dst_ref, sem, idx_ref[0, 0, 2 * i + p], 2 * i + p).start(priority=p)
        return c

    lax.fori_loop(0, n // 2, start, 0, unroll=DMA_ISSUE_UNROLL // 2)


def _wait_row_gather(src_hbm, dst_ref, sem, n):
    pltpu.make_async_copy(src_hbm.at[pl.ds(0, n)], dst_ref, sem).wait()


def _scatter_copy(x_ref, dst_hbm, sem, src_row, dst_row):
    return pltpu.make_async_copy(x_ref.at[pl.ds(src_row, 1)], dst_hbm.at[pl.ds(dst_row, 1)], sem)


def _scatter_kernel(p0_ref, p1_ref, x_ref, dst_in_hbm_unused, dst_hbm, sem, *, tm):
    def start(r, c):
        _scatter_copy(x_ref, dst_hbm, sem, r, p0_ref[0, 0, r]).start(priority=0)
        _scatter_copy(x_ref, dst_hbm, sem, r, p1_ref[0, 0, r]).start(priority=1)
        return c

    lax.fori_loop(0, tm, start, 0, unroll=DMA_ISSUE_UNROLL)
    for _ in range(2):
        pltpu.make_async_copy(x_ref, dst_hbm.at[pl.ds(0, tm)], sem).wait()


def _scatter_rows(x, pos0, pos1, dst, tm):
    t, d = x.shape
    idx = pl.BlockSpec((1, 1, tm), lambda i: (i, 0, 0), memory_space=pltpu.SMEM)
    return pl.pallas_call(
        functools.partial(_scatter_kernel, tm=tm),
        grid=(t // tm,),
        in_specs=[idx, idx, pl.BlockSpec((tm, d), lambda i: (i, 0)), pl.BlockSpec(memory_space=pl.ANY)],
        out_specs=pl.BlockSpec(memory_space=pl.ANY),
        out_shape=jax.ShapeDtypeStruct(dst.shape, dst.dtype),
        input_output_aliases={3: 0},
        scratch_shapes=[pltpu.SemaphoreType.DMA(())],
        compiler_params=_params("arbitrary"),
        name="scatter_rows",
    )(pos0.reshape(-1, 1, tm), pos1.reshape(-1, 1, tm), x, dst)


def _combine_ln_kernel(p0_ref, p1_ref, p0_next_ref, p1_next_ref, x_ref, route_ref, y_hbm, g_ref, b_ref,
                       op_ref, os_ref, a_scr, b_scr, sems, *, tm, n_prompt_tiles):
    i = pl.program_id(0)
    slot = i % 2

    def start(p0, p1, s):
        _start_row_gather(p0, y_hbm, a_scr.at[s], sems.at[0, s], tm)
        _start_row_gather(p1, y_hbm, b_scr.at[s], sems.at[1, s], tm)

    @pl.when(i == 0)
    def _():
        start(p0_ref, p1_ref, 0)

    @pl.when(i + 1 < pl.num_programs(0))
    def _():
        start(p0_next_ref, p1_next_ref, 1 - slot)

    _wait_row_gather(y_hbm, a_scr.at[slot], sems.at[0, slot], tm)
    _wait_row_gather(y_hbm, b_scr.at[slot], sems.at[1, slot], tm)
    moe = route_ref[:, 2:3] * a_scr[slot] + route_ref[:, 3:4] * b_scr[slot]
    out = _layer_norm(DEEPNORM_ALPHA * x_ref[...] + moe, g_ref[...], b_ref[...])

    @pl.when(i < n_prompt_tiles)
    def _():
        op_ref[...] = out

    @pl.when(i >= n_prompt_tiles)
    def _():
        os_ref[...] = out


def _combine_ln(x, route, y_sorted, pos0, pos1, g, b, tm, n_prompt):
    t, d = x.shape
    npt = n_prompt // tm
    nst = (t - n_prompt) // tm
    n_tiles = npt + nst
    idx = pl.BlockSpec((1, 1, tm), lambda i: (i, 0, 0), memory_space=pltpu.SMEM)
    idx_next = pl.BlockSpec((1, 1, tm), lambda i: (jnp.minimum(i + 1, n_tiles - 1), 0, 0), memory_space=pltpu.SMEM)
    vec = pl.BlockSpec((1, d), lambda i: (0, 0))
    p0, p1 = pos0.reshape(-1, 1, tm), pos1.reshape(-1, 1, tm)
    return pl.pallas_call(
        functools.partial(_combine_ln_kernel, tm=tm, n_prompt_tiles=npt),
        grid=(n_tiles,),
        in_specs=[idx, idx, idx_next, idx_next, pl.BlockSpec((tm, d), lambda i: (i, 0)),
                  pl.BlockSpec((tm, LANES), lambda i: (i, 0)), pl.BlockSpec(memory_space=pl.ANY), vec, vec],
        out_specs=[pl.BlockSpec((tm, d), lambda i: (jnp.minimum(i, npt - 1), 0)),
                   pl.BlockSpec((tm, d), lambda i: (jnp.maximum(i - npt, 0), 0))],
        out_shape=[jax.ShapeDtypeStruct((n_prompt, d), F32), jax.ShapeDtypeStruct((t - n_prompt, d), F32)],
        scratch_shapes=[pltpu.VMEM((2, tm, d), F32), pltpu.VMEM((2, tm, d), F32), pltpu.SemaphoreType.DMA((2, 2))],
        compiler_params=_params("arbitrary"),
        name="moe_combine_ln",
    )(p0, p1, p0, p1, x, route, y_sorted, g.reshape(1, d), b.reshape(1, d))


def _rwkv_proj_kernel(x_ref, prev_ref, mu_ref, wr_ref, wk_ref, wv_ref, w1_ref, w2_ref, a1_ref, a2_ref,
                      g1_ref, g2_ref, vec_ref,
                      r_ref, lw_ref, k_ref, v_ref, kk_ref, b_ref, g_ref, bonus_ref, *, seg):
    x = x_ref[...]
    tm, d = x.shape
    rolled = pltpu.roll(x, 1, 0).reshape(tm // seg, seg, d)
    first = lax.broadcasted_iota(jnp.int32, (tm // seg, seg, d), 1) == 0
    x_prev = jnp.where(first, prev_ref[...], rolled).reshape(tm, d)
    xx = x_prev - x

    def mixed(i):
        return (x + xx * mu_ref[i:i + 1, :]).astype(BF16)

    r = _dot(mixed(0), wr_ref[...])
    k = _dot(mixed(2), wk_ref[...])
    v = _dot(mixed(3), wv_ref[...])
    w0, a0, k_k, k_a, r_k = (vec_ref[i:i + 1, :] for i in range(5))
    wl = w0 + _dot(jnp.tanh(_dot(mixed(1), w1_ref[...])).astype(BF16), w2_ref[...])
    softplus_neg = jnp.maximum(-wl, 0.0) + jnp.log(1.0 + jnp.exp(-jnp.abs(wl)))
    lw_ref[...] = -jnp.exp(-softplus_neg - 0.5)
    a = jax.nn.sigmoid(a0 + _dot(_dot(mixed(4), a1_ref[...]).astype(BF16), a2_ref[...]))
    g_ref[...] = _dot(jax.nn.sigmoid(_dot(mixed(5), g1_ref[...])).astype(BF16), g2_ref[...])
    ones_pair = _head_ones(PAIR)
    kk = k * k_k
    kk = kk / jnp.maximum(jnp.sqrt(_head_sum(kk * kk, ones_pair)), 1e-12)
    k = k * (1.0 + (a - 1.0) * k_a)
    r_ref[...] = r
    k_ref[...] = k
    v_ref[...] = v
    kk_ref[...] = kk
    b_ref[...] = kk * a
    bonus_ref[...] = _head_sum(r * k * r_k, ones_pair) * v


def _rwkv_proj(x, prev_rows, mu, w_r, w_k, w_v, w1, w2, a1, a2, g1, g2, vecs, tm, seg):
    t, d = x.shape
    row = pl.BlockSpec((tm, d), lambda i: (i, 0))

    def full(a):
        return pl.BlockSpec(a.shape, lambda i: (0,) * a.ndim)

    consts = (mu, w_r, w_k, w_v, w1, w2, a1, a2, g1, g2, vecs)
    return pl.pallas_call(
        functools.partial(_rwkv_proj_kernel, seg=seg),
        grid=(t // tm,),
        in_specs=[row, pl.BlockSpec((tm // seg, 1, d), lambda i: (i, 0, 0))] + [full(c) for c in consts],
        out_specs=[row] * 8,
        out_shape=[jax.ShapeDtypeStruct((t, d), F32)] * 8,
        compiler_params=_params("parallel"),
        name="rwkv_proj",
    )(x, prev_rows, *consts)


def _unit_lower_inverses(lows, limit):
    n = lows[0].shape[0]
    r = lax.broadcasted_iota(jnp.int32, (n, n), 0)
    c = lax.broadcasted_iota(jnp.int32, (n, n), 1)
    eye = (r == c).astype(F32)
    base = (r >> 3) == (c >> 3)
    ps = [jnp.where(base, -low, 0.0) for low in lows]
    invs = [eye + p for p in ps]
    for _ in range(2):
        ps = [_dot2(p, p) for p in ps]
        invs = [inv + _dot2(inv, p) for inv, p in zip(invs, ps)]
    shift = 3
    while (1 << shift) < limit:
        sel = ((r >> (shift + 1)) == (c >> (shift + 1))) & ((r >> shift) != (c >> shift))
        mids = [_dot2(inv, jnp.where(sel, low, 0.0)) for inv, low in zip(invs, lows)]
        invs = [inv - _dot2(mid, inv) for inv, mid in zip(invs, mids)]
        shift += 1
    return invs


def _rwkv_scan_kernel(r_ref, lw_ref, k_ref, v_ref, kk_ref, b_ref, h0_ref, o_ref, hT_ref, h_scr, *, n_chunks, n_pairs):
    cs = SCAN_CHUNK
    tb = pl.program_id(2)

    @pl.when(tb == 0)
    def _():
        h_scr[...] = h0_ref[0]

    lane = lax.broadcasted_iota(jnp.int32, (cs, PAIR), 1)
    row = lax.broadcasted_iota(jnp.int32, (cs, PAIR), 0)
    head_a = lane < HEAD_DIM
    strict = (lane & (HEAD_DIM - 1)) < row
    incl = (lane & (HEAD_DIM - 1)) <= row
    pr = lax.broadcasted_iota(jnp.int32, (PAIR, PAIR), 0) >> 6
    pc = lax.broadcasted_iota(jnp.int32, (PAIR, PAIR), 1) >> 6
    same_head = pr == pc

    def stack(x):
        return jnp.concatenate([jnp.where(head_a, x, 0.0), jnp.where(head_a, 0.0, x)], axis=0)

    def each(fn, *lists):
        return [fn(*xs) for xs in zip(*lists)]

    def cumulative(lw):
        acc, shift = lw, 1
        while shift < cs:
            acc = acc + jnp.where(row >= shift, pltpu.roll(acc, shift, 0), 0.0)
            shift *= 2
        return acc

    def chunk(ci, carry):
        sl = pl.ds(pl.multiple_of(ci * cs, cs), cs)
        lanes = [slice(p * PAIR, (p + 1) * PAIR) for p in range(n_pairs)]
        r, lw, k, v, kk, b = ([ref[0, sl, ln] for ln in lanes] for ref in (r_ref, lw_ref, k_ref, v_ref, kk_ref, b_ref))
        h = [h_scr[p] for p in range(n_pairs)]
        cum = each(cumulative, lw)
        total = [c[cs - 1:cs, :] for c in cum]
        r_hat = each(lambda x, c: x * jnp.exp(c), r, cum)
        kk_hat = each(lambda x, c, l: x * jnp.exp(c - l), kk, cum, lw)
        k_hat = each(lambda x, c: x * jnp.exp(-c), k, cum)
        b_hat = each(lambda x, c: x * jnp.exp(-c), b, cum)
        gram = each(lambda kh, rh, bh, k2: _dot2_nt(jnp.concatenate([kh, rh], axis=0),
                                                    jnp.concatenate([stack(bh), stack(k2)], axis=0)),
                    kk_hat, r_hat, b_hat, k_hat)
        l_b = [jnp.where(strict, g[:cs, :PAIR], 0.0) for g in gram]
        l_k = [jnp.where(strict, g[:cs, PAIR:], 0.0) for g in gram]
        a_rb = [jnp.where(incl, g[cs:, :PAIR], 0.0) for g in gram]
        a_rk = [jnp.where(incl, g[cs:, PAIR:], 0.0) for g in gram]
        v_st = each(stack, v)
        w = each(lambda kh, lk, hh, vs: _dot2(jnp.concatenate([kh, lk], axis=1), jnp.concatenate([hh, vs], axis=0)),
                 kk_hat, l_k, h, v_st)
        inv = _unit_lower_inverses(each(stack, l_b), cs)
        u_st = each(lambda t_, w_: _dot2(t_, stack(w_)), inv, w)
        o = each(lambda rh, ak, ab, hh, vs, us: _dot2(jnp.concatenate([rh, ak, -ab], axis=1),
                                                      jnp.concatenate([hh, vs, us], axis=0)),
                 r_hat, a_rk, a_rb, h, v_st, u_st)
        for ln, o_p in zip(lanes, o):
            o_ref[0, sl, ln] = o_p
        u = [us[:cs] + us[cs:] for us in u_st]
        to_end = each(lambda t_, c: jnp.exp(t_ - c), total, cum)
        upd = each(lambda k_, b_, e_, v_, u_: _dot2_tn(jnp.concatenate([k_ * e_, -(b_ * e_)], axis=0),
                                                       jnp.concatenate([v_, u_], axis=0)),
                   k, b, to_end, v, u)
        for p in range(n_pairs):
            decay_col = jnp.transpose(jnp.broadcast_to(jnp.exp(total[p]), (PAIR, PAIR)))
            h_scr[p] = decay_col * h[p] + jnp.where(same_head, upd[p], 0.0)
        return carry

    lax.fori_loop(0, n_chunks, chunk, 0)

    @pl.when(tb == pl.num_programs(2) - 1)
    def _():
        hT_ref[0] = h_scr[...]


def _rwkv_scan(r, lw, k, v, kk, b, h0, tb):
    bsz, t, d = r.shape
    npair = d // PAIR
    group = math.gcd(npair, SCAN_PAIRS_PER_STEP)
    seq = pl.BlockSpec((1, tb, group * PAIR), lambda bi, pi, ti: (bi, ti, pi))
    st = pl.BlockSpec((1, group, PAIR, PAIR), lambda bi, pi, ti: (bi, pi, 0, 0))
    return pl.pallas_call(
        functools.partial(_rwkv_scan_kernel, n_chunks=tb // SCAN_CHUNK, n_pairs=group),
        grid=(bsz, npair // group, t // tb),
        in_specs=[seq] * 6 + [st],
        out_specs=[seq, st],
        out_shape=[jax.ShapeDtypeStruct((bsz, t, d), F32), jax.ShapeDtypeStruct((bsz, npair, PAIR, PAIR), F32)],
        scratch_shapes=[pltpu.VMEM((group, PAIR, PAIR), F32)],
        compiler_params=_params("parallel", "parallel", "arbitrary"),
        name="rwkv_scan",
    )(r, lw, k, v, kk, b, h0)


def _rwkv_out_kernel(o_ref, g_ref, bonus_ref, x_ref, wo_ref, vec_ref, dst_hbm_unused, out_ref):
    o = o_ref[...]
    ones_pair = _head_ones(PAIR)
    mean = _head_sum(o, ones_pair) * (1.0 / HEAD_DIM)
    c = o - mean
    var = _head_sum(c * c, ones_pair) * (1.0 / HEAD_DIM)
    on = c * lax.rsqrt(var + GN_EPS) * vec_ref[0:1, :] + vec_ref[1:2, :]
    y = _dot(((on + bonus_ref[...]) * g_ref[...]).astype(BF16), wo_ref[...])
    out_ref[...] = _layer_norm(DEEPNORM_ALPHA * x_ref[...] + y, vec_ref[2:3, :], vec_ref[3:4, :])


def _rwkv_out(o, g, bonus, x, w_o_bf16, vecs, dst, n_rows, row_offset, tm):
    t, d = x.shape
    row = pl.BlockSpec((tm, d), lambda i: (i, 0))
    off = row_offset // tm
    return pl.pallas_call(
        _rwkv_out_kernel,
        grid=(t // tm,),
        in_specs=[row, row, row, row, pl.BlockSpec((d, d), lambda i: (0, 0)),
                  pl.BlockSpec(vecs.shape, lambda i: (0, 0)), pl.BlockSpec(memory_space=pl.ANY)],
        out_specs=pl.BlockSpec((tm, d), lambda i: (off + i, 0)),
        out_shape=jax.ShapeDtypeStruct((n_rows, d), F32),
        input_output_aliases={6: 0},
        compiler_params=_params("parallel"),
        name="rwkv_out_ln",
    )(o, g, bonus, x, w_o_bf16, vecs, dst)


def _router_kernel(x_ref, w_ref, o_ref, *, n_experts):
    logits = _dot3(x_ref[...], w_ref[...])
    lane = lax.broadcasted_iota(jnp.int32, logits.shape, 1)
    lane_f = lane.astype(F32)
    lowest = jnp.float32(-3.0e38)
    logits = jnp.where(lane < n_experts, logits, lowest)
    m1 = jnp.max(logits, axis=1, keepdims=True)
    i1 = jnp.min(jnp.where(logits == m1, lane_f, float(LANES)), axis=1, keepdims=True)
    rest = jnp.where(lane_f == i1, lowest, logits)
    m2 = jnp.max(rest, axis=1, keepdims=True)
    i2 = jnp.min(jnp.where(rest == m2, lane_f, float(LANES)), axis=1, keepdims=True)
    e = jnp.exp(m2 - m1)
    g1 = 1.0 / (1.0 + e)
    g2 = e * g1
    out = jnp.where(lane == 0, i1, 0.0)
    out = jnp.where(lane == 1, i2, out)
    out = jnp.where(lane == 2, g1, out)
    o_ref[...] = jnp.where(lane == 3, g2, out)


def _router(x, w_router, tm):
    t, d = x.shape
    e = w_router.shape[1]
    w_pad = jnp.zeros((d, LANES), F32).at[:, :e].set(w_router)
    return pl.pallas_call(
        functools.partial(_router_kernel, n_experts=e),
        grid=(t // tm,),
        in_specs=[pl.BlockSpec((tm, d), lambda i: (i, 0)), pl.BlockSpec((d, LANES), lambda i: (0, 0))],
        out_specs=pl.BlockSpec((tm, LANES), lambda i: (i, 0)),
        out_shape=jax.ShapeDtypeStruct((t, LANES), F32),
        compiler_params=_params("parallel"),
        name="router_top2",
    )(x, w_pad)


def _ffn_chunk(f, cap=1536):
    best = LANES
    for tn in range(LANES, cap + 1, LANES):
        if f % tn == 0:
            best = tn
    return best


def _row_tile(t, cap):
    tm = cap
    while t % tm:
        tm //= 2
    return tm


def _pair_states(s):
    b, h, n, _ = s.shape
    st = jnp.swapaxes(s, 2, 3).reshape(b, h // 2, 2, n, n)
    z = jnp.zeros_like(st[:, :, 0])
    top = jnp.concatenate([st[:, :, 0], z], axis=-1)
    bot = jnp.concatenate([z, st[:, :, 1]], axis=-1)
    return jnp.concatenate([top, bot], axis=-2)


def _unpair_states(hp):
    b, p, _, _ = hp.shape
    n = HEAD_DIM
    st = jnp.stack([hp[:, :, :n, :n], hp[:, :, n:, n:]], axis=2).reshape(b, 2 * p, n, n)
    return jnp.swapaxes(st, 2, 3)


def _attention_layer(x, w_qkv, w_o, ln_g, ln_b, cache_k=None, cache_v=None):
    b, t, d = x.shape
    q, kb, vb, kf, vf = _qkv_proj(x, w_qkv, _row_tile(t, ROW_TILE))
    if cache_k is None:
        ao = _sb_attention(q, kb, vb, _row_tile(t, ATTN_QUERY_BLOCK), LANES, 0)
    else:
        past = cache_k.shape[2]
        merge = lambda c: jnp.swapaxes(c.astype(BF16), 1, 2).reshape(b, past, d)
        keys = jnp.concatenate([merge(cache_k), kb], axis=1)
        vals = jnp.concatenate([merge(cache_v), vb], axis=1)
        ao = _sb_attention(q, keys, vals, t, t, past)
    x1 = _mm_res_ln(ao.reshape(b * t, d), w_o, x.reshape(b * t, d), ln_g, ln_b, _row_tile(b * t, ROW_TILE))
    return x1, kf, vf


def _rwkv_layer(x, x_last, s0, p, dst, n_rows, row_offset):
    b, t, d = x.shape
    seg = SCAN_CHUNK
    xs = x.reshape(b, t // seg, seg, d)
    prev = jnp.concatenate([x_last[:, None, :], xs[:, :-1, seg - 1, :]], axis=1).reshape(b * t // seg, 1, d)
    xf = x.reshape(b * t, d)
    tm = _row_tile(b * t, RWKV_ROW_TILE)
    r, lw, k, v, kk, bb, g, bonus = _rwkv_proj(xf, prev, p["mu"], p["w_r"], p["w_k"], p["w_v"], p["w1"], p["w2"],
                                               p["a1"], p["a2"], p["g1"], p["g2"], p["proj_vecs"], tm, seg)
    shp = (b, t, d)
    o, h_t = _rwkv_scan(*(z.reshape(shp) for z in (r, lw, k, v, kk, bb)), _pair_states(s0), _row_tile(t, ROW_TILE))
    dst = _rwkv_out(o.reshape(b * t, d), g, bonus, xf, p["w_o"], p["out_vecs"], dst, n_rows, row_offset, tm)
    return dst, _unpair_states(h_t)


def _moe_layer(x, n_prompt, w_router, w_gate_up, w_down, ln_g, ln_b, tm):
    t, d = x.shape
    e = w_router.shape[1]
    route = _router(x, w_router, tm)
    experts = jnp.concatenate([route[:, 0], route[:, 1]]).astype(jnp.int32)
    onehot = (experts[:, None] == jnp.arange(e, dtype=jnp.int32)[None, :]).astype(jnp.int32)
    counts = jnp.sum(onehot, axis=0)
    padded = ((counts + tm - 1) // tm) * tm
    ends = jnp.cumsum(padded)
    starts = ends - padded
    rank = jnp.sum(jnp.cumsum(onehot, axis=0) * onehot, axis=1) - 1
    pos = (jnp.sum(starts[None, :] * onehot, axis=1) + rank).astype(jnp.int32)
    n_rows = ((2 * t + e * (tm - 1)) // tm) * tm
    tile_start = jnp.arange(n_rows // tm, dtype=jnp.int32) * tm
    tile_expert = jnp.minimum(jnp.sum((tile_start[:, None] >= ends[None, :]).astype(jnp.int32), axis=1), e - 1)
    n_tiles_used = (ends[-1] // tm).astype(jnp.int32).reshape(1)
    xg = _scatter_rows(x, pos[:t], pos[t:], jnp.zeros((n_rows, d), F32), tm)
    y = _moe_ffn(tile_expert, n_tiles_used, xg, w_gate_up, w_down, tm, _ffn_chunk(w_down.shape[1], cap=2048))
    return _combine_ln(x, route, y, pos[:t], pos[t:], ln_g, ln_b, tm, n_prompt)


def kernel(x_prompt, x_sample, cache_k, cache_v, state_wkv, state_shift, att_w_qkv, att_w_o, ffn_w_gate_up, ffn_w_down, rwkv_mu, rwkv_w_rkv, rwkv_w0, rwkv_w1, rwkv_w2, rwkv_a0, rwkv_a1, rwkv_a2, rwkv_g1, rwkv_g2, rwkv_k_k, rwkv_k_a, rwkv_r_k, rwkv_gn_g, rwkv_gn_b, rwkv_w_o, moe_w_router, moe_w_gate_up, moe_w_down, ln_mix_g, ln_mix_b, ln_ffn_g, ln_ffn_b):
    bp, tp, d = x_prompt.shape
    bs, ts, _ = x_sample.shape
    n_prompt, n_sample = bp * tp, bs * ts
    bf = lambda a: a.astype(BF16)

    w_qkv, w_o = bf(att_w_qkv[0]), bf(att_w_o[0])
    xp1, k_p, v_p = _attention_layer(x_prompt, w_qkv, w_o, ln_mix_g[0], ln_mix_b[0])
    xs1, k_s, v_s = _attention_layer(x_sample, w_qkv, w_o, ln_mix_g[0], ln_mix_b[0], cache_k[0], cache_v[0])
    w_gu, w_dn = bf(ffn_w_gate_up[0]), bf(ffn_w_down[0])
    tn = _ffn_chunk(w_dn.shape[0], cap=3072)
    xp2 = _ffn_ln(xp1, w_gu, w_dn, ln_ffn_g[0], ln_ffn_b[0], _row_tile(n_prompt, ROW_TILE), tn).reshape(bp, tp, d)
    xs2 = _ffn_ln(xs1, w_gu, w_dn, ln_ffn_g[0], ln_ffn_b[0], _row_tile(n_sample, ROW_TILE), tn).reshape(bs, ts, d)

    p = {
        "mu": rwkv_mu[0], "w_r": bf(rwkv_w_rkv[0, 0]), "w_k": bf(rwkv_w_rkv[0, 1]), "w_v": bf(rwkv_w_rkv[0, 2]),
        "w1": bf(rwkv_w1[0]), "w2": bf(rwkv_w2[0]), "a1": bf(rwkv_a1[0]), "a2": bf(rwkv_a2[0]),
        "g1": bf(rwkv_g1[0]), "g2": bf(rwkv_g2[0]), "w_o": bf(rwkv_w_o[0]),
        "proj_vecs": jnp.stack([rwkv_w0[0], rwkv_a0[0], rwkv_k_k[0], rwkv_k_a[0], rwkv_r_k[0].reshape(d)]),
        "out_vecs": jnp.stack([rwkv_gn_g[0], rwkv_gn_b[0], ln_mix_g[1], ln_mix_b[1]]),
    }
    n_all = n_prompt + n_sample
    x3, wkv_p = _rwkv_layer(xp2, jnp.zeros((bp, d), F32), jnp.zeros((bp, d // HEAD_DIM, HEAD_DIM, HEAD_DIM), F32),
                            p, jnp.zeros((n_all, d), F32), n_all, 0)
    x3, wkv_s = _rwkv_layer(xs2, state_shift[0], state_wkv[0], p, x3, n_all, n_prompt)
    y_p, y_s = _moe_layer(x3, n_prompt, moe_w_router[0], bf(moe_w_gate_up[0]), bf(moe_w_down[0]),
                          ln_ffn_g[1], ln_ffn_b[1], _row_tile(n_sample, ROW_TILE))
    return (y_p.reshape(bp, tp, d), y_s.reshape(bs, ts, d),
            k_p[None], v_p[None], wkv_p[None], xp2[:, -1][None],
            k_s[None], v_s[None], wkv_s[None], xs2[:, -1][None])
```

```python
import functools
import math

import jax
import jax.numpy as jnp
import numpy as np
from jax import lax
from jax.experimental import pallas as pl
from jax.experimental.pallas import tpu as pltpu

F32 = jnp.float32
BF16 = jnp.bfloat16

HEAD_DIM = 64
LANES = 128
PAIR = 2 * HEAD_DIM
LN_EPS = 1e-5
GN_EPS = 64e-5
DEEPNORM_ALPHA = 4.0 ** 0.25
Q_SCALE = HEAD_DIM ** -0.5 * 1.4426950408889634
SCAN_PAIRS_PER_STEP = 8
SCAN_CHUNK = 64
DMA_ISSUE_UNROLL = 8
ROW_TILE = 512
RWKV_ROW_TILE = 256
V7X_VMEM_BYTES = 64 * 1024 * 1024
V7X_VMEM_LIMIT = V7X_VMEM_BYTES - 8 * 1024 * 1024


def _params(*sem, vmem=V7X_VMEM_LIMIT):
    return pltpu.CompilerParams(dimension_semantics=sem, vmem_limit_bytes=vmem)


def _layer_norm(y, g, b):
    mu = jnp.mean(y, axis=-1, keepdims=True)
    c = y - mu
    var = jnp.mean(c * c, axis=-1, keepdims=True)
    return c * lax.rsqrt(var + LN_EPS) * g + b


def _split2(x):
    hi = x.astype(BF16)
    lo = (x - hi.astype(F32)).astype(BF16)
    return hi, lo


def _dot(a, b):
    return jnp.dot(a, b, preferred_element_type=F32)


def _dot_nt(a, b):
    return lax.dot_general(a, b, (((1,), (1,)), ((), ())), preferred_element_type=F32)


def _dot_tn(a, b):
    return lax.dot_general(a, b, (((0,), (0,)), ((), ())), preferred_element_type=F32)


def _dot3(a, b):
    ah, al = _split2(a)
    bh, bl = _split2(b)
    return _dot(jnp.concatenate([ah, ah, al], axis=1), jnp.concatenate([bh, bl, bh], axis=0))


def _dot2(a, b):
    ah, al = _split2(a)
    bb = b.astype(BF16)
    return _dot(jnp.concatenate([ah, al], axis=1), jnp.concatenate([bb, bb], axis=0))


def _dot2_nt(a, b):
    ah, al = _split2(a)
    bb = b.astype(BF16)
    return _dot_nt(jnp.concatenate([ah, al], axis=1), jnp.concatenate([bb, bb], axis=1))


def _dot2_tn(a, b):
    ah, al = _split2(a)
    bb = b.astype(BF16)
    return _dot_tn(jnp.concatenate([ah, al], axis=0), jnp.concatenate([bb, bb], axis=0))


def _head_ones(n):
    r = lax.broadcasted_iota(jnp.int32, (n, n), 0) >> 6
    c = lax.broadcasted_iota(jnp.int32, (n, n), 1) >> 6
    return (r == c).astype(BF16)


def _head_sum(x, ones_pair):
    d = x.shape[1]
    cols = [_dot2(x[:, c:c + PAIR], ones_pair) for c in range(0, d, PAIR)]
    return jnp.concatenate(cols, axis=1)


def _qkv_kernel(x_ref, wq_ref, wk_ref, wv_ref, qb_ref, kb_ref, vb_ref, kf_ref, vf_ref, *, n_heads):
    xb = x_ref[0].astype(BF16)
    qb_ref[0] = (_dot(xb, wq_ref[...]) * Q_SCALE).astype(BF16)
    for w_ref, b_ref, f_ref in ((wk_ref, kb_ref, kf_ref), (wv_ref, vb_ref, vf_ref)):
        y = _dot(xb, w_ref[...])
        b_ref[0] = y.astype(BF16)
        for h in range(n_heads):
            f_ref[0, h] = y[:, h * HEAD_DIM:(h + 1) * HEAD_DIM]


def _qkv_proj(x, w_bf16, tm):
    b, t, d = x.shape
    h = d // HEAD_DIM
    row = pl.BlockSpec((1, tm, d), lambda bi, ti: (bi, ti, 0))
    hspec = pl.BlockSpec((1, h, tm, HEAD_DIM), lambda bi, ti: (bi, 0, ti, 0))
    shp = (b, h, t, HEAD_DIM)
    return pl.pallas_call(
        functools.partial(_qkv_kernel, n_heads=h),
        grid=(b, t // tm),
        in_specs=[row] + [pl.BlockSpec((d, d), lambda bi, ti, n=n: (0, n)) for n in range(3)],
        out_specs=[row] * 3 + [hspec] * 2,
        out_shape=[jax.ShapeDtypeStruct((b, t, d), BF16)] * 3 + [jax.ShapeDtypeStruct(shp, F32)] * 2,
        compiler_params=_params("parallel", "parallel"),
        name="qkv_proj",
    )(x, w_bf16, w_bf16, w_bf16)


ATTN_STAGES = 5
ATTN_QUERY_BLOCK = 128
MASKED_SCORE = -1.0e30


DEAD_LOG2 = -160.0
F_Q0, F_K0, F_BIAS, F_FIRST, F_REAL, F_QB, F_NEXT, F_LANE = range(8)
ATTN_MAX_LANES = 16
ATTN_ITEMS_PER_TRIP = 4


def _attn_schedule(nq, per_q, past_blocks, bq, bk):
    n_lanes = 1
    while n_lanes * 2 <= min(ATTN_MAX_LANES, nq):
        n_lanes *= 2
    items = [[0, 0, 0, 1, 0, -1, 0, lane] for lane in range(n_lanes)]
    lane_start, lane_end = [], []
    for lane in range(n_lanes):
        lane_start.append(len(items))
        mine = [qb for qb in range(nq)
                if (qb % (2 * n_lanes) if qb % (2 * n_lanes) < n_lanes else 2 * n_lanes - 1 - qb % (2 * n_lanes)) == lane]
        for qb in mine:
            n_full = past_blocks + qb * per_q
            blocks = [(n_full + j, j + 1) for j in reversed(range(per_q))] + [(kb, 0) for kb in reversed(range(n_full))]
            nxt = len(items) + len(blocks)
            for i, (kb, bias_id) in enumerate(blocks):
                items.append([qb * bq, kb * bk, bias_id, int(i == 0), 1, qb, nxt, lane])
        lane_end.append(len(items))
    return items, n_lanes, lane_start, lane_end


def _sb_attn_kernel(tab_ref, q_ref, k_ref, v_ref, o_ref, z_scr, lb_scr, x2_scr, sums_scr, a_scr, later_scr,
                    acc_scr, bias_scr, suffix_scr, cur_ref, dead_ref, qprev_ref, valid_ref,
                    *, bq, bk, tab_len, n_lanes, width, lane_start, lane_end):
    per_q = bq // bk
    for ln in range(n_lanes):
        cur_ref[ln] = lane_start[ln]
        dead_ref[ln] = -1
        qprev_ref[ln] = 0
        valid_ref[ln] = 0
    r = lax.broadcasted_iota(jnp.int32, (2 * bk, 2 * bk), 0)
    c = lax.broadcasted_iota(jnp.int32, (2 * bk, 2 * bk), 1)
    r = jnp.where(r >= bk, r - bk, r)
    suffix_scr[...] = ((c >= bk) | (r > c)).astype(BF16)
    qi = lax.broadcasted_iota(jnp.int32, (bq, bk), 0)
    kj = lax.broadcasted_iota(jnp.int32, (bq, bk), 1)
    bias_scr[0] = jnp.zeros((bq, bk), F32)
    for j in range(per_q):
        bias_scr[j + 1] = jnp.where((kj + j * bk) < qi, 0.0, MASKED_SCORE)
    for scr in (z_scr, lb_scr, x2_scr, sums_scr, a_scr, later_scr, acc_scr):
        scr[...] = jnp.zeros_like(scr)
    lane_a = lax.broadcasted_iota(jnp.int32, (bk, PAIR), 1) < HEAD_DIM

    def field(f, i):
        return tab_ref[f * tab_len + i]

    def stack_heads(blk):
        zero = jnp.zeros_like(blk)
        return jnp.concatenate([jnp.where(lane_a, blk, zero), jnp.where(lane_a, zero, blk)], axis=0)

    def keep_going(c):
        return c[1] < ATTN_STAGES - 1

    def body(c):
        trip, idle = c[0], c[1]
        in_flight = [c[2 + s * width:2 + (s + 1) * width] for s in range(ATTN_STAGES - 1)]
        i1, i2, i3, i4 = in_flight
        i0, any_more = [], None
        for w in range(width):
            lane = (trip * width + w) & (n_lanes - 1)
            end = jnp.int32(lane_end[0])
            for ln in range(1, n_lanes):
                end = jnp.where(lane == ln, lane_end[ln], end)
            nxt = cur_ref[lane]
            peek = jnp.minimum(nxt, tab_len - 1)
            nxt = jnp.where((nxt < end) & (field(F_QB, peek) == dead_ref[lane]), field(F_NEXT, peek), nxt)
            more = nxt < end
            i0.append(jnp.where(more, nxt, lane))
            cur_ref[lane] = jnp.where(more, nxt + 1, nxt)
            any_more = more if any_more is None else any_more | more
        idle = jnp.where(any_more, 0, idle + 1)
        first5 = [field(F_FIRST, i) == 1 for i in i4]
        lane5 = [field(F_LANE, i) for i in i4]
        for w in range(width):
            @pl.when(first5[w] & (valid_ref[lane5[w]] == 1))
            def _():
                o_ref[0, pl.ds(pl.multiple_of(qprev_ref[lane5[w]], bq), bq), :] = acc_scr[lane5[w]].astype(BF16)

            qprev_ref[lane5[w]] = field(F_Q0, i4[w])
            valid_ref[lane5[w]] = field(F_REAL, i4[w])
        for w in range(width):
            k5 = pl.multiple_of(field(F_K0, i4[w]), bk)
            contrib = _dot(a_scr[w], stack_heads(v_ref[0, pl.ds(k5, bk), :]))
            acc_scr[lane5[w]] = jnp.where(first5[w], contrib, acc_scr[lane5[w]] + contrib)
        slot = trip & 1
        for w in range(width):
            restart = field(F_FIRST, i3[w]) == 1
            lane4 = field(F_LANE, i3[w])
            lowest = None
            for h in range(2):
                cols = slice(h * bk, (h + 1) * bk)
                later = jnp.where(restart, 0.0, later_scr[lane4, :, cols])
                between = sums_scr[w, h, :, :bk] + later
                later = later + sums_scr[w, h, :, bk:]
                later_scr[lane4, :, cols] = later
                lowest = later if lowest is None else jnp.maximum(lowest, later)
                a_scr[w, :, cols] = jnp.exp2(lb_scr[w, slot, :, cols] + between).astype(BF16)
            dead_ref[lane4] = jnp.where(jnp.max(lowest) < DEAD_LOG2, field(F_QB, i3[w]), -1)
        for w in range(width):
            for h in range(2):
                sums_scr[w, h] = _dot(x2_scr[w, h], suffix_scr[...])
        for w in range(width):
            bias = bias_scr[field(F_BIAS, i1[w])]
            for h in range(2):
                cols = slice(h * bk, (h + 1) * bk)
                z = z_scr[w, :, cols] + bias
                soft = jnp.log2(1.0 + jnp.exp2(-jnp.abs(z)))
                log_beta = jnp.minimum(z, 0.0) - soft
                hi, lo = _split2(log_beta - z)
                lb_scr[w, slot, :, cols] = log_beta
                x2_scr[w, h] = jnp.concatenate([hi, lo], axis=1)
        for w in range(width):
            q1 = pl.multiple_of(field(F_Q0, i0[w]), bq)
            k1 = pl.multiple_of(field(F_K0, i0[w]), bk)
            z_scr[w] = _dot_nt(q_ref[0, pl.ds(q1, bq), :], stack_heads(k_ref[0, pl.ds(k1, bk), :]))
        return (trip + 1, idle, *i0, *i1, *i2, *i3)

    zero = jnp.int32(0)
    lax.while_loop(keep_going, body, (zero,) * (2 + (ATTN_STAGES - 1) * width))
    for ln in range(n_lanes):
        @pl.when(valid_ref[ln] == 1)
        def _():
            o_ref[0, pl.ds(pl.multiple_of(qprev_ref[ln], bq), bq), :] = acc_scr[ln].astype(BF16)


def _sb_attention(q, k, v, bq, bk, q_start):
    b, tq, d = q.shape
    tk = k.shape[1]
    assert tq % bq == 0 and bq % bk == 0 and q_start % bk == 0 and tk == q_start + tq
    per_q = bq // bk
    items, n_lanes, lane_start, lane_end = _attn_schedule(tq // bq, per_q, q_start // bk, bq, bk)
    lane_smem = pltpu.SMEM((n_lanes,), jnp.int32)
    width = max(1, min(ATTN_ITEMS_PER_TRIP, n_lanes // (ATTN_STAGES - 1)))
    table = jnp.asarray(np.asarray(items, np.int32).T.reshape(-1))
    qspec = pl.BlockSpec((1, tq, PAIR), lambda bi, pi, tab: (bi, 0, pi))
    kspec = pl.BlockSpec((1, tk, PAIR), lambda bi, pi, tab: (bi, 0, pi))
    grid_spec = pltpu.PrefetchScalarGridSpec(
        num_scalar_prefetch=1,
        grid=(b, d // PAIR),
        in_specs=[qspec, kspec, kspec],
        out_specs=qspec,
        scratch_shapes=[pltpu.VMEM((width, bq, 2 * bk), F32), pltpu.VMEM((width, 2, bq, 2 * bk), F32),
                        pltpu.VMEM((width, 2, bq, 2 * bk), BF16), pltpu.VMEM((width, 2, bq, 2 * bk), F32),
                        pltpu.VMEM((width, bq, 2 * bk), BF16), pltpu.VMEM((n_lanes, bq, 2 * bk), F32),
                        pltpu.VMEM((n_lanes, bq, PAIR), F32), pltpu.VMEM((per_q + 1, bq, bk), F32),
                        pltpu.VMEM((2 * bk, 2 * bk), BF16), lane_smem, lane_smem, lane_smem, lane_smem],
    )
    return pl.pallas_call(
        functools.partial(_sb_attn_kernel, bq=bq, bk=bk, tab_len=len(items), n_lanes=n_lanes, width=width,
                          lane_start=tuple(lane_start), lane_end=tuple(lane_end)),
        grid_spec=grid_spec,
        out_shape=jax.ShapeDtypeStruct((b, tq, d), BF16),
        compiler_params=_params("parallel", "parallel"),
        name="sb_attention",
    )(table, q, k, v)


def _mm_res_ln_kernel(a_ref, w_ref, x_ref, g_ref, b_ref, o_ref):
    y = _dot(a_ref[...], w_ref[...])
    o_ref[...] = _layer_norm(DEEPNORM_ALPHA * x_ref[...] + y, g_ref[...], b_ref[...])


def _mm_res_ln(a, w_bf16, x, g, b, tm):
    t, d = x.shape
    kdim = a.shape[1]
    vec = pl.BlockSpec((1, d), lambda i: (0, 0))
    return pl.pallas_call(
        _mm_res_ln_kernel,
        grid=(t // tm,),
        in_specs=[pl.BlockSpec((tm, kdim), lambda i: (i, 0)),
                  pl.BlockSpec((kdim, d), lambda i: (0, 0)),
                  pl.BlockSpec((tm, d), lambda i: (i, 0)), vec, vec],
        out_specs=pl.BlockSpec((tm, d), lambda i: (i, 0)),
        out_shape=jax.ShapeDtypeStruct((t, d), F32),
        compiler_params=_params("parallel"),
        name="attn_out_ln",
    )(a, w_bf16, x, g.reshape(1, d), b.reshape(1, d))


def _swiglu_step(xb, wg_ref, wu_ref, wd_ref):
    gate = _dot(xb, wg_ref[...])
    up = _dot(xb, wu_ref[...])
    hidden = gate * jax.nn.sigmoid(gate) * up
    return _dot(hidden.astype(BF16), wd_ref[...])


def _ffn_ln_kernel(x_ref, wg_ref, wu_ref, wd_ref, g_ref, b_ref, o_ref, xb_scr, acc_scr):
    j = pl.program_id(1)

    @pl.when(j == 0)
    def _():
        xb_scr[...] = x_ref[...].astype(BF16)
        acc_scr[...] = jnp.zeros_like(acc_scr)

    acc_scr[...] += _swiglu_step(xb_scr[...], wg_ref, wu_ref, wd_ref)

    @pl.when(j == pl.num_programs(1) - 1)
    def _():
        o_ref[...] = _layer_norm(DEEPNORM_ALPHA * x_ref[...] + acc_scr[...], g_ref[...], b_ref[...])


def _ffn_ln(x, w_gate_up_bf16, w_down_bf16, g, b, tm, tn):
    t, d = x.shape
    f = w_down_bf16.shape[0]
    nj = f // tn
    vec = pl.BlockSpec((1, d), lambda i, j: (0, 0))
    once = dict(pipeline_mode=pl.Buffered(1)) if nj == 1 else {}
    return pl.pallas_call(
        _ffn_ln_kernel,
        grid=(t // tm, nj),
        in_specs=[pl.BlockSpec((tm, d), lambda i, j: (i, 0)),
                  pl.BlockSpec((d, tn), lambda i, j: (0, j), **once),
                  pl.BlockSpec((d, tn), lambda i, j: (0, nj + j), **once),
                  pl.BlockSpec((tn, d), lambda i, j: (j, 0), **once), vec, vec],
        out_specs=pl.BlockSpec((tm, d), lambda i, j: (i, 0)),
        out_shape=jax.ShapeDtypeStruct((t, d), F32),
        scratch_shapes=[pltpu.VMEM((tm, d), BF16), pltpu.VMEM((tm, d), F32)],
        compiler_params=_params("parallel", "arbitrary"),
        name="dense_swiglu_ln",
    )(x, w_gate_up_bf16, w_gate_up_bf16, w_down_bf16, g.reshape(1, d), b.reshape(1, d))


def _moe_ffn_kernel(te_ref, nt_ref, x_ref, wg_ref, wu_ref, wd_ref, o_ref, xb_scr, acc_scr):
    i = pl.program_id(0)
    j = pl.program_id(1)
    used = i < nt_ref[0]

    @pl.when(used & (j == 0))
    def _():
        xb_scr[...] = x_ref[...].astype(BF16)
        acc_scr[...] = jnp.zeros_like(acc_scr)

    @pl.when(used)
    def _():
        acc_scr[...] += _swiglu_step(xb_scr[...], wg_ref.at[0], wu_ref.at[0], wd_ref.at[0])

    @pl.when(j == pl.num_programs(1) - 1)
    def _():
        o_ref[...] = jnp.where(used, acc_scr[...], 0.0)


def _moe_ffn(tile_expert, n_tiles_used, xg, w_gate_up_bf16, w_down_bf16, tm, tn):
    n, d = xg.shape
    f = w_down_bf16.shape[1]
    nj = f // tn
    grid_spec = pltpu.PrefetchScalarGridSpec(
        num_scalar_prefetch=2,
        grid=(n // tm, nj),
        in_specs=[pl.BlockSpec((tm, d), lambda i, j, te, nt: (i, 0)),
                  pl.BlockSpec((1, d, tn), lambda i, j, te, nt: (te[i], 0, j)),
                  pl.BlockSpec((1, d, tn), lambda i, j, te, nt: (te[i], 0, nj + j)),
                  pl.BlockSpec((1, tn, d), lambda i, j, te, nt: (te[i], j, 0))],
        out_specs=pl.BlockSpec((tm, d), lambda i, j, te, nt: (i, 0)),
        scratch_shapes=[pltpu.VMEM((tm, d), BF16), pltpu.VMEM((tm, d), F32)],
    )
    return pl.pallas_call(
        _moe_ffn_kernel,
        grid_spec=grid_spec,
        out_shape=jax.ShapeDtypeStruct((n, d), F32),
        compiler_params=_params("parallel", "arbitrary"),
        name="expert_swiglu",
    )(tile_expert, n_tiles_used, xg, w_gate_up_bf16, w_gate_up_bf16, w_down_bf16)


def _row_copy(src_hbm, dst_ref, sem, src_row, dst_row):
    return pltpu.make_async_copy(src_hbm.at[pl.ds(src_row, 1)], dst_ref.at[pl.ds(dst_row, 1)], sem)


def _start_row_gather(idx_ref, src_hbm, dst_ref, sem, n):
    def start(i, c):
        for p in range(2):
            _row_copy(src_hbm, dst_ref, sem, idx_ref[0, 0, 2 * i + p], 2 * i + p).start(priority=p)
        return c

    lax.fori_loop(0, n // 2, start, 0, unroll=DMA_ISSUE_UNROLL // 2)


def _wait_row_gather(src_hbm, dst_ref, sem, n):
    pltpu.make_async_copy(src_hbm.at[pl.ds(0, n)], dst_ref, sem).wait()


def _scatter_copy(x_ref, dst_hbm, sem, src_row, dst_row):
    return pltpu.make_async_copy(x_ref.at[pl.ds(src_row, 1)], dst_hbm.at[pl.ds(dst_row, 1)], sem)


def _scatter_kernel(p0_ref, p1_ref, x_ref, dst_in_hbm_unused, dst_hbm, sem, *, tm):
    def start(r, c):
        _scatter_copy(x_ref, dst_hbm, sem, r, p0_ref[0, 0, r]).start(priority=0)
        _scatter_copy(x_ref, dst_hbm, sem, r, p1_ref[0, 0, r]).start(priority=1)
        return c

    lax.fori_loop(0, tm, start, 0, unroll=DMA_ISSUE_UNROLL)
    for _ in range(2):
        pltpu.make_async_copy(x_ref, dst_hbm.at[pl.ds(0, tm)], sem).wait()


def _scatter_rows(x, pos0, pos1, dst, tm):
    t, d = x.shape
    idx = pl.BlockSpec((1, 1, tm), lambda i: (i, 0, 0), memory_space=pltpu.SMEM)
    return pl.pallas_call(
        functools.partial(_scatter_kernel, tm=tm),
        grid=(t // tm,),
        in_specs=[idx, idx, pl.BlockSpec((tm, d), lambda i: (i, 0)), pl.BlockSpec(memory_space=pl.ANY)],
        out_specs=pl.BlockSpec(memory_space=pl.ANY),
        out_shape=jax.ShapeDtypeStruct(dst.shape, dst.dtype),
        input_output_aliases={3: 0},
        scratch_shapes=[pltpu.SemaphoreType.DMA(())],
        compiler_params=_params("arbitrary"),
        name="scatter_rows",
    )(pos0.reshape(-1, 1, tm), pos1.reshape(-1, 1, tm), x, dst)


def _combine_ln_kernel(p0_ref, p1_ref, p0_next_ref, p1_next_ref, x_ref, route_ref, y_hbm, g_ref, b_ref,
                       op_ref, os_ref, a_scr, b_scr, sems, *, tm, n_prompt_tiles):
    i = pl.program_id(0)
    slot = i % 2

    def start(p0, p1, s):
        _start_row_gather(p0, y_hbm, a_scr.at[s], sems.at[0, s], tm)
        _start_row_gather(p1, y_hbm, b_scr.at[s], sems.at[1, s], tm)

    @pl.when(i == 0)
    def _():
        start(p0_ref, p1_ref, 0)

    @pl.when(i + 1 < pl.num_programs(0))
    def _():
        start(p0_next_ref, p1_next_ref, 1 - slot)

    _wait_row_gather(y_hbm, a_scr.at[slot], sems.at[0, slot], tm)
    _wait_row_gather(y_hbm, b_scr.at[slot], sems.at[1, slot], tm)
    moe = route_ref[:, 2:3] * a_scr[slot] + route_ref[:, 3:4] * b_scr[slot]
    out = _layer_norm(DEEPNORM_ALPHA * x_ref[...] + moe, g_ref[...], b_ref[...])

    @pl.when(i < n_prompt_tiles)
    def _():
        op_ref[...] = out

    @pl.when(i >= n_prompt_tiles)
    def _():
        os_ref[...] = out


def _combine_ln(x, route, y_sorted, pos0, pos1, g, b, tm, n_prompt):
    t, d = x.shape
    npt = n_prompt // tm
    nst = (t - n_prompt) // tm
    n_tiles = npt + nst
    idx = pl.BlockSpec((1, 1, tm), lambda i: (i, 0, 0), memory_space=pltpu.SMEM)
    idx_next = pl.BlockSpec((1, 1, tm), lambda i: (jnp.minimum(i + 1, n_tiles - 1), 0, 0), memory_space=pltpu.SMEM)
    vec = pl.BlockSpec((1, d), lambda i: (0, 0))
    p0, p1 = pos0.reshape(-1, 1, tm), pos1.reshape(-1, 1, tm)
    return pl.pallas_call(
        functools.partial(_combine_ln_kernel, tm=tm, n_prompt_tiles=npt),
        grid=(n_tiles,),
        in_specs=[idx, idx, idx_next, idx_next, pl.BlockSpec((tm, d), lambda i: (i, 0)),
                  pl.BlockSpec((tm, LANES), lambda i: (i, 0)), pl.BlockSpec(memory_space=pl.ANY), vec, vec],
        out_specs=[pl.BlockSpec((tm, d), lambda i: (jnp.minimum(i, npt - 1), 0)),
                   pl.BlockSpec((tm, d), lambda i: (jnp.maximum(i - npt, 0), 0))],
        out_shape=[jax.ShapeDtypeStruct((n_prompt, d), F32), jax.ShapeDtypeStruct((t - n_prompt, d), F32)],
        scratch_shapes=[pltpu.VMEM((2, tm, d), F32), pltpu.VMEM((2, tm, d), F32), pltpu.SemaphoreType.DMA((2, 2))],
        compiler_params=_params("arbitrary"),
        name="moe_combine_ln",
    )(p0, p1, p0, p1, x, route, y_sorted, g.reshape(1, d), b.reshape(1, d))


def _rwkv_proj_kernel(x_ref, prev_ref, mu_ref, wr_ref, wk_ref, wv_ref, w1_ref, w2_ref, a1_ref, a2_ref,
                      g1_ref, g2_ref, vec_ref,
                      r_ref, lw_ref, k_ref, v_ref, kk_ref, b_ref, g_ref, bonus_ref, *, seg):
    x = x_ref[...]
    tm, d = x.shape
    rolled = pltpu.roll(x, 1, 0).reshape(tm // seg, seg, d)
    first = lax.broadcasted_iota(jnp.int32, (tm // seg, seg, d), 1) == 0
    x_prev = jnp.where(first, prev_ref[...], rolled).reshape(tm, d)
    xx = x_prev - x

    def mixed(i):
        return (x + xx * mu_ref[i:i + 1, :]).astype(BF16)

    r = _dot(mixed(0), wr_ref[...])
    k = _dot(mixed(2), wk_ref[...])
    v = _dot(mixed(3), wv_ref[...])
    w0, a0, k_k, k_a, r_k = (vec_ref[i:i + 1, :] for i in range(5))
    wl = w0 + _dot(jnp.tanh(_dot(mixed(1), w1_ref[...])).astype(BF16), w2_ref[...])
    softplus_neg = jnp.maximum(-wl, 0.0) + jnp.log(1.0 + jnp.exp(-jnp.abs(wl)))
    lw_ref[...] = -jnp.exp(-softplus_neg - 0.5)
    a = jax.nn.sigmoid(a0 + _dot(_dot(mixed(4), a1_ref[...]).astype(BF16), a2_ref[...]))
    g_ref[...] = _dot(jax.nn.sigmoid(_dot(mixed(5), g1_ref[...])).astype(BF16), g2_ref[...])
    ones_pair = _head_ones(PAIR)
    kk = k * k_k
    kk = kk / jnp.maximum(jnp.sqrt(_head_sum(kk * kk, ones_pair)), 1e-12)
    k = k * (1.0 + (a - 1.0) * k_a)
    r_ref[...] = r
    k_ref[...] = k
    v_ref[...] = v
    kk_ref[...] = kk
    b_ref[...] = kk * a
    bonus_ref[...] = _head_sum(r * k * r_k, ones_pair) * v


def _rwkv_proj(x, prev_rows, mu, w_r, w_k, w_v, w1, w2, a1, a2, g1, g2, vecs, tm, seg):
    t, d = x.shape
    row = pl.BlockSpec((tm, d), lambda i: (i, 0))

    def full(a):
        return pl.BlockSpec(a.shape, lambda i: (0,) * a.ndim)

    consts = (mu, w_r, w_k, w_v, w1, w2, a1, a2, g1, g2, vecs)
    return pl.pallas_call(
        functools.partial(_rwkv_proj_kernel, seg=seg),
        grid=(t // tm,),
        in_specs=[row, pl.BlockSpec((tm // seg, 1, d), lambda i: (i, 0, 0))] + [full(c) for c in consts],
        out_specs=[row] * 8,
        out_shape=[jax.ShapeDtypeStruct((t, d), F32)] * 8,
        compiler_params=_params("parallel"),
        name="rwkv_proj",
    )(x, prev_rows, *consts)


def _unit_lower_inverses(lows, limit):
    n = lows[0].shape[0]
    r = lax.broadcasted_iota(jnp.int32, (n, n), 0)
    c = lax.broadcasted_iota(jnp.int32, (n, n), 1)
    eye = (r == c).astype(F32)
    base = (r >> 3) == (c >> 3)
    ps = [jnp.where(base, -low, 0.0) for low in lows]
    invs = [eye + p for p in ps]
    for _ in range(2):
        ps = [_dot2(p, p) for p in ps]
        invs = [inv + _dot2(inv, p) for inv, p in zip(invs, ps)]
    shift = 3
    while (1 << shift) < limit:
        sel = ((r >> (shift + 1)) == (c >> (shift + 1))) & ((r >> shift) != (c >> shift))
        mids = [_dot2(inv, jnp.where(sel, low, 0.0)) for inv, low in zip(invs, lows)]
        invs = [inv - _dot2(mid, inv) for inv, mid in zip(invs, mids)]
        shift += 1
    return invs


def _rwkv_scan_kernel(r_ref, lw_ref, k_ref, v_ref, kk_ref, b_ref, h0_ref, o_ref, hT_ref, h_scr, *, n_chunks, n_pairs):
    cs = SCAN_CHUNK
    tb = pl.program_id(2)

    @pl.when(tb == 0)
    def _():
        h_scr[...] = h0_ref[0]

    lane = lax.broadcasted_iota(jnp.int32, (cs, PAIR), 1)
    row = lax.broadcasted_iota(jnp.int32, (cs, PAIR), 0)
    head_a = lane < HEAD_DIM
    strict = (lane & (HEAD_DIM - 1)) < row
    incl = (lane & (HEAD_DIM - 1)) <= row
    pr = lax.broadcasted_iota(jnp.int32, (PAIR, PAIR), 0) >> 6
    pc = lax.broadcasted_iota(jnp.int32, (PAIR, PAIR), 1) >> 6
    same_head = pr == pc

    def stack(x):
        return jnp.concatenate([jnp.where(head_a, x, 0.0), jnp.where(head_a, 0.0, x)], axis=0)

    def each(fn, *lists):
        return [fn(*xs) for xs in zip(*lists)]

    def cumulative(lw):
        acc, shift = lw, 1
        while shift < cs:
            acc = acc + jnp.where(row >= shift, pltpu.roll(acc, shift, 0), 0.0)
            shift *= 2
        return acc

    def chunk(ci, carry):
        sl = pl.ds(pl.multiple_of(ci * cs, cs), cs)
        lanes = [slice(p * PAIR, (p + 1) * PAIR) for p in range(n_pairs)]
        r, lw, k, v, kk, b = ([ref[0, sl, ln] for ln in lanes] for ref in (r_ref, lw_ref, k_ref, v_ref, kk_ref, b_ref))
        h = [h_scr[p] for p in range(n_pairs)]
        cum = each(cumulative, lw)
        total = [c[cs - 1:cs, :] for c in cum]
        r_hat = each(lambda x, c: x * jnp.exp(c), r, cum)
        kk_hat = each(lambda x, c, l: x * jnp.exp(c - l), kk, cum, lw)
        k_hat = each(lambda x, c: x * jnp.exp(-c), k, cum)
        b_hat = each(lambda x, c: x * jnp.exp(-c), b, cum)
        gram = each(lambda kh, rh, bh, k2: _dot2_nt(jnp.concatenate([kh, rh], axis=0),
                                                    jnp.concatenate([stack(bh), stack(k2)], axis=0)),
                    kk_hat, r_hat, b_hat, k_hat)
        l_b = [jnp.where(strict, g[:cs, :PAIR], 0.0) for g in gram]
        l_k = [jnp.where(strict, g[:cs, PAIR:], 0.0) for g in gram]
        a_rb = [jnp.where(incl, g[cs:, :PAIR], 0.0) for g in gram]
        a_rk = [jnp.where(incl, g[cs:, PAIR:], 0.0) for g in gram]
        v_st = each(stack, v)
        w = each(lambda kh, lk, hh, vs: _dot2(jnp.concatenate([kh, lk], axis=1), jnp.concatenate([hh, vs], axis=0)),
                 kk_hat, l_k, h, v_st)
        inv = _unit_lower_inverses(each(stack, l_b), cs)
        u_st = each(lambda t_, w_: _dot2(t_, stack(w_)), inv, w)
        o = each(lambda rh, ak, ab, hh, vs, us: _dot2(jnp.concatenate([rh, ak, -ab], axis=1),
                                                      jnp.concatenate([hh, vs, us], axis=0)),
                 r_hat, a_rk, a_rb, h, v_st, u_st)
        for ln, o_p in zip(lanes, o):
            o_ref[0, sl, ln] = o_p
        u = [us[:cs] + us[cs:] for us in u_st]
        to_end = each(lambda t_, c: jnp.exp(t_ - c), total, cum)
        upd = each(lambda k_, b_, e_, v_, u_: _dot2_tn(jnp.concatenate([k_ * e_, -(b_ * e_)], axis=0),
                                                       jnp.concatenate([v_, u_], axis=0)),
                   k, b, to_end, v, u)
        for p in range(n_pairs):
            decay_col = jnp.transpose(jnp.broadcast_to(jnp.exp(total[p]), (PAIR, PAIR)))
            h_scr[p] = decay_col * h[p] + jnp.where(same_head, upd[p], 0.0)
        return carry

    lax.fori_loop(0, n_chunks, chunk, 0)

    @pl.when(tb == pl.num_programs(2) - 1)
    def _():
        hT_ref[0] = h_scr[...]


def _rwkv_scan(r, lw, k, v, kk, b, h0, tb):
    bsz, t, d = r.shape
    npair = d // PAIR
    group = math.gcd(npair, SCAN_PAIRS_PER_STEP)
    seq = pl.BlockSpec((1, tb, group * PAIR), lambda bi, pi, ti: (bi, ti, pi))
    st = pl.BlockSpec((1, group, PAIR, PAIR), lambda bi, pi, ti: (bi, pi, 0, 0))
    return pl.pallas_call(
        functools.partial(_rwkv_scan_kernel, n_chunks=tb // SCAN_CHUNK, n_pairs=group),
        grid=(bsz, npair // group, t // tb),
        in_specs=[seq] * 6 + [st],
        out_specs=[seq, st],
        out_shape=[jax.ShapeDtypeStruct((bsz, t, d), F32), jax.ShapeDtypeStruct((bsz, npair, PAIR, PAIR), F32)],
        scratch_shapes=[pltpu.VMEM((group, PAIR, PAIR), F32)],
        compiler_params=_params("parallel", "parallel", "arbitrary"),
        name="rwkv_scan",
    )(r, lw, k, v, kk, b, h0)


def _route_top2(x, w_router_pad, n_experts):
    logits = _dot3(x, w_router_pad)
    lane = lax.broadcasted_iota(jnp.int32, logits.shape, 1)
    lane_f = lane.astype(F32)
    lowest = jnp.float32(-3.0e38)
    logits = jnp.where(lane < n_experts, logits, lowest)
    m1 = jnp.max(logits, axis=1, keepdims=True)
    i1 = jnp.min(jnp.where(logits == m1, lane_f, float(LANES)), axis=1, keepdims=True)
    rest = jnp.where(lane_f == i1, lowest, logits)
    m2 = jnp.max(rest, axis=1, keepdims=True)
    i2 = jnp.min(jnp.where(rest == m2, lane_f, float(LANES)), axis=1, keepdims=True)
    e = jnp.exp(m2 - m1)
    g1 = 1.0 / (1.0 + e)
    g2 = e * g1
    out = jnp.where(lane == 0, i1, 0.0)
    out = jnp.where(lane == 1, i2, out)
    out = jnp.where(lane == 2, g1, out)
    return jnp.where(lane == 3, g2, out)


def _rwkv_out_kernel(*refs, n_experts, n_main, has_tail):
    o_ref, g_ref, bonus_ref, x_ref, wo_ref, vec_ref, wr_ref = refs[:7]
    out_ref, route_ref = refs[-2:]

    @pl.when(pl.program_id(0) < n_main)
    def _():
        o = o_ref[...]
        ones_pair = _head_ones(PAIR)
        mean = _head_sum(o, ones_pair) * (1.0 / HEAD_DIM)
        c = o - mean
        var = _head_sum(c * c, ones_pair) * (1.0 / HEAD_DIM)
        on = c * lax.rsqrt(var + GN_EPS) * vec_ref[0:1, :] + vec_ref[1:2, :]
        y = _dot(((on + bonus_ref[...]) * g_ref[...]).astype(BF16), wo_ref[...])
        x_new = _layer_norm(DEEPNORM_ALPHA * x_ref[...] + y, vec_ref[2:3, :], vec_ref[3:4, :])
        out_ref[...] = x_new
        route_ref[...] = _route_top2(x_new, wr_ref[...], n_experts)

    if has_tail:
        tail_x_ref, tail_route_ref = refs[7:9]

        @pl.when(pl.program_id(0) >= n_main)
        def _():
            out_ref[...] = tail_x_ref[...]
            route_ref[...] = tail_route_ref[...]


def _rwkv_out(o, g, bonus, x, w_o_bf16, vecs, w_router_pad, n_experts, tail, tm):
    t, d = x.shape
    n_main = t // tm
    n_tail = 0 if tail is None else tail[0].shape[0] // tm
    main = lambda i: (jnp.minimum(i, n_main - 1), 0)
    row = pl.BlockSpec((tm, d), main)
    const = lambda a: pl.BlockSpec(a.shape, lambda i: (0, 0))
    in_specs = [row, row, row, row, const(w_o_bf16), const(vecs), const(w_router_pad)]
    args = [o, g, bonus, x, w_o_bf16, vecs, w_router_pad]
    if tail is not None:
        behind = lambda i: (jnp.maximum(i - n_main, 0), 0)
        in_specs += [pl.BlockSpec((tm, d), behind), pl.BlockSpec((tm, LANES), behind)]
        args += list(tail)
    n_rows = (n_main + n_tail) * tm
    return pl.pallas_call(
        functools.partial(_rwkv_out_kernel, n_experts=n_experts, n_main=n_main, has_tail=tail is not None),
        grid=(n_main + n_tail,),
        in_specs=in_specs,
        out_specs=[pl.BlockSpec((tm, d), lambda i: (i, 0)), pl.BlockSpec((tm, LANES), lambda i: (i, 0))],
        out_shape=[jax.ShapeDtypeStruct((n_rows, d), F32), jax.ShapeDtypeStruct((n_rows, LANES), F32)],
        compiler_params=_params("arbitrary"),
        name="rwkv_out_ln",
    )(*args)


def _ffn_chunk(f, cap=1536):
    best = LANES
    for tn in range(LANES, cap + 1, LANES):
        if f % tn == 0:
            best = tn
    return best


def _row_tile(t, cap):
    tm = cap
    while t % tm:
        tm //= 2
    return tm


def _pair_states(s):
    b, h, n, _ = s.shape
    st = jnp.swapaxes(s, 2, 3).reshape(b, h // 2, 2, n, n)
    z = jnp.zeros_like(st[:, :, 0])
    top = jnp.concatenate([st[:, :, 0], z], axis=-1)
    bot = jnp.concatenate([z, st[:, :, 1]], axis=-1)
    return jnp.concatenate([top, bot], axis=-2)


def _unpair_states(hp):
    b, p, _, _ = hp.shape
    n = HEAD_DIM
    st = jnp.stack([hp[:, :, :n, :n], hp[:, :, n:, n:]], axis=2).reshape(b, 2 * p, n, n)
    return jnp.swapaxes(st, 2, 3)


def _attention_layer(x, w_qkv, w_o, ln_g, ln_b, cache_k=None, cache_v=None):
    b, t, d = x.shape
    q, kb, vb, kf, vf = _qkv_proj(x, w_qkv, _row_tile(t, ROW_TILE))
    if cache_k is None:
        ao = _sb_attention(q, kb, vb, _row_tile(t, ATTN_QUERY_BLOCK), LANES, 0)
    else:
        past = cache_k.shape[2]
        merge = lambda c: jnp.swapaxes(c.astype(BF16), 1, 2).reshape(b, past, d)
        keys = jnp.concatenate([merge(cache_k), kb], axis=1)
        vals = jnp.concatenate([merge(cache_v), vb], axis=1)
        ao = _sb_attention(q, keys, vals, t, t, past)
    x1 = _mm_res_ln(ao.reshape(b * t, d), w_o, x.reshape(b * t, d), ln_g, ln_b, _row_tile(b * t, ROW_TILE))
    return x1, kf, vf


def _rwkv_layer(x, x_last, s0, p, tail):
    b, t, d = x.shape
    seg = SCAN_CHUNK
    xs = x.reshape(b, t // seg, seg, d)
    prev = jnp.concatenate([x_last[:, None, :], xs[:, :-1, seg - 1, :]], axis=1).reshape(b * t // seg, 1, d)
    xf = x.reshape(b * t, d)
    tm = _row_tile(b * t, RWKV_ROW_TILE)
    r, lw, k, v, kk, bb, g, bonus = _rwkv_proj(xf, prev, p["mu"], p["w_r"], p["w_k"], p["w_v"], p["w1"], p["w2"],
                                               p["a1"], p["a2"], p["g1"], p["g2"], p["proj_vecs"], tm, seg)
    shp = (b, t, d)
    o, h_t = _rwkv_scan(*(z.reshape(shp) for z in (r, lw, k, v, kk, bb)), _pair_states(s0), _row_tile(t, ROW_TILE))
    tm_out = _row_tile(b * t, ROW_TILE) if tail is None else math.gcd(_row_tile(b * t, ROW_TILE), tail[0].shape[0])
    x_new, route = _rwkv_out(o.reshape(b * t, d), g, bonus, xf, p["w_o"], p["out_vecs"], p["w_router"],
                             p["n_experts"], tail, tm_out)
    return x_new, route, _unpair_states(h_t)


def _moe_layer(x, route, n_prompt, e, w_gate_up, w_down, ln_g, ln_b, tm):
    t, d = x.shape
    experts = jnp.concatenate([route[:, 0], route[:, 1]]).astype(jnp.int32)
    onehot = (experts[:, None] == jnp.arange(e, dtype=jnp.int32)[None, :]).astype(jnp.int32)
    counts = jnp.sum(onehot, axis=0)
    padded = ((counts + tm - 1) // tm) * tm
    ends = jnp.cumsum(padded)
    starts = ends - padded
    rank = jnp.sum(jnp.cumsum(onehot, axis=0) * onehot, axis=1) - 1
    pos = (jnp.sum(starts[None, :] * onehot, axis=1) + rank).astype(jnp.int32)
    n_rows = ((2 * t + e * (tm - 1)) // tm) * tm
    tile_start = jnp.arange(n_rows // tm, dtype=jnp.int32) * tm
    tile_expert = jnp.minimum(jnp.sum((tile_start[:, None] >= ends[None, :]).astype(jnp.int32), axis=1), e - 1)
    n_tiles_used = (ends[-1] // tm).astype(jnp.int32).reshape(1)
    xg = _scatter_rows(x, pos[:t], pos[t:], jnp.zeros((n_rows, d), F32), tm)
    y = _moe_ffn(tile_expert, n_tiles_used, xg, w_gate_up, w_down, tm, _ffn_chunk(w_down.shape[1], cap=2048))
    return _combine_ln(x, route, y, pos[:t], pos[t:], ln_g, ln_b, tm, n_prompt)


def kernel(x_prompt, x_sample, cache_k, cache_v, state_wkv, state_shift, att_w_qkv, att_w_o, ffn_w_gate_up, ffn_w_down, rwkv_mu, rwkv_w_rkv, rwkv_w0, rwkv_w1, rwkv_w2, rwkv_a0, rwkv_a1, rwkv_a2, rwkv_g1, rwkv_g2, rwkv_k_k, rwkv_k_a, rwkv_r_k, rwkv_gn_g, rwkv_gn_b, rwkv_w_o, moe_w_router, moe_w_gate_up, moe_w_down, ln_mix_g, ln_mix_b, ln_ffn_g, ln_ffn_b):
    bp, tp, d = x_prompt.shape
    bs, ts, _ = x_sample.shape
    n_prompt, n_sample = bp * tp, bs * ts
    n_experts = moe_w_router.shape[2]
    bf = lambda a: a.astype(BF16)

    w_qkv, w_o = bf(att_w_qkv[0]), bf(att_w_o[0])
    xp1, k_p, v_p = _attention_layer(x_prompt, w_qkv, w_o, ln_mix_g[0], ln_mix_b[0])
    xs1, k_s, v_s = _attention_layer(x_sample, w_qkv, w_o, ln_mix_g[0], ln_mix_b[0], cache_k[0], cache_v[0])
    w_gu, w_dn = bf(ffn_w_gate_up[0]), bf(ffn_w_down[0])
    tn = _ffn_chunk(w_dn.shape[0], cap=3072)
    xp2 = _ffn_ln(xp1, w_gu, w_dn, ln_ffn_g[0], ln_ffn_b[0], _row_tile(n_prompt, ROW_TILE), tn).reshape(bp, tp, d)
    xs2 = _ffn_ln(xs1, w_gu, w_dn, ln_ffn_g[0], ln_ffn_b[0], _row_tile(n_sample, ROW_TILE), tn).reshape(bs, ts, d)

    p = {
        "mu": rwkv_mu[0], "w_r": bf(rwkv_w_rkv[0, 0]), "w_k": bf(rwkv_w_rkv[0, 1]), "w_v": bf(rwkv_w_rkv[0, 2]),
        "w1": bf(rwkv_w1[0]), "w2": bf(rwkv_w2[0]), "a1": bf(rwkv_a1[0]), "a2": bf(rwkv_a2[0]),
        "g1": bf(rwkv_g1[0]), "g2": bf(rwkv_g2[0]), "w_o": bf(rwkv_w_o[0]),
        "proj_vecs": jnp.stack([rwkv_w0[0], rwkv_a0[0], rwkv_k_k[0], rwkv_k_a[0], rwkv_r_k[0].reshape(d)]),
        "out_vecs": jnp.stack([rwkv_gn_g[0], rwkv_gn_b[0], ln_mix_g[1], ln_mix_b[1]]),
        "w_router": jnp.zeros((d, LANES), F32).at[:, :n_experts].set(moe_w_router[0]), "n_experts": n_experts,
    }
    xs3, route_s, wkv_s = _rwkv_layer(xs2, state_shift[0], state_wkv[0], p, None)
    x3, route, wkv_p = _rwkv_layer(xp2, jnp.zeros((bp, d), F32),
                                   jnp.zeros((bp, d // HEAD_DIM, HEAD_DIM, HEAD_DIM), F32), p, (xs3, route_s))
    y_p, y_s = _moe_layer(x3, route, n_prompt, n_experts, bf(moe_w_gate_up[0]), bf(moe_w_down[0]),
                          ln_ffn_g[1], ln_ffn_b[1], _row_tile(n_sample, ROW_TILE))
    return (y_p.reshape(bp, tp, d), y_s.reshape(bs, ts, d),
            k_p[None], v_p[None], wkv_p[None], xp2[:, -1][None],
            k_s[None], v_s[None], wkv_s[None], xs2[:, -1][None])
```
